```python
import jax, jax.numpy as jnp
from jax import lax
import numpy as np

D_MODEL = 1024
BATCH = 16
SEQ = 256
DEPTH = 4
DEC_BATCH = 4
DEC_SEQ = 1024
PAST_LEN = 512

GRID_W = 64
MIX_WIDTH = D_MODEL
MLA_HEADS = 8
QK_NOPE = 64
QK_ROPE = 32
QK_HEAD = QK_NOPE + QK_ROPE
V_HEAD = 64
Q_LORA = D_MODEL // 4
KV_LORA = D_MODEL // 8
MLA_WIDTH = MLA_HEADS * V_HEAD
CONV_CH = D_MODEL // 4
CONV_K = 31
NA_HEADS = 4
NA_HEAD_DIM = 64
NA_WIDTH = NA_HEADS * NA_HEAD_DIM
NA_WIN_H = 8
NA_WIN_W = 16
N_EXPERTS = 16
N_GROUPS = 4
EXPERTS_PER_GROUP = N_EXPERTS // N_GROUPS
TOP_K = 2
EXPERT_FF = D_MODEL // 2
ROPE_THETA = 10000.0
Q_BLOCK = 128
EPS = 1e-6
NEG_INF = -1e30
MLA_SCALE = QK_HEAD ** -0.5
NA_SCALE = NA_HEAD_DIM ** -0.5
OFF_Q = Q_LORA
OFF_KV = OFF_Q + KV_LORA
OFF_KR = OFF_KV + QK_ROPE
OFF_CONV = OFF_KR + 2 * CONV_CH
IN_COLS = OFF_CONV + 3 * NA_WIDTH
IN_SPLITS = (OFF_Q, OFF_KV, OFF_KR, OFF_CONV)

kernel_name = 'hybrid_mla_conformer_natten_moe_dit_step'


def rms_norm(x, g):
    xf = x.astype(jnp.float32)
    y = xf * lax.rsqrt(jnp.mean(xf * xf, axis=-1, keepdims=True) + EPS)
    return (y * g.astype(jnp.float32)).astype(x.dtype)


def layer_norm(x, g, b):
    xf = x.astype(jnp.float32)
    xc = xf - jnp.mean(xf, axis=-1, keepdims=True)
    var = jnp.mean(xc * xc, axis=-1, keepdims=True)
    return (xc * lax.rsqrt(var + EPS) * g.astype(jnp.float32) + b.astype(jnp.float32)).astype(x.dtype)


def modulation(cond, w_ada, b_ada):
    mod = jnp.dot(jax.nn.silu(cond), w_ada) + b_ada
    return jnp.split(mod, 6, axis=-1)


def axial_rope_tables(n_tokens):
    t = jnp.arange(n_tokens, dtype=jnp.int32)
    row = (t // GRID_W).astype(jnp.float32)
    col = (t % GRID_W).astype(jnp.float32)
    n_freq = QK_ROPE // 4
    inv_freq = ROPE_THETA ** (-jnp.arange(n_freq, dtype=jnp.float32) / n_freq)
    ang = jnp.concatenate([row[:, None] * inv_freq, col[:, None] * inv_freq], axis=-1)
    return jnp.cos(ang), jnp.sin(ang)


def rotate_tail(x, cos, sin):
    xr = x[..., QK_NOPE:].astype(jnp.float32)
    xp = xr.reshape(xr.shape[:-1] + (QK_ROPE // 2, 2))
    x1, x2 = xp[..., 0], xp[..., 1]
    cs = cos[:, None, :]
    sn = sin[:, None, :]
    rot = jnp.stack([x1 * cs - x2 * sn, x1 * sn + x2 * cs], axis=-1).reshape(xr.shape)
    return jnp.concatenate([x[..., :QK_NOPE], rot.astype(x.dtype)], axis=-1)


def blocked_attention(q, k, v, scale):
    bsz, n_q, n_h, d_q = q.shape
    qb = jnp.moveaxis(q.reshape(bsz, n_q // Q_BLOCK, Q_BLOCK, n_h, d_q), 1, 0)

    def one_block(qi):
        s = jnp.einsum('bqhd,bkhd->bhqk', qi, k).astype(jnp.float32) * scale
        p = jax.nn.softmax(s, axis=-1)
        return jnp.einsum('bhqk,bkhd->bqhd', p.astype(v.dtype), v)

    o = lax.map(one_block, qb)
    return jnp.moveaxis(o, 0, 1).reshape(bsz, n_q, n_h, v.shape[-1])


def mla_queries(q_lat, q_a_g, w_q_up, q_norm_g):
    q = jnp.dot(rms_norm(q_lat, q_a_g), w_q_up)
    q = q.reshape(q.shape[:2] + (MLA_HEADS, QK_HEAD))
    return rms_norm(q, q_norm_g)


def mla_keys_values(ckv_n, krope, w_kv_up, k_norm_g):
    kv = jnp.dot(ckv_n, w_kv_up).reshape(ckv_n.shape[:2] + (MLA_HEADS, QK_NOPE + V_HEAD))
    k_nope, v = kv[..., :QK_NOPE], kv[..., QK_NOPE:]
    k_rope = jnp.broadcast_to(krope[:, :, None, :], krope.shape[:2] + (MLA_HEADS, QK_ROPE))
    k = rms_norm(jnp.concatenate([k_nope, k_rope], axis=-1), k_norm_g)
    return k, v


def conformer_conv(u, dw_w, dw_b, ln_g, ln_b):
    a, g = jnp.split(u, 2, axis=-1)
    h = a * jax.nn.sigmoid(g)
    pad = CONV_K // 2
    h = lax.conv_general_dilated(h, dw_w[:, None, :].astype(h.dtype), window_strides=(1,),
                                 padding=[(pad, pad)], dimension_numbers=('NWC', 'WIO', 'NWC'),
                                 feature_group_count=CONV_CH) + dw_b
    return jax.nn.silu(layer_norm(h, ln_g, ln_b))


def na_heads(na_qkv, q_g, k_g):
    q, k, v = jnp.split(na_qkv, 3, axis=-1)
    shp = na_qkv.shape[:2] + (NA_HEADS, NA_HEAD_DIM)
    return rms_norm(q.reshape(shp), q_g), rms_norm(k.reshape(shp), k_g), v.reshape(shp)


def neighbourhood_attention(q, k, v, k_ctx, v_ctx, rpb):
    bsz, n_tok, n_h, d = q.shape
    rows = n_tok // GRID_W
    kh = min(NA_WIN_H, rows)
    r = jnp.arange(rows, dtype=jnp.int32)
    cidx = jnp.arange(GRID_W, dtype=jnp.int32)
    r0 = jnp.clip(r - kh // 2, 0, rows - kh)
    c0 = jnp.clip(cidx - NA_WIN_W // 2, 0, GRID_W - NA_WIN_W)
    row_idx = r0[:, None] + jnp.arange(kh, dtype=jnp.int32)[None, :]
    qg = q.reshape(bsz, rows, GRID_W, n_h, d)
    k_rows = k.reshape(bsz, rows, GRID_W, n_h, d)[:, row_idx]
    v_rows = v.reshape(bsz, rows, GRID_W, n_h, d)[:, row_idx]
    s_loc = jnp.einsum('brqhd,brjkhd->bhrqjk', qg, k_rows).astype(jnp.float32) * NA_SCALE
    dr_idx = row_idx - r[:, None] + (NA_WIN_H - 1)
    dc_idx = jnp.clip(cidx[None, :] - cidx[:, None] + (NA_WIN_W - 1), 0, 2 * NA_WIN_W - 2)
    bias = rpb[:, dr_idx[:, None, :, None], dc_idx[None, :, None, :]]
    in_win = (cidx[None, :] >= c0[:, None]) & (cidx[None, :] < c0[:, None] + NA_WIN_W)
    s_loc = jnp.where(in_win[None, None, None, :, None, :], s_loc + bias[None].astype(jnp.float32), NEG_INF)
    s_ctx = jnp.einsum('brqhd,bkhd->bhrqk', qg, k_ctx).astype(jnp.float32) * NA_SCALE
    n_loc = kh * GRID_W
    s_all = jnp.concatenate([s_loc.reshape(bsz, n_h, rows, GRID_W, n_loc), s_ctx], axis=-1)
    p = jax.nn.softmax(s_all, axis=-1)
    p_loc = p[..., :n_loc].reshape(bsz, n_h, rows, GRID_W, kh, GRID_W).astype(v.dtype)
    p_ctx = p[..., n_loc:].astype(v.dtype)
    o = jnp.einsum('bhrqjk,brjkhd->brqhd', p_loc, v_rows) + jnp.einsum('bhrqk,bkhd->brqhd', p_ctx, v_ctx)
    return o.reshape(bsz, n_tok, n_h, d)


def grouped_moe(h, w_router, b_router, w_gate, w_up, w_down):
    shp = h.shape
    t = h.reshape(-1, D_MODEL)
    n_tok = t.shape[0]
    aff = jax.nn.sigmoid(jnp.dot(t, w_router).astype(jnp.float32))
    sel = (aff + b_router.astype(jnp.float32)).reshape(n_tok, N_GROUPS, EXPERTS_PER_GROUP)
    group_score = jnp.sum(lax.top_k(sel, TOP_K)[0], axis=-1)
    g_idx = jnp.argmax(group_score, axis=-1)
    sel_in_group = jnp.take_along_axis(sel, g_idx[:, None, None], axis=1)[:, 0]
    _, local = lax.top_k(sel_in_group, TOP_K)
    e_idx = g_idx[:, None] * EXPERTS_PER_GROUP + local
    w_sel = jnp.take_along_axis(aff, e_idx, axis=-1)
    w_sel = w_sel / jnp.sum(w_sel, axis=-1, keepdims=True)
    gates = jnp.sum(jax.nn.one_hot(e_idx, N_EXPERTS, dtype=jnp.float32) * w_sel[..., None], axis=1)
    hid = jax.nn.silu(jnp.einsum('td,edf->tef', t, w_gate)) * jnp.einsum('td,edf->tef', t, w_up)
    hid = hid * gates[..., None].astype(hid.dtype)
    return jnp.einsum('tef,efd->td', hid, w_down).reshape(shp)


def mixer_inputs(x, shift, scale, lp):
    h = rms_norm(x, lp['norm1_g']) * (1 + scale) + shift
    proj = jnp.dot(h, lp['w_in'])
    q_lat, ckv_raw, krope, conv_in, na_qkv = jnp.split(proj, IN_SPLITS, axis=-1)
    ckv_n = rms_norm(ckv_raw, lp['kv_a_g'])
    q_mla = mla_queries(q_lat, lp['q_a_g'], lp['w_q_up'], lp['mla_q_norm_g'])
    k_mla, v_mla = mla_keys_values(ckv_n, krope, lp['w_kv_up'], lp['mla_k_norm_g'])
    o_conv = conformer_conv(conv_in, lp['conv_dw_w'], lp['conv_dw_b'], lp['conv_ln_g'], lp['conv_ln_b'])
    q_na, k_na, v_na = na_heads(na_qkv, lp['na_q_norm_g'], lp['na_k_norm_g'])
    return q_mla, k_mla, v_mla, ckv_n, krope, o_conv, q_na, k_na, v_na


def finish_layer(x, o_mla, o_conv, o_na, gate1, shift2, scale2, gate2, lp):
    bsz, n_tok = x.shape[:2]
    mix = jnp.concatenate([o_mla.reshape(bsz, n_tok, MLA_WIDTH), o_conv,
                           o_na.reshape(bsz, n_tok, NA_WIDTH)], axis=-1)
    x = x + gate1 * jnp.dot(mix, lp['w_out'])
    h = rms_norm(x, lp['norm2_g']) * (1 + scale2) + shift2
    return x + gate2 * grouped_moe(h, lp['w_router'], lp['b_router'], lp['w_e_gate'], lp['w_e_up'], lp['w_e_down'])


def setup_inputs(seed: int = 0) -> dict:
    key = jax.random.key(seed)
    ks = jax.random.split(key, 40)
    f32 = jnp.float32

    def nrm(i, shape, scale):
        return jax.random.normal(ks[i], shape, f32) * scale

    def gain(i, shape):
        return 1.0 + 0.05 * jax.random.normal(ks[i], shape, f32)

    return {
        'x_prompt': nrm(0, (BATCH, SEQ, D_MODEL), 1.0),
        'x_sample': nrm(1, (DEC_BATCH, DEC_SEQ, D_MODEL), 1.0),
        'cache_mla_ckv': nrm(2, (DEC_BATCH, DEPTH, PAST_LEN, KV_LORA), 1.0),
        'cache_mla_krope': nrm(3, (DEC_BATCH, DEPTH, PAST_LEN, QK_ROPE), 1.0),
        'cache_na_k': nrm(4, (DEC_BATCH, DEPTH, PAST_LEN, NA_HEADS, NA_HEAD_DIM), 1.0),
        'cache_na_v': nrm(5, (DEC_BATCH, DEPTH, PAST_LEN, NA_HEADS, NA_HEAD_DIM), 1.0),
        'c': nrm(6, (DEC_BATCH, D_MODEL), 1.0),
        'c_ctx': nrm(7, (D_MODEL,), 1.0),
        'w_ada': nrm(8, (DEPTH, D_MODEL, 6 * D_MODEL), 0.5 * D_MODEL ** -0.5),
        'b_ada': nrm(9, (DEPTH, 6 * D_MODEL), 0.02),
        'norm1_g': gain(10, (DEPTH, D_MODEL)),
        'norm2_g': gain(11, (DEPTH, D_MODEL)),
        'w_in': nrm(12, (DEPTH, D_MODEL, IN_COLS), D_MODEL ** -0.5),
        'q_a_g': gain(13, (DEPTH, Q_LORA)),
        'kv_a_g': gain(14, (DEPTH, KV_LORA)),
        'w_q_up': nrm(15, (DEPTH, Q_LORA, MLA_HEADS * QK_HEAD), Q_LORA ** -0.5),
        'w_kv_up': nrm(16, (DEPTH, KV_LORA, MLA_HEADS * (QK_NOPE + V_HEAD)), KV_LORA ** -0.5),
        'mla_q_norm_g': gain(17, (DEPTH, QK_HEAD)),
        'mla_k_norm_g': gain(18, (DEPTH, QK_HEAD)),
        'conv_dw_w': nrm(19, (DEPTH, CONV_K, CONV_CH), CONV_K ** -0.5),
        'conv_dw_b': nrm(20, (DEPTH, CONV_CH), 0.02),
        'conv_ln_g': gain(21, (DEPTH, CONV_CH)),
        'conv_ln_b': nrm(22, (DEPTH, CONV_CH), 0.02),
        'na_q_norm_g': gain(23, (DEPTH, NA_HEAD_DIM)),
        'na_k_norm_g': gain(24, (DEPTH, NA_HEAD_DIM)),
        'na_rpb': nrm(25, (DEPTH, NA_HEADS, 2 * NA_WIN_H - 1, 2 * NA_WIN_W - 1), 0.1),
        'w_out': nrm(26, (DEPTH, MIX_WIDTH, D_MODEL), MIX_WIDTH ** -0.5),
        'w_router': nrm(27, (D_MODEL, N_EXPERTS), D_MODEL ** -0.5),
        'b_router': nrm(28, (N_EXPERTS,), 0.01),
        'w_e_gate': nrm(29, (DEPTH, N_EXPERTS, D_MODEL, EXPERT_FF), D_MODEL ** -0.5),
        'w_e_up': nrm(30, (DEPTH, N_EXPERTS, D_MODEL, EXPERT_FF), D_MODEL ** -0.5),
        'w_e_down': nrm(31, (DEPTH, N_EXPERTS, EXPERT_FF, D_MODEL), EXPERT_FF ** -0.5),
    }


def reference(x_prompt, x_sample, cache_mla_ckv, cache_mla_krope, cache_na_k, cache_na_v, c, c_ctx,
              w_ada, b_ada, norm1_g, norm2_g, w_in, q_a_g, kv_a_g, w_q_up, w_kv_up, mla_q_norm_g,
              mla_k_norm_g, conv_dw_w, conv_dw_b, conv_ln_g, conv_ln_b, na_q_norm_g, na_k_norm_g, na_rpb,
              w_out, w_router, b_router, w_e_gate, w_e_up, w_e_down):
    layers = []
    for l in range(DEPTH):
        layers.append({
            'w_in': w_in[l], 'norm1_g': norm1_g[l], 'norm2_g': norm2_g[l],
            'q_a_g': q_a_g[l], 'kv_a_g': kv_a_g[l], 'w_q_up': w_q_up[l], 'w_kv_up': w_kv_up[l],
            'mla_q_norm_g': mla_q_norm_g[l], 'mla_k_norm_g': mla_k_norm_g[l],
            'conv_dw_w': conv_dw_w[l], 'conv_dw_b': conv_dw_b[l], 'conv_ln_g': conv_ln_g[l], 'conv_ln_b': conv_ln_b[l],
            'na_q_norm_g': na_q_norm_g[l], 'na_k_norm_g': na_k_norm_g[l], 'na_rpb': na_rpb[l],
            'w_out': w_out[l], 'w_router': w_router, 'b_router': b_router,
            'w_e_gate': w_e_gate[l], 'w_e_up': w_e_up[l], 'w_e_down': w_e_down[l],
        })

    xp = x_prompt
    ckv_list, krope_list, nak_list, nav_list = [], [], [], []
    for l in range(DEPTH):
        lp = layers[l]
        sh1, sc1, g1, sh2, sc2, g2 = modulation(c_ctx, w_ada[l], b_ada[l])
        q, k, v, ckv_n, krope, o_conv, qn, kn, vn = mixer_inputs(xp, sh1, sc1, lp)
        o_mla = blocked_attention(q, k, v, MLA_SCALE)
        o_na = blocked_attention(qn, kn, vn, NA_SCALE)
        xp = finish_layer(xp, o_mla, o_conv, o_na, g1, sh2, sc2, g2, lp)
        ckv_list.append(ckv_n)
        krope_list.append(krope)
        nak_list.append(kn)
        nav_list.append(vn)

    xs = x_sample
    cos, sin = axial_rope_tables(x_sample.shape[1])
    for l in range(DEPTH):
        lp = layers[l]
        sh1, sc1, g1, sh2, sc2, g2 = [m[:, None, :] for m in modulation(c, w_ada[l], b_ada[l])]
        q, k, v, _, _, o_conv, qn, kn, vn = mixer_inputs(xs, sh1, sc1, lp)
        q = rotate_tail(q, cos, sin)
        k = rotate_tail(k, cos, sin)
        k_ctx, v_ctx = mla_keys_values(cache_mla_ckv[:, l], cache_mla_krope[:, l], lp['w_kv_up'], lp['mla_k_norm_g'])
        o_mla = blocked_attention(q, jnp.concatenate([k, k_ctx], axis=1), jnp.concatenate([v, v_ctx], axis=1), MLA_SCALE)
        o_na = neighbourhood_attention(qn, kn, vn, cache_na_k[:, l], cache_na_v[:, l], lp['na_rpb'])
        xs = finish_layer(xs, o_mla, o_conv, o_na, g1, sh2, sc2, g2, lp)

    return (xp, xs, jnp.stack(ckv_list, axis=1), jnp.stack(krope_list, axis=1),
            jnp.stack(nak_list, axis=1), jnp.stack(nav_list, axis=1))
```

```python
import functools

import numpy as np
import jax
import jax.numpy as jnp
from jax import lax
from jax.experimental import pallas as pl
from jax.experimental.pallas import tpu as pltpu

F32 = jnp.float32
BF16 = jnp.bfloat16

D = 1024
N_PROMPT = 16
PROMPT_LEN = 256
N_SAMPLE = 4
SAMPLE_LEN = 1024
DEPTH = 4
PAST = 512
GRID_W = 64
GRID_ROWS = SAMPLE_LEN // GRID_W
MLA_HEADS = 8
QK_NOPE = 64
QK_ROPE = 32
QK_HEAD = QK_NOPE + QK_ROPE
V_HEAD = 64
Q_LORA = 256
KV_LORA = 128
CONV_CH = 256
CONV_K = 31
NA_HEADS = 4
NA_DIM = 64
NA_WIDTH = NA_HEADS * NA_DIM
NA_WIN_H = 8
NA_WIN_W = 16
N_EXPERTS = 16
N_GROUPS = 4
EPG = 4
EXPERT_FF = 512
ROPE_THETA = 10000.0
EPS = 1e-6
NEG_INF = -1e30
MLA_SCALE = QK_HEAD ** -0.5
NA_SCALE = NA_DIM ** -0.5

LANES = 128
TM = 256
T_PROMPT = N_PROMPT * PROMPT_LEN
T_SAMPLE = N_SAMPLE * SAMPLE_LEN
T_ALL = T_PROMPT + T_SAMPLE
N_TILES = T_ALL // TM
PROMPT_TILES = T_PROMPT // TM
TILES_PER_SAMPLE = SAMPLE_LEN // TM
N_COND = 1 + N_SAMPLE
COND_ROWS = 8
MOD_W = 6 * D

COL_Q = 0
COL_KV = Q_LORA
COL_KR = COL_KV + KV_LORA
COL_CONV = COL_KR + LANES
COL_NA = COL_CONV + 2 * CONV_CH
IN_COLS_P = COL_NA + 3 * NA_WIDTH

NA_KEY_ROWS = 12
NA_KEYS = NA_KEY_ROWS * GRID_W

VMEM_LIMIT = 56 * 1024 * 1024

MOE_TM = 512
MOE_HALVES = 2
MOE_TILES = T_ALL // MOE_TM // MOE_HALVES


def _cparams(sem):
    return pltpu.CompilerParams(dimension_semantics=sem, vmem_limit_bytes=VMEM_LIMIT)


def _mod_row(i):
    return jnp.where(i < PROMPT_TILES, 0, 1 + (i - PROMPT_TILES) // TILES_PER_SAMPLE)


def _rope_block(i):
    return jnp.where(i < PROMPT_TILES, 0, 1 + (i - PROMPT_TILES) % TILES_PER_SAMPLE)


def _dot(a, b):
    return jnp.dot(a, b, preferred_element_type=F32)


def _dot_t(a, b):
    return lax.dot_general(a, b, (((1,), (1,)), ((), ())), preferred_element_type=F32)


def _silu(x):
    return x * jax.nn.sigmoid(x)


def _lane_is_low(shape):
    return lax.broadcasted_iota(jnp.int32, shape, len(shape) - 1) < (LANES // 2)


MOD_BLK = 768


def _mod_kernel(ct_ref, w_ref, b_ref, o_ref, cb_ref):
    first = (pl.program_id(0) == 0) & (pl.program_id(1) == 0)

    @pl.when(first)
    def _():
        s = _silu(ct_ref[...])
        for m in range(N_COND):
            cb_ref[m] = jnp.broadcast_to(s[:, m:m + 1], (D, LANES))

    for j in range(MOD_BLK // LANES):
        w = w_ref[0, :, j * LANES:(j + 1) * LANES]
        rows = [jnp.sum(w * cb_ref[m], axis=0, keepdims=True) for m in range(N_COND)]
        rows.append(jnp.zeros((COND_ROWS - N_COND, LANES), F32))
        o_ref[0, :, j * LANES:(j + 1) * LANES] = (
            jnp.concatenate(rows, axis=0) + b_ref[0, :, j * LANES:(j + 1) * LANES])


def _modulation(cond_t, w_ada, b_ada):
    nb = MOD_W // MOD_BLK
    return pl.pallas_call(
        _mod_kernel,
        grid=(DEPTH, nb),
        in_specs=[
            pl.BlockSpec((D, COND_ROWS), lambda l, j: (0, 0)),
            pl.BlockSpec((1, D, MOD_BLK), lambda l, j: (l, 0, j)),
            pl.BlockSpec((1, 1, MOD_BLK), lambda l, j: (l, 0, j)),
        ],
        out_specs=pl.BlockSpec((1, COND_ROWS, MOD_BLK), lambda l, j: (l, 0, j)),
        out_shape=jax.ShapeDtypeStruct((DEPTH, COND_ROWS, MOD_W), F32),
        scratch_shapes=[pltpu.VMEM((N_COND, D, LANES), F32)],
        compiler_params=_cparams(("arbitrary", "arbitrary")),
        name="modulation",
    )(cond_t, w_ada, b_ada.reshape(DEPTH, 1, MOD_W))


def _rms(x, width):
    return x * lax.rsqrt(jnp.sum(x * x, axis=-1, keepdims=True) * (1.0 / width) + EPS)


def _rope(x, cos, sin):
    lane = lax.broadcasted_iota(jnp.int32, x.shape, 1)
    partner = jnp.where((lane & 1) == 0, pltpu.roll(x, LANES - 1, 1), pltpu.roll(x, 1, 1))
    return x * cos + partner * sin


def _pair_rms(x, gain):
    low = _lane_is_low(x.shape)
    sq = x * x
    tot = jnp.sum(sq, axis=-1, keepdims=True)
    lo = jnp.sum(jnp.where(low, sq, 0.0), axis=-1, keepdims=True)
    ms = jnp.where(low, lo, tot - lo) * (1.0 / NA_DIM)
    return x * lax.rsqrt(ms + EPS) * gain


def _softmax_pv(scores, values):
    m = scores[0].max(axis=-1, keepdims=True)
    for s in scores[1:]:
        m = jnp.maximum(m, s.max(axis=-1, keepdims=True))
    acc = None
    den = None
    for s, v in zip(scores, values):
        p = jnp.exp(s - m)
        d = jnp.sum(p, axis=-1, keepdims=True)
        o = _dot(p.astype(BF16), v)
        acc = o if acc is None else acc + o
        den = d if den is None else den + d
    return acc * (1.0 / den)


def _pre_kernel(has_moe, *refs):
    if has_moe:
        x1_ref, y_ref, modp_ref = refs[:3]
        refs = refs[3:]
    else:
        x_ref = refs[0]
        refs = refs[1:]
    (mod_ref, n1g_ref, win_ref, qag_ref, kvag_ref, wq_ref, wk_ref, wv_ref, qng_ref, kng_ref,
     naqg_ref, nakg_ref, cos_ref, sin_ref) = refs[:14]
    refs = refs[14:]
    if has_moe:
        xo_ref = refs[0]
        refs = refs[1:]
    (q_out, k_out, v_out, ckv_out, kr_out, u_out, naq_out, nak_out, nav_out, nakf_out, navf_out,
     win_bf, wq_bf, wk_bf, wv_bf) = refs

    @pl.when(pl.program_id(0) == 0)
    def _():
        win_bf[...] = win_ref[0].astype(BF16)
        wq_bf[...] = wq_ref[0].astype(BF16)
        wk_bf[...] = wk_ref[0].astype(BF16)
        wv_bf[...] = wv_ref[0].astype(BF16)

    if has_moe:
        gate2 = modp_ref[0, :, 5 * D:6 * D]
        x = x1_ref[...] + gate2 * y_ref[...]
        xo_ref[...] = x
    else:
        x = x_ref[...]
    shift = mod_ref[0, :, 0:D]
    scale = mod_ref[0, :, D:2 * D]
    h = _rms(x, D) * n1g_ref[0] * (1.0 + scale) + shift
    proj = _dot(h.astype(BF16), win_bf[...])

    cos = cos_ref[...]
    sin = sin_ref[...]

    q_lat = _rms(proj[:, COL_Q:COL_Q + Q_LORA], Q_LORA) * qag_ref[0]
    q = _dot(q_lat.astype(BF16), wq_bf[...])
    for hd in range(MLA_HEADS):
        qh = q[:, hd * LANES:(hd + 1) * LANES]
        qh = _rope(_rms(qh, QK_HEAD) * qng_ref[0], cos, sin)
        q_out[hd] = (qh * MLA_SCALE).astype(BF16)

    ckv_n = _rms(proj[:, COL_KV:COL_KV + KV_LORA], KV_LORA) * kvag_ref[0]
    ckv_out[...] = ckv_n
    kr = proj[:, COL_KR:COL_KR + LANES]
    kr_out[...] = kr
    ckv_b = ckv_n.astype(BF16)
    k_nope = _dot(ckv_b, wk_bf[...])
    for hd in range(MLA_HEADS):
        kh = k_nope[:, hd * LANES:(hd + 1) * LANES] + kr
        kh = _rope(_rms(kh, QK_HEAD) * kng_ref[0], cos, sin)
        k_out[hd] = kh.astype(BF16)
    v = _dot(ckv_b, wv_bf[...])
    for p in range(MLA_HEADS // 2):
        v_out[p] = v[:, p * LANES:(p + 1) * LANES].astype(BF16)

    a = proj[:, COL_CONV:COL_CONV + CONV_CH]
    g = proj[:, COL_CONV + CONV_CH:COL_CONV + 2 * CONV_CH]
    u_out[...] = a * jax.nn.sigmoid(g)

    for p in range(NA_HEADS // 2):
        sl = slice(p * LANES, (p + 1) * LANES)
        qn = _pair_rms(proj[:, COL_NA + p * LANES:COL_NA + (p + 1) * LANES], naqg_ref[0])
        kn = _pair_rms(proj[:, COL_NA + NA_WIDTH + p * LANES:COL_NA + NA_WIDTH + (p + 1) * LANES],
                       nakg_ref[0])
        vn = proj[:, COL_NA + 2 * NA_WIDTH + p * LANES:COL_NA + 2 * NA_WIDTH + (p + 1) * LANES]
        naq_out[:, sl] = (qn * NA_SCALE).astype(BF16)
        nak_out[:, sl] = kn.astype(BF16)
        nav_out[:, sl] = vn.astype(BF16)
        nakf_out[:, sl] = kn
        navf_out[:, sl] = vn


def _pre(layer, has_moe, x_or_parts, mods3, prm):
    row = lambda i: (i, 0)
    const2 = lambda i: (0, 0)
    lay3 = lambda i: (layer, 0, 0)
    in_specs = []
    args = []
    if has_moe:
        x1, y = x_or_parts
        in_specs += [pl.BlockSpec((TM, D), row), pl.BlockSpec((TM, D), row),
                     pl.BlockSpec((1, 1, MOD_W), lambda i: ((layer - 1) * COND_ROWS + _mod_row(i), 0, 0))]
        args += [x1, y, mods3]
    else:
        in_specs += [pl.BlockSpec((TM, D), row)]
        args += [x_or_parts]
    in_specs += [
        pl.BlockSpec((1, 1, MOD_W), lambda i: (layer * COND_ROWS + _mod_row(i), 0, 0)),
        pl.BlockSpec((1, 1, D), lay3),
        pl.BlockSpec((1, D, IN_COLS_P), lay3),
        pl.BlockSpec((1, 1, Q_LORA), lay3),
        pl.BlockSpec((1, 1, KV_LORA), lay3),
        pl.BlockSpec((1, Q_LORA, MLA_HEADS * LANES), lay3),
        pl.BlockSpec((1, KV_LORA, MLA_HEADS * LANES), lay3),
        pl.BlockSpec((1, KV_LORA, MLA_HEADS * V_HEAD), lay3),
        pl.BlockSpec((1, 1, LANES), lay3),
        pl.BlockSpec((1, 1, LANES), lay3),
        pl.BlockSpec((1, 1, LANES), lay3),
        pl.BlockSpec((1, 1, LANES), lay3),
        pl.BlockSpec((TM, LANES), lambda i: (_rope_block(i), 0)),
        pl.BlockSpec((TM, LANES), lambda i: (_rope_block(i), 0)),
    ]
    args += [mods3, prm["norm1_g"], prm["w_in"], prm["q_a_g"], prm["kv_a_g"], prm["w_q"], prm["w_k"],
             prm["w_v"], prm["q_norm_g"], prm["k_norm_g"], prm["na_q_g"], prm["na_k_g"],
             prm["cos"], prm["sin"]]
    out_specs = []
    out_shape = []
    if has_moe:
        out_specs.append(pl.BlockSpec((TM, D), row))
        out_shape.append(jax.ShapeDtypeStruct((T_ALL, D), F32))
    head3 = lambda n: pl.BlockSpec((n, TM, LANES), lambda i: (0, i, 0))
    out_specs += [head3(MLA_HEADS), head3(MLA_HEADS), head3(MLA_HEADS // 2),
                  pl.BlockSpec((TM, KV_LORA), row), pl.BlockSpec((TM, LANES), row),
                  pl.BlockSpec((TM, CONV_CH), row)] + [pl.BlockSpec((TM, NA_WIDTH), row)] * 5
    out_shape += [
        jax.ShapeDtypeStruct((MLA_HEADS, T_ALL, LANES), BF16),
        jax.ShapeDtypeStruct((MLA_HEADS, T_ALL, LANES), BF16),
        jax.ShapeDtypeStruct((MLA_HEADS // 2, T_ALL, LANES), BF16),
        jax.ShapeDtypeStruct((T_ALL, KV_LORA), F32),
        jax.ShapeDtypeStruct((T_ALL, LANES), F32),
        jax.ShapeDtypeStruct((T_ALL, CONV_CH), F32),
        jax.ShapeDtypeStruct((T_ALL, NA_WIDTH), BF16),
        jax.ShapeDtypeStruct((T_ALL, NA_WIDTH), BF16),
        jax.ShapeDtypeStruct((T_ALL, NA_WIDTH), BF16),
        jax.ShapeDtypeStruct((T_ALL, NA_WIDTH), F32),
        jax.ShapeDtypeStruct((T_ALL, NA_WIDTH), F32),
    ]
    return pl.pallas_call(
        functools.partial(_pre_kernel, has_moe),
        grid=(N_TILES,),
        in_specs=in_specs,
        out_specs=out_specs,
        out_shape=out_shape,
        scratch_shapes=[pltpu.VMEM((D, IN_COLS_P), BF16), pltpu.VMEM((Q_LORA, MLA_HEADS * LANES), BF16),
                        pltpu.VMEM((KV_LORA, MLA_HEADS * LANES), BF16),
                        pltpu.VMEM((KV_LORA, MLA_HEADS * V_HEAD), BF16)],
        compiler_params=_cparams(("arbitrary",)),
        name="pre_l%d" % layer,
    )(*args)


def _ctx_kernel(ckv_ref, kr_ref, wk_ref, wv_ref, kng_ref, kc_out, vc_out):
    ckv = ckv_ref[0, 0].astype(BF16)
    kr = kr_ref[0, 0]
    k_nope = _dot(ckv, wk_ref[0].astype(BF16))
    for hd in range(MLA_HEADS):
        kh = k_nope[:, hd * LANES:(hd + 1) * LANES] + kr
        kc_out[0, 0, hd] = (_rms(kh, QK_HEAD) * kng_ref[0]).astype(BF16)
    v = _dot(ckv, wv_ref[0].astype(BF16))
    for p in range(MLA_HEADS // 2):
        vc_out[0, 0, p] = v[:, p * LANES:(p + 1) * LANES].astype(BF16)


def _ctx_kv(cache_ckv, kr_slab, prm):
    return pl.pallas_call(
        _ctx_kernel,
        grid=(DEPTH, N_SAMPLE),
        in_specs=[
            pl.BlockSpec((1, 1, PAST, KV_LORA), lambda l, b: (b, l, 0, 0)),
            pl.BlockSpec((1, 1, PAST, LANES), lambda l, b: (b, l, 0, 0)),
            pl.BlockSpec((1, KV_LORA, MLA_HEADS * LANES), lambda l, b: (l, 0, 0)),
            pl.BlockSpec((1, KV_LORA, MLA_HEADS * V_HEAD), lambda l, b: (l, 0, 0)),
            pl.BlockSpec((1, 1, LANES), lambda l, b: (l, 0, 0)),
        ],
        out_specs=[
            pl.BlockSpec((1, 1, MLA_HEADS, PAST, LANES), lambda l, b: (l, b, 0, 0, 0)),
            pl.BlockSpec((1, 1, MLA_HEADS // 2, PAST, LANES), lambda l, b: (l, b, 0, 0, 0)),
        ],
        out_shape=[
            jax.ShapeDtypeStruct((DEPTH, N_SAMPLE, MLA_HEADS, PAST, LANES), BF16),
            jax.ShapeDtypeStruct((DEPTH, N_SAMPLE, MLA_HEADS // 2, PAST, LANES), BF16),
        ],
        compiler_params=_cparams(("arbitrary", "arbitrary")),
        name="ctx_kv",
    )(cache_ckv, kr_slab, prm["w_k"], prm["w_v"], prm["k_norm_g"])


N_DR = 2 * NA_WIN_H - 1
N_DC = 2 * NA_WIN_W - 1


def _bias_kernel(rpb_ref, o_ref):
    cq = lax.broadcasted_iota(jnp.int32, (GRID_W, GRID_W), 0)
    ck = lax.broadcasted_iota(jnp.int32, (GRID_W, GRID_W), 1)
    dc = jnp.clip(ck - cq + (NA_WIN_W - 1), 0, N_DC - 1)
    c0 = jnp.clip(cq - NA_WIN_W // 2, 0, GRID_W - NA_WIN_W)
    in_win = (ck >= c0) & (ck < c0 + NA_WIN_W)
    for dr in range(N_DR):
        acc = jnp.zeros((GRID_W, GRID_W), F32)
        for j in range(N_DC):
            acc = jnp.where(dc == j, rpb_ref[pl.program_id(0), pl.program_id(1), dr, j], acc)
        o_ref[0, 0, dr] = jnp.where(in_win, acc, NEG_INF)
    o_ref[0, 0, N_DR] = jnp.full((GRID_W, GRID_W), NEG_INF, F32)


def _na_bias_blocks(na_rpb):
    return pl.pallas_call(
        _bias_kernel,
        grid=(DEPTH, NA_HEADS),
        in_specs=[pl.BlockSpec(memory_space=pltpu.SMEM)],
        out_specs=pl.BlockSpec((1, 1, N_DR + 1, GRID_W, GRID_W), lambda l, h: (l, h, 0, 0, 0)),
        out_shape=jax.ShapeDtypeStruct((DEPTH, NA_HEADS, N_DR + 1, GRID_W, GRID_W), F32),
        compiler_params=_cparams(("arbitrary", "arbitrary")),
        name="na_bias",
    )(na_rpb)


def _na_block_index():
    idx = np.full((TILES_PER_SAMPLE, TM // GRID_W, NA_KEY_ROWS), N_DR, np.int32)
    for i in range(TILES_PER_SAMPLE):
        ks = _na_key_start_row(i)
        for rq in range(TM // GRID_W):
            r = i * (TM // GRID_W) + rq
            r0 = min(max(r - NA_WIN_H // 2, 0), GRID_ROWS - NA_WIN_H)
            for rk in range(NA_KEY_ROWS):
                kr = ks + rk
                if r0 <= kr < r0 + NA_WIN_H:
                    idx[i, rq, rk] = kr - r + (NA_WIN_H - 1)
    return idx


def _na_key_start_row(i):
    return 0 if i < TILES_PER_SAMPLE // 2 else GRID_ROWS - NA_KEY_ROWS


def _prompt_attn_kernel(q_ref, k_ref, v_ref, nq_ref, nk_ref, nv_ref, om_ref, on_ref):
    low = _lane_is_low((PROMPT_LEN, LANES))
    for p in range(MLA_HEADS // 2):
        outs = []
        for hh in range(2):
            hd = 2 * p + hh
            s = _dot_t(q_ref[hd], k_ref[hd])
            outs.append(_softmax_pv([s], [v_ref[p]]))
        om_ref[p] = jnp.where(low, outs[0], outs[1]).astype(BF16)
    for p in range(NA_HEADS // 2):
        sl = slice(p * LANES, (p + 1) * LANES)
        qp = nq_ref[:, sl]
        kp = nk_ref[:, sl]
        vp = nv_ref[:, sl]
        outs = []
        for hh in range(2):
            qm = jnp.where(low if hh == 0 else ~low, qp, jnp.zeros_like(qp))
            outs.append(_softmax_pv([_dot_t(qm, kp)], [vp]))
        on_ref[:, sl] = jnp.where(low, outs[0], outs[1]).astype(BF16)


def _prompt_attn(q, k, v, nq, nk, nv):
    head3 = lambda n: pl.BlockSpec((n, PROMPT_LEN, LANES), lambda b: (0, b, 0))
    row = pl.BlockSpec((PROMPT_LEN, NA_WIDTH), lambda b: (b, 0))
    return pl.pallas_call(
        _prompt_attn_kernel,
        grid=(N_PROMPT,),
        in_specs=[head3(MLA_HEADS), head3(MLA_HEADS), head3(MLA_HEADS // 2), row, row, row],
        out_specs=[head3(MLA_HEADS // 2), row],
        out_shape=[jax.ShapeDtypeStruct((MLA_HEADS // 2, T_PROMPT, LANES), BF16),
                   jax.ShapeDtypeStruct((T_PROMPT, NA_WIDTH), BF16)],
        compiler_params=_cparams(("arbitrary",)),
        name="prompt_attn",
    )(q, k, v, nq, nk, nv)


def _sample_mla_kernel(q_ref, k_ref, v_ref, kc_ref, vc_ref, o_ref):
    low = _lane_is_low((TM, LANES))
    outs = []
    for hh in range(2):
        q = q_ref[hh]
        s_lat = _dot_t(q, k_ref[hh])
        s_ctx = _dot_t(q, kc_ref[0, 0, hh])
        outs.append(_softmax_pv([s_lat, s_ctx], [v_ref[0], vc_ref[0, 0, 0]]))
    o_ref[0] = jnp.where(low, outs[0], outs[1]).astype(BF16)


def _sample_mla(layer, q, k, v, kc, vc):
    seq_blk0 = T_PROMPT // SAMPLE_LEN
    return pl.pallas_call(
        _sample_mla_kernel,
        grid=(N_SAMPLE, MLA_HEADS // 2, TILES_PER_SAMPLE),
        in_specs=[
            pl.BlockSpec((2, TM, LANES), lambda b, p, i: (p, PROMPT_TILES + b * TILES_PER_SAMPLE + i, 0)),
            pl.BlockSpec((2, SAMPLE_LEN, LANES), lambda b, p, i: (p, seq_blk0 + b, 0)),
            pl.BlockSpec((1, SAMPLE_LEN, LANES), lambda b, p, i: (p, seq_blk0 + b, 0)),
            pl.BlockSpec((1, 1, 2, PAST, LANES), lambda b, p, i: (layer, b, p, 0, 0)),
            pl.BlockSpec((1, 1, 1, PAST, LANES), lambda b, p, i: (layer, b, p, 0, 0)),
        ],
        out_specs=pl.BlockSpec((1, TM, LANES), lambda b, p, i: (p, b * TILES_PER_SAMPLE + i, 0)),
        out_shape=jax.ShapeDtypeStruct((MLA_HEADS // 2, T_SAMPLE, LANES), BF16),
        compiler_params=_cparams(("arbitrary", "arbitrary", "arbitrary")),
        name="sample_mla",
    )(q, k, v, kc, vc)


def _sample_na_kernel(q_ref, k_ref, v_ref, kc_ref, vc_ref, b_ref, o_ref):
    i = pl.program_id(2)
    start = pl.multiple_of(
        jnp.where(i < TILES_PER_SAMPLE // 2, 0, (GRID_ROWS - NA_KEY_ROWS) * GRID_W), TM)
    k_win = k_ref[pl.ds(start, NA_KEYS), :]
    v_win = v_ref[pl.ds(start, NA_KEYS), :]
    kc = kc_ref[0, 0].astype(BF16)
    vc = vc_ref[0, 0].astype(BF16)
    q = q_ref[...]
    low = _lane_is_low((TM, LANES))
    outs = []
    for hh in range(2):
        qm = jnp.where(low if hh == 0 else ~low, q, jnp.zeros_like(q))
        s_loc = _dot_t(qm, k_win) + b_ref[0, hh, 0]
        s_ctx = _dot_t(qm, kc)
        outs.append(_softmax_pv([s_loc, s_ctx], [v_win, vc]))
    o_ref[...] = jnp.where(low, outs[0], outs[1]).astype(BF16)


def _sample_na(layer, nq, nk, nv, cache_k, cache_v, bias):
    seq_blk0 = T_PROMPT // SAMPLE_LEN
    return pl.pallas_call(
        _sample_na_kernel,
        grid=(N_SAMPLE, NA_HEADS // 2, TILES_PER_SAMPLE),
        in_specs=[
            pl.BlockSpec((TM, LANES), lambda b, p, i: (PROMPT_TILES + b * TILES_PER_SAMPLE + i, p)),
            pl.BlockSpec((SAMPLE_LEN, LANES), lambda b, p, i: (seq_blk0 + b, p)),
            pl.BlockSpec((SAMPLE_LEN, LANES), lambda b, p, i: (seq_blk0 + b, p)),
            pl.BlockSpec((1, 1, PAST, LANES), lambda b, p, i: (b, layer, 0, p)),
            pl.BlockSpec((1, 1, PAST, LANES), lambda b, p, i: (b, layer, 0, p)),
            pl.BlockSpec((1, 2, 1, TM, NA_KEYS), lambda b, p, i: (layer, p, i, 0, 0)),
        ],
        out_specs=pl.BlockSpec((TM, LANES), lambda b, p, i: (b * TILES_PER_SAMPLE + i, p)),
        out_shape=jax.ShapeDtypeStruct((T_SAMPLE, NA_WIDTH), BF16),
        compiler_params=_cparams(("arbitrary", "arbitrary", "arbitrary")),
        name="sample_na",
    )(nq, nk, nv, cache_k, cache_v, bias)


CONV_PAD = 16


def _argmax_first(cols):
    best = cols[0]
    idx = jnp.zeros(best.shape, jnp.int32)
    for j in range(1, len(cols)):
        upd = cols[j] > best
        idx = jnp.where(upd, j, idx)
        best = jnp.where(upd, cols[j], best)
    return idx


def _route(aff, bias_row):
    sel = aff + bias_row
    cols = [sel[:, e:e + 1] for e in range(N_EXPERTS)]
    group_scores = []
    for g in range(N_GROUPS):
        c = cols[g * EPG:(g + 1) * EPG]
        best = None
        for a in range(EPG):
            for b in range(a + 1, EPG):
                s = c[a] + c[b]
                best = s if best is None else jnp.maximum(best, s)
        group_scores.append(best)
    g_idx = _argmax_first(group_scores)
    in_group = []
    for j in range(EPG):
        v = cols[j]
        for g in range(1, N_GROUPS):
            v = jnp.where(g_idx == g, cols[g * EPG + j], v)
        in_group.append(v)
    i1 = _argmax_first(in_group)
    masked = [jnp.where(i1 == j, -jnp.inf, in_group[j]) for j in range(EPG)]
    i2 = _argmax_first(masked)
    e1 = g_idx * EPG + i1
    e2 = g_idx * EPG + i2
    lane = lax.broadcasted_iota(jnp.int32, aff.shape, 1)
    w1 = jnp.sum(jnp.where(lane == e1, aff, 0.0), axis=-1, keepdims=True)
    w2 = jnp.sum(jnp.where(lane == e2, aff, 0.0), axis=-1, keepdims=True)
    tot = w1 + w2
    return jnp.where(lane == e1, w1 / tot, 0.0) + jnp.where(lane == e2, w2 / tot, 0.0)


def _post_kernel(x_ref, om_ref, up_ref, uc_ref, un_ref, on_ref, mod_ref, cw_ref, cb_ref, lg_ref, lb_ref,
                 wo_ref, n2g_ref, wr_ref, br_ref, x1_out, h2_out, gates_out, wo_bf, ext_ref):
    i = pl.program_id(0)

    @pl.when(i == 0)
    def _():
        wo_bf[...] = wo_ref[0].astype(BF16)

    pos_in_seq = (i - PROMPT_TILES) % TILES_PER_SAMPLE
    is_first = (i < PROMPT_TILES) | (pos_in_seq == 0)
    is_last = (i < PROMPT_TILES) | (pos_in_seq == TILES_PER_SAMPLE - 1)
    ext_ref[0:CONV_PAD, :] = jnp.where(is_first, 0.0, up_ref[TM - CONV_PAD:TM, :])
    ext_ref[CONV_PAD:CONV_PAD + TM, :] = uc_ref[...]
    ext_ref[CONV_PAD + TM:2 * CONV_PAD + TM, :] = jnp.where(is_last, 0.0, un_ref[0:CONV_PAD, :])
    acc = jnp.zeros((TM, CONV_CH), F32)
    for j in range(CONV_K):
        off = CONV_PAD - CONV_K // 2 + j
        acc = acc + ext_ref[off:off + TM, :] * cw_ref[0, j:j + 1, :]
    conv = acc + cb_ref[0]
    cc = conv - jnp.mean(conv, axis=-1, keepdims=True)
    var = jnp.mean(cc * cc, axis=-1, keepdims=True)
    o_conv = _silu(cc * lax.rsqrt(var + EPS) * lg_ref[0] + lb_ref[0])

    mix = jnp.concatenate([om_ref[p] for p in range(MLA_HEADS // 2)]
                          + [o_conv.astype(BF16), on_ref[...]], axis=-1)
    att = _dot(mix, wo_bf[...])
    gate1 = mod_ref[0, :, 2 * D:3 * D]
    x1 = x_ref[...] + gate1 * att
    x1_out[...] = x1
    shift2 = mod_ref[0, :, 3 * D:4 * D]
    scale2 = mod_ref[0, :, 4 * D:5 * D]
    h2 = _rms(x1, D) * n2g_ref[0] * (1.0 + scale2) + shift2
    h2_out[...] = h2.astype(BF16)
    logits = jnp.dot(h2, wr_ref[...], preferred_element_type=F32, precision=lax.Precision.HIGHEST)
    gates_out[...] = _route(jax.nn.sigmoid(logits), br_ref[...])


def _post(layer, x, o_mla, u, o_na, mods3, prm):
    row = lambda i: (i, 0)
    lay3 = lambda i: (layer, 0, 0)
    return pl.pallas_call(
        _post_kernel,
        grid=(N_TILES,),
        in_specs=[
            pl.BlockSpec((TM, D), row),
            pl.BlockSpec((MLA_HEADS // 2, TM, LANES), lambda i: (0, i, 0)),
            pl.BlockSpec((TM, CONV_CH), lambda i: (jnp.maximum(i - 1, 0), 0)),
            pl.BlockSpec((TM, CONV_CH), row),
            pl.BlockSpec((TM, CONV_CH), lambda i: (jnp.minimum(i + 1, N_TILES - 1), 0)),
            pl.BlockSpec((TM, NA_WIDTH), row),
            pl.BlockSpec((1, 1, MOD_W), lambda i: (layer * COND_ROWS + _mod_row(i), 0, 0)),
            pl.BlockSpec((1, CONV_K, CONV_CH), lay3),
            pl.BlockSpec((1, 1, CONV_CH), lay3),
            pl.BlockSpec((1, 1, CONV_CH), lay3),
            pl.BlockSpec((1, 1, CONV_CH), lay3),
            pl.BlockSpec((1, D, D), lay3),
            pl.BlockSpec((1, 1, D), lay3),
            pl.BlockSpec((D, N_EXPERTS), lambda i: (0, 0)),
            pl.BlockSpec((1, N_EXPERTS), lambda i: (0, 0)),
        ],
        out_specs=[pl.BlockSpec((TM, D), row), pl.BlockSpec((TM, D), row),
                   pl.BlockSpec((TM, N_EXPERTS), row)],
        out_shape=[jax.ShapeDtypeStruct((T_ALL, D), F32), jax.ShapeDtypeStruct((T_ALL, D), BF16),
                   jax.ShapeDtypeStruct((T_ALL, N_EXPERTS), F32)],
        scratch_shapes=[pltpu.VMEM((D, D), BF16), pltpu.VMEM((TM + 2 * CONV_PAD, CONV_CH), F32)],
        compiler_params=_cparams(("arbitrary",)),
        name="post_l%d" % layer,
    )(x, o_mla, u, u, u, o_na, mods3, prm["conv_w"], prm["conv_b"], prm["conv_ln_g"], prm["conv_ln_b"],
      prm["w_out"], prm["norm2_g"], prm["w_router"], prm["b_router"])


def _moe_kernel(h_ref, g_ref, wg_ref, wu_ref, wd_ref, o_ref, acc_ref, wg_bf, wu_bf, wd_bf):
    e = pl.program_id(1)
    t = pl.program_id(2)

    @pl.when(t == 0)
    def _():
        wg_bf[...] = wg_ref[0, 0].astype(BF16)
        wu_bf[...] = wu_ref[0, 0].astype(BF16)
        wd_bf[...] = wd_ref[0, 0].astype(BF16)

    h = h_ref[...]
    lane = lax.broadcasted_iota(jnp.int32, (MOE_TM, N_EXPERTS), 1)
    gate = jnp.sum(jnp.where(lane == e, g_ref[...], 0.0), axis=-1, keepdims=True)
    hid = _silu(_dot(h, wg_bf[...])) * _dot(h, wu_bf[...]) * gate
    y = _dot(hid.astype(BF16), wd_bf[...])

    @pl.when(e == 0)
    def _():
        acc_ref[t] = y

    @pl.when((e > 0) & (e < N_EXPERTS - 1))
    def _():
        acc_ref[t] = acc_ref[t] + y

    @pl.when(e == N_EXPERTS - 1)
    def _():
        o_ref[...] = acc_ref[t] + y


def _moe(layer, h2, gates, prm):
    tok = lambda hf, e, t: (hf * MOE_TILES + t, 0)
    out_idx = lambda hf, e, t: (hf * MOE_TILES + jnp.where(e == N_EXPERTS - 1, t, 0), 0)
    return pl.pallas_call(
        _moe_kernel,
        grid=(MOE_HALVES, N_EXPERTS, MOE_TILES),
        in_specs=[
            pl.BlockSpec((MOE_TM, D), tok),
            pl.BlockSpec((MOE_TM, N_EXPERTS), tok),
            pl.BlockSpec((1, 1, D, EXPERT_FF), lambda hf, e, t: (layer, e, 0, 0)),
            pl.BlockSpec((1, 1, D, EXPERT_FF), lambda hf, e, t: (layer, e, 0, 0)),
            pl.BlockSpec((1, 1, EXPERT_FF, D), lambda hf, e, t: (layer, e, 0, 0)),
        ],
        out_specs=pl.BlockSpec((MOE_TM, D), out_idx),
        out_shape=jax.ShapeDtypeStruct((T_ALL, D), F32),
        scratch_shapes=[pltpu.VMEM((MOE_TILES, MOE_TM, D), F32), pltpu.VMEM((D, EXPERT_FF), BF16),
                        pltpu.VMEM((D, EXPERT_FF), BF16), pltpu.VMEM((EXPERT_FF, D), BF16)],
        compiler_params=_cparams(("arbitrary", "arbitrary", "arbitrary")),
        name="moe_l%d" % layer,
    )(h2, gates, prm["w_e_gate"], prm["w_e_up"], prm["w_e_down"])


def _final_kernel(x1_ref, y_ref, mod_ref, o_ref):
    o_ref[...] = x1_ref[...] + mod_ref[0, :, 5 * D:6 * D] * y_ref[...]


def _final(x1, y, mods3):
    row = lambda i: (i, 0)
    return pl.pallas_call(
        _final_kernel,
        grid=(N_TILES,),
        in_specs=[pl.BlockSpec((TM, D), row), pl.BlockSpec((TM, D), row),
                  pl.BlockSpec((1, 1, MOD_W), lambda i: ((DEPTH - 1) * COND_ROWS + _mod_row(i), 0, 0))],
        out_specs=pl.BlockSpec((TM, D), row),
        out_shape=jax.ShapeDtypeStruct((T_ALL, D), F32),
        compiler_params=_cparams(("arbitrary",)),
        name="final_residual",
    )(x1, y, mods3)


def _rope_tables():
    t = np.arange(SAMPLE_LEN)
    n_freq = QK_ROPE // 4
    inv_freq = ROPE_THETA ** (-np.arange(n_freq, dtype=np.float32) / n_freq)
    row = (t // GRID_W).astype(np.float32)
    col = (t % GRID_W).astype(np.float32)
    ang = jnp.concatenate([jnp.asarray(row)[:, None] * inv_freq, jnp.asarray(col)[:, None] * inv_freq],
                          axis=-1)
    cos_p = jnp.repeat(jnp.cos(ang), 2, axis=-1)
    sin_p = jnp.repeat(jnp.sin(ang), 2, axis=-1) * jnp.tile(jnp.asarray([-1.0, 1.0], F32), QK_ROPE // 2)
    pad_l = jnp.ones((SAMPLE_LEN, QK_NOPE), F32)
    pad_r = jnp.ones((SAMPLE_LEN, LANES - QK_HEAD), F32)
    cos = jnp.concatenate([pad_l, cos_p, pad_r], axis=-1)
    sin = jnp.concatenate([0 * pad_l, sin_p, 0 * pad_r], axis=-1)
    ident_c = jnp.ones((TM, LANES), F32)
    return jnp.concatenate([ident_c, cos], axis=0), jnp.concatenate([0 * ident_c, sin], axis=0)


def _pad_lanes(x, lo, total):
    pad = [(0, 0)] * (x.ndim - 1) + [(lo, total - lo - x.shape[-1])]
    return jnp.pad(x, pad)


def kernel(x_prompt, x_sample, cache_mla_ckv, cache_mla_krope, cache_na_k, cache_na_v, c, c_ctx, w_ada, b_ada, norm1_g, norm2_g, w_in, q_a_g, kv_a_g, w_q_up, w_kv_up, mla_q_norm_g, mla_k_norm_g, conv_dw_w, conv_dw_b, conv_ln_g, conv_ln_b, na_q_norm_g, na_k_norm_g, na_rpb, w_out, w_router, b_router, w_e_gate, w_e_up, w_e_down):
    off_q, off_kv, off_kr = Q_LORA, Q_LORA + KV_LORA, Q_LORA + KV_LORA + QK_ROPE
    w_in_p = jnp.concatenate(
        [w_in[..., :off_kv], _pad_lanes(w_in[..., off_kv:off_kr], QK_NOPE, LANES), w_in[..., off_kr:]], axis=-1)
    w_q = _pad_lanes(w_q_up.reshape(DEPTH, Q_LORA, MLA_HEADS, QK_HEAD), 0, LANES).reshape(
        DEPTH, Q_LORA, MLA_HEADS * LANES)
    w_kv = w_kv_up.reshape(DEPTH, KV_LORA, MLA_HEADS, QK_NOPE + V_HEAD)
    w_k = _pad_lanes(w_kv[..., :QK_NOPE], 0, LANES).reshape(DEPTH, KV_LORA, MLA_HEADS * LANES)
    w_v = w_kv[..., QK_NOPE:].reshape(DEPTH, KV_LORA, MLA_HEADS * V_HEAD)
    cos, sin = _rope_tables()
    vec = lambda a: a.reshape(DEPTH, 1, a.shape[-1])
    prm = {
        "norm1_g": vec(norm1_g), "norm2_g": vec(norm2_g), "w_in": w_in_p, "q_a_g": vec(q_a_g),
        "kv_a_g": vec(kv_a_g), "w_q": w_q, "w_k": w_k, "w_v": w_v,
        "q_norm_g": vec(_pad_lanes(mla_q_norm_g, 0, LANES)), "k_norm_g": vec(_pad_lanes(mla_k_norm_g, 0, LANES)),
        "na_q_g": vec(jnp.tile(na_q_norm_g, (1, 2))), "na_k_g": vec(jnp.tile(na_k_norm_g, (1, 2))),
        "cos": cos, "sin": sin,
        "conv_w": conv_dw_w, "conv_b": vec(conv_dw_b), "conv_ln_g": vec(conv_ln_g), "conv_ln_b": vec(conv_ln_b),
        "w_out": w_out, "w_router": w_router, "b_router": b_router.reshape(1, N_EXPERTS),
        "w_e_gate": w_e_gate, "w_e_up": w_e_up, "w_e_down": w_e_down,
    }

    cond = jnp.concatenate([c_ctx[None], c, jnp.zeros((COND_ROWS - N_COND, D), F32)], axis=0)
    mods3 = _modulation(cond.T, w_ada, b_ada).reshape(DEPTH * COND_ROWS, 1, MOD_W)

    kc, vc = _ctx_kv(cache_mla_ckv, _pad_lanes(cache_mla_krope, QK_NOPE, LANES), prm)
    blocks = _na_bias_blocks(na_rpb)
    bias = blocks[:, :, _na_block_index()]
    bias = bias.transpose(0, 1, 2, 3, 5, 4, 6).reshape(DEPTH, NA_HEADS, TILES_PER_SAMPLE, TM, NA_KEYS)
    cache_k = cache_na_k.reshape(N_SAMPLE, DEPTH, PAST, NA_WIDTH)
    cache_v = cache_na_v.reshape(N_SAMPLE, DEPTH, PAST, NA_WIDTH)

    x = jnp.concatenate([x_prompt.reshape(T_PROMPT, D), x_sample.reshape(T_SAMPLE, D)], axis=0)
    x1 = y = None
    new_ckv, new_kr, new_nak, new_nav = [], [], [], []
    for layer in range(DEPTH):
        if layer == 0:
            outs = _pre(layer, False, x, mods3, prm)
        else:
            outs = _pre(layer, True, (x1, y), mods3, prm)
            x, outs = outs[0], outs[1:]
        q, k, v, ckv_n, kr, u, nq, nk, nv, nkf, nvf = outs
        om_p, on_p = _prompt_attn(q, k, v, nq, nk, nv)
        om_s = _sample_mla(layer, q, k, v, kc, vc)
        on_s = _sample_na(layer, nq, nk, nv, cache_k, cache_v, bias)
        o_mla = jnp.concatenate([om_p, om_s], axis=1)
        o_na = jnp.concatenate([on_p, on_s], axis=0)
        x1, h2, gates = _post(layer, x, o_mla, u, o_na, mods3, prm)
        y = _moe(layer, h2, gates, prm)
        new_ckv.append(ckv_n[:T_PROMPT].reshape(N_PROMPT, PROMPT_LEN, KV_LORA))
        new_kr.append(kr[:T_PROMPT, QK_NOPE:QK_HEAD].reshape(N_PROMPT, PROMPT_LEN, QK_ROPE))
        new_nak.append(nkf[:T_PROMPT].reshape(N_PROMPT, PROMPT_LEN, NA_HEADS, NA_DIM))
        new_nav.append(nvf[:T_PROMPT].reshape(N_PROMPT, PROMPT_LEN, NA_HEADS, NA_DIM))
    xf = _final(x1, y, mods3)
    return (xf[:T_PROMPT].reshape(N_PROMPT, PROMPT_LEN, D), xf[T_PROMPT:].reshape(N_SAMPLE, SAMPLE_LEN, D),
            jnp.stack(new_ckv, axis=1), jnp.stack(new_kr, axis=1), jnp.stack(new_nak, axis=1),
            jnp.stack(new_nav, axis=1))
```

```python
import functools

import numpy as np
import jax
import jax.numpy as jnp
from jax import lax
from jax.experimental import pallas as pl
from jax.experimental.pallas import tpu as pltpu

F32 = jnp.float32
BF16 = jnp.bfloat16

D = 1024
N_PROMPT = 16
PROMPT_LEN = 256
N_SAMPLE = 4
SAMPLE_LEN = 1024
DEPTH = 4
PAST = 512
GRID_W = 64
GRID_ROWS = SAMPLE_LEN // GRID_W
MLA_HEADS = 8
QK_NOPE = 64
QK_ROPE = 32
QK_HEAD = QK_NOPE + QK_ROPE
V_HEAD = 64
Q_LORA = 256
KV_LORA = 128
CONV_CH = 256
CONV_K = 31
NA_HEADS = 4
NA_DIM = 64
NA_WIDTH = NA_HEADS * NA_DIM
NA_WIN_H = 8
NA_WIN_W = 16
N_EXPERTS = 16
N_GROUPS = 4
EPG = 4
EXPERT_FF = 512
ROPE_THETA = 10000.0
EPS = 1e-6
NEG_INF = -1e30
MLA_SCALE = QK_HEAD ** -0.5
NA_SCALE = NA_DIM ** -0.5

LANES = 128
TM = 256
T_PROMPT = N_PROMPT * PROMPT_LEN
T_SAMPLE = N_SAMPLE * SAMPLE_LEN
T_ALL = T_PROMPT + T_SAMPLE
N_TILES = T_ALL // TM
PROMPT_TILES = T_PROMPT // TM
TILES_PER_SAMPLE = SAMPLE_LEN // TM
N_COND = 1 + N_SAMPLE
COND_ROWS = 8
MOD_W = 6 * D

COL_Q = 0
COL_KV = Q_LORA
COL_KR = COL_KV + KV_LORA
COL_CONV = COL_KR + LANES
COL_NA = COL_CONV + 2 * CONV_CH
IN_COLS_P = COL_NA + 3 * NA_WIDTH

NA_KEY_ROWS = 12
NA_KEYS = NA_KEY_ROWS * GRID_W

VMEM_LIMIT = 56 * 1024 * 1024

RUN_ALIGN = 16
TILE_CAP = 768
RUN_SIZES = (256, 128, 64, 32, 16)
MOE_CHUNK = 256
XY_ROWS = 8704


def _cparams(sem):
    return pltpu.CompilerParams(dimension_semantics=sem, vmem_limit_bytes=VMEM_LIMIT)


def _mod_row(i):
    return jnp.where(i < PROMPT_TILES, 0, 1 + (i - PROMPT_TILES) // TILES_PER_SAMPLE)


def _rope_block(i):
    return jnp.where(i < PROMPT_TILES, 0, 1 + (i - PROMPT_TILES) % TILES_PER_SAMPLE)


def _dot(a, b):
    return jnp.dot(a, b, preferred_element_type=F32)


def _dot_t(a, b):
    return lax.dot_general(a, b, (((1,), (1,)), ((), ())), preferred_element_type=F32)


def _silu(x):
    return x * jax.nn.sigmoid(x)


def _lane_is_low(shape):
    return lax.broadcasted_iota(jnp.int32, shape, len(shape) - 1) < (LANES // 2)


MOD_BLK = 768


def _mod_kernel(ct_ref, w_ref, b_ref, o_ref, cb_ref):
    first = (pl.program_id(0) == 0) & (pl.program_id(1) == 0)

    @pl.when(first)
    def _():
        s = _silu(ct_ref[...])
        for m in range(N_COND):
            cb_ref[m] = jnp.broadcast_to(s[:, m:m + 1], (D, LANES))

    for j in range(MOD_BLK // LANES):
        w = w_ref[0, :, j * LANES:(j + 1) * LANES]
        rows = [jnp.sum(w * cb_ref[m], axis=0, keepdims=True) for m in range(N_COND)]
        rows.append(jnp.zeros((COND_ROWS - N_COND, LANES), F32))
        o_ref[0, :, j * LANES:(j + 1) * LANES] = (
            jnp.concatenate(rows, axis=0) + b_ref[0, :, j * LANES:(j + 1) * LANES])


def _modulation(cond_t, w_ada, b_ada):
    nb = MOD_W // MOD_BLK
    return pl.pallas_call(
        _mod_kernel,
        grid=(DEPTH, nb),
        in_specs=[
            pl.BlockSpec((D, COND_ROWS), lambda l, j: (0, 0)),
            pl.BlockSpec((1, D, MOD_BLK), lambda l, j: (l, 0, j)),
            pl.BlockSpec((1, 1, MOD_BLK), lambda l, j: (l, 0, j)),
        ],
        out_specs=pl.BlockSpec((1, COND_ROWS, MOD_BLK), lambda l, j: (l, 0, j)),
        out_shape=jax.ShapeDtypeStruct((DEPTH, COND_ROWS, MOD_W), F32),
        scratch_shapes=[pltpu.VMEM((N_COND, D, LANES), F32)],
        compiler_params=_cparams(("arbitrary", "arbitrary")),
        name="modulation",
    )(cond_t, w_ada, b_ada.reshape(DEPTH, 1, MOD_W))


def _rms(x, width):
    return x * lax.rsqrt(jnp.sum(x * x, axis=-1, keepdims=True) * (1.0 / width) + EPS)


def _rope(x, cos, sin):
    lane = lax.broadcasted_iota(jnp.int32, x.shape, 1)
    partner = jnp.where((lane & 1) == 0, pltpu.roll(x, LANES - 1, 1), pltpu.roll(x, 1, 1))
    return x * cos + partner * sin


def _pair_rms(x, gain):
    low = _lane_is_low(x.shape)
    sq = x * x
    tot = jnp.sum(sq, axis=-1, keepdims=True)
    lo = jnp.sum(jnp.where(low, sq, 0.0), axis=-1, keepdims=True)
    ms = jnp.where(low, lo, tot - lo) * (1.0 / NA_DIM)
    return x * lax.rsqrt(ms + EPS) * gain


def _softmax_pv(scores, values):
    m = scores[0].max(axis=-1, keepdims=True)
    for s in scores[1:]:
        m = jnp.maximum(m, s.max(axis=-1, keepdims=True))
    acc = None
    den = None
    for s, v in zip(scores, values):
        p = jnp.exp(s - m)
        d = jnp.sum(p, axis=-1, keepdims=True)
        o = _dot(p.astype(BF16), v)
        acc = o if acc is None else acc + o
        den = d if den is None else den + d
    return acc * (1.0 / den)


def _moe_combine(x1_ref, ys_ref, pc_ref, modp_ref):
    pc = pc_ref[...]
    lane = lax.broadcasted_iota(jnp.int32, (TM, TILE_CAP), 1)
    ys = ys_ref[...]
    y1 = _dot((lane == pc[:, 0:1].astype(jnp.int32)).astype(BF16), ys)
    y2 = _dot((lane == pc[:, 1:2].astype(jnp.int32)).astype(BF16), ys)
    gate2 = modp_ref[0, :, 5 * D:6 * D]
    return x1_ref[...] + gate2 * (pc[:, 2:3] * y1 + pc[:, 3:4] * y2)


def _pre_kernel(has_moe, *refs):
    if has_moe:
        x1_ref, ys_ref, pc_ref, modp_ref = refs[:4]
        refs = refs[4:]
    else:
        x_ref = refs[0]
        refs = refs[1:]
    (mod_ref, n1g_ref, win_ref, qag_ref, kvag_ref, wq_ref, wk_ref, wv_ref, qng_ref, kng_ref,
     naqg_ref, nakg_ref, cos_ref, sin_ref) = refs[:14]
    refs = refs[14:]
    if has_moe:
        xo_ref = refs[0]
        refs = refs[1:]
    (q_out, k_out, v_out, ckv_out, kr_out, u_out, naq_out, nak_out, nav_out, nakf_out, navf_out,
     win_bf, wq_bf, wk_bf, wv_bf) = refs

    @pl.when(pl.program_id(0) == 0)
    def _():
        win_bf[...] = win_ref[0].astype(BF16)
        wq_bf[...] = wq_ref[0].astype(BF16)
        wk_bf[...] = wk_ref[0].astype(BF16)
        wv_bf[...] = wv_ref[0].astype(BF16)

    if has_moe:
        x = _moe_combine(x1_ref, ys_ref, pc_ref, modp_ref)
        xo_ref[...] = x
    else:
        x = x_ref[...]
    shift = mod_ref[0, :, 0:D]
    scale = mod_ref[0, :, D:2 * D]
    h = _rms(x, D) * n1g_ref[0] * (1.0 + scale) + shift
    proj = _dot(h.astype(BF16), win_bf[...])

    cos = cos_ref[...]
    sin = sin_ref[...]

    q_lat = _rms(proj[:, COL_Q:COL_Q + Q_LORA], Q_LORA) * qag_ref[0]
    q = _dot(q_lat.astype(BF16), wq_bf[...])
    for hd in range(MLA_HEADS):
        qh = q[:, hd * LANES:(hd + 1) * LANES]
        qh = _rope(_rms(qh, QK_HEAD) * qng_ref[0], cos, sin)
        q_out[hd] = (qh * MLA_SCALE).astype(BF16)

    ckv_n = _rms(proj[:, COL_KV:COL_KV + KV_LORA], KV_LORA) * kvag_ref[0]
    ckv_out[...] = ckv_n
    kr = proj[:, COL_KR:COL_KR + LANES]
    kr_out[...] = kr
    ckv_b = ckv_n.astype(BF16)
    k_nope = _dot(ckv_b, wk_bf[...])
    for hd in range(MLA_HEADS):
        kh = k_nope[:, hd * LANES:(hd + 1) * LANES] + kr
        kh = _rope(_rms(kh, QK_HEAD) * kng_ref[0], cos, sin)
        k_out[hd] = kh.astype(BF16)
    v = _dot(ckv_b, wv_bf[...])
    for p in range(MLA_HEADS // 2):
        v_out[p] = v[:, p * LANES:(p + 1) * LANES].astype(BF16)

    a = proj[:, COL_CONV:COL_CONV + CONV_CH]
    g = proj[:, COL_CONV + CONV_CH:COL_CONV + 2 * CONV_CH]
    u_out[...] = a * jax.nn.sigmoid(g)

    for p in range(NA_HEADS // 2):
        sl = slice(p * LANES, (p + 1) * LANES)
        qn = _pair_rms(proj[:, COL_NA + p * LANES:COL_NA + (p + 1) * LANES], naqg_ref[0])
        kn = _pair_rms(proj[:, COL_NA + NA_WIDTH + p * LANES:COL_NA + NA_WIDTH + (p + 1) * LANES],
                       nakg_ref[0])
        vn = proj[:, COL_NA + 2 * NA_WIDTH + p * LANES:COL_NA + 2 * NA_WIDTH + (p + 1) * LANES]
        naq_out[:, sl] = (qn * NA_SCALE).astype(BF16)
        nak_out[:, sl] = kn.astype(BF16)
        nav_out[:, sl] = vn.astype(BF16)
        nakf_out[:, sl] = kn
        navf_out[:, sl] = vn


def _pre(layer, has_moe, x_or_parts, mods3, prm):
    row = lambda i: (i, 0)
    const2 = lambda i: (0, 0)
    lay3 = lambda i: (layer, 0, 0)
    in_specs = []
    args = []
    if has_moe:
        x1, ys, pcol = x_or_parts
        in_specs += [pl.BlockSpec((TM, D), row), pl.BlockSpec((TILE_CAP, D), row),
                     pl.BlockSpec((TM, LANES), row),
                     pl.BlockSpec((1, 1, MOD_W), lambda i: ((layer - 1) * COND_ROWS + _mod_row(i), 0, 0))]
        args += [x1, ys, pcol, mods3]
    else:
        in_specs += [pl.BlockSpec((TM, D), row)]
        args += [x_or_parts]
    in_specs += [
        pl.BlockSpec((1, 1, MOD_W), lambda i: (layer * COND_ROWS + _mod_row(i), 0, 0)),
        pl.BlockSpec((1, 1, D), lay3),
        pl.BlockSpec((1, D, IN_COLS_P), lay3),
        pl.BlockSpec((1, 1, Q_LORA), lay3),
        pl.BlockSpec((1, 1, KV_LORA), lay3),
        pl.BlockSpec((1, Q_LORA, MLA_HEADS * LANES), lay3),
        pl.BlockSpec((1, KV_LORA, MLA_HEADS * LANES), lay3),
        pl.BlockSpec((1, KV_LORA, MLA_HEADS * V_HEAD), lay3),
        pl.BlockSpec((1, 1, LANES), lay3),
        pl.BlockSpec((1, 1, LANES), lay3),
        pl.BlockSpec((1, 1, LANES), lay3),
        pl.BlockSpec((1, 1, LANES), lay3),
        pl.BlockSpec((TM, LANES), lambda i: (_rope_block(i), 0)),
        pl.BlockSpec((TM, LANES), lambda i: (_rope_block(i), 0)),
    ]
    args += [mods3, prm["norm1_g"], prm["w_in"], prm["q_a_g"], prm["kv_a_g"], prm["w_q"], prm["w_k"],
             prm["w_v"], prm["q_norm_g"], prm["k_norm_g"], prm["na_q_g"], prm["na_k_g"],
             prm["cos"], prm["sin"]]
    out_specs = []
    out_shape = []
    if has_moe:
        out_specs.append(pl.BlockSpec((TM, D), row))
        out_shape.append(jax.ShapeDtypeStruct((T_ALL, D), F32))
    head3 = lambda n: pl.BlockSpec((n, TM, LANES), lambda i: (0, i, 0))
    out_specs += [head3(MLA_HEADS), head3(MLA_HEADS), head3(MLA_HEADS // 2),
                  pl.BlockSpec((TM, KV_LORA), row), pl.BlockSpec((TM, LANES), row),
                  pl.BlockSpec((TM, CONV_CH), row)] + [pl.BlockSpec((TM, NA_WIDTH), row)] * 5
    out_shape += [
        jax.ShapeDtypeStruct((MLA_HEADS, T_ALL, LANES), BF16),
        jax.ShapeDtypeStruct((MLA_HEADS, T_ALL, LANES), BF16),
        jax.ShapeDtypeStruct((MLA_HEADS // 2, T_ALL, LANES), BF16),
        jax.ShapeDtypeStruct((T_ALL, KV_LORA), F32),
        jax.ShapeDtypeStruct((T_ALL, LANES), F32),
        jax.ShapeDtypeStruct((T_ALL, CONV_CH), F32),
        jax.ShapeDtypeStruct((T_ALL, NA_WIDTH), BF16),
        jax.ShapeDtypeStruct((T_ALL, NA_WIDTH), BF16),
        jax.ShapeDtypeStruct((T_ALL, NA_WIDTH), BF16),
        jax.ShapeDtypeStruct((T_ALL, NA_WIDTH), F32),
        jax.ShapeDtypeStruct((T_ALL, NA_WIDTH), F32),
    ]
    return pl.pallas_call(
        functools.partial(_pre_kernel, has_moe),
        grid=(N_TILES,),
        in_specs=in_specs,
        out_specs=out_specs,
        out_shape=out_shape,
        scratch_shapes=[pltpu.VMEM((D, IN_COLS_P), BF16), pltpu.VMEM((Q_LORA, MLA_HEADS * LANES), BF16),
                        pltpu.VMEM((KV_LORA, MLA_HEADS * LANES), BF16),
                        pltpu.VMEM((KV_LORA, MLA_HEADS * V_HEAD), BF16)],
        compiler_params=_cparams(("arbitrary",)),
        name="pre_l%d" % layer,
    )(*args)


def _ctx_kernel(ckv_ref, kr_ref, wk_ref, wv_ref, kng_ref, kc_out, vc_out):
    ckv = ckv_ref[0, 0].astype(BF16)
    kr = kr_ref[0, 0]
    k_nope = _dot(ckv, wk_ref[0].astype(BF16))
    for hd in range(MLA_HEADS):
        kh = k_nope[:, hd * LANES:(hd + 1) * LANES] + kr
        kc_out[0, 0, hd] = (_rms(kh, QK_HEAD) * kng_ref[0]).astype(BF16)
    v = _dot(ckv, wv_ref[0].astype(BF16))
    for p in range(MLA_HEADS // 2):
        vc_out[0, 0, p] = v[:, p * LANES:(p + 1) * LANES].astype(BF16)


def _ctx_kv(cache_ckv, kr_slab, prm):
    return pl.pallas_call(
        _ctx_kernel,
        grid=(DEPTH, N_SAMPLE),
        in_specs=[
            pl.BlockSpec((1, 1, PAST, KV_LORA), lambda l, b: (b, l, 0, 0)),
            pl.BlockSpec((1, 1, PAST, LANES), lambda l, b: (b, l, 0, 0)),
            pl.BlockSpec((1, KV_LORA, MLA_HEADS * LANES), lambda l, b: (l, 0, 0)),
            pl.BlockSpec((1, KV_LORA, MLA_HEADS * V_HEAD), lambda l, b: (l, 0, 0)),
            pl.BlockSpec((1, 1, LANES), lambda l, b: (l, 0, 0)),
        ],
        out_specs=[
            pl.BlockSpec((1, 1, MLA_HEADS, PAST, LANES), lambda l, b: (l, b, 0, 0, 0)),
            pl.BlockSpec((1, 1, MLA_HEADS // 2, PAST, LANES), lambda l, b: (l, b, 0, 0, 0)),
        ],
        out_shape=[
            jax.ShapeDtypeStruct((DEPTH, N_SAMPLE, MLA_HEADS, PAST, LANES), BF16),
            jax.ShapeDtypeStruct((DEPTH, N_SAMPLE, MLA_HEADS // 2, PAST, LANES), BF16),
        ],
        compiler_params=_cparams(("arbitrary", "arbitrary")),
        name="ctx_kv",
    )(cache_ckv, kr_slab, prm["w_k"], prm["w_v"], prm["k_norm_g"])


N_DR = 2 * NA_WIN_H - 1
N_DC = 2 * NA_WIN_W - 1


def _bias_kernel(rpb_ref, o_ref):
    cq = lax.broadcasted_iota(jnp.int32, (GRID_W, GRID_W), 0)
    ck = lax.broadcasted_iota(jnp.int32, (GRID_W, GRID_W), 1)
    dc = jnp.clip(ck - cq + (NA_WIN_W - 1), 0, N_DC - 1)
    c0 = jnp.clip(cq - NA_WIN_W // 2, 0, GRID_W - NA_WIN_W)
    in_win = (ck >= c0) & (ck < c0 + NA_WIN_W)
    for dr in range(N_DR):
        acc = jnp.zeros((GRID_W, GRID_W), F32)
        for j in range(N_DC):
            acc = jnp.where(dc == j, rpb_ref[pl.program_id(0), pl.program_id(1), dr, j], acc)
        o_ref[0, 0, dr] = jnp.where(in_win, acc, NEG_INF)
    o_ref[0, 0, N_DR] = jnp.full((GRID_W, GRID_W), NEG_INF, F32)


def _na_bias_blocks(na_rpb):
    return pl.pallas_call(
        _bias_kernel,
        grid=(DEPTH, NA_HEADS),
        in_specs=[pl.BlockSpec(memory_space=pltpu.SMEM)],
        out_specs=pl.BlockSpec((1, 1, N_DR + 1, GRID_W, GRID_W), lambda l, h: (l, h, 0, 0, 0)),
        out_shape=jax.ShapeDtypeStruct((DEPTH, NA_HEADS, N_DR + 1, GRID_W, GRID_W), F32),
        compiler_params=_cparams(("arbitrary", "arbitrary")),
        name="na_bias",
    )(na_rpb)


def _na_block_index():
    idx = np.full((TILES_PER_SAMPLE, TM // GRID_W, NA_KEY_ROWS), N_DR, np.int32)
    for i in range(TILES_PER_SAMPLE):
        ks = _na_key_start_row(i)
        for rq in range(TM // GRID_W):
            r = i * (TM // GRID_W) + rq
            r0 = min(max(r - NA_WIN_H // 2, 0), GRID_ROWS - NA_WIN_H)
            for rk in range(NA_KEY_ROWS):
                kr = ks + rk
                if r0 <= kr < r0 + NA_WIN_H:
                    idx[i, rq, rk] = kr - r + (NA_WIN_H - 1)
    return idx


def _na_key_start_row(i):
    return 0 if i < TILES_PER_SAMPLE // 2 else GRID_ROWS - NA_KEY_ROWS


def _prompt_attn_kernel(q_ref, k_ref, v_ref, nq_ref, nk_ref, nv_ref, om_ref, on_ref):
    low = _lane_is_low((PROMPT_LEN, LANES))
    for p in range(MLA_HEADS // 2):
        outs = []
        for hh in range(2):
            hd = 2 * p + hh
            s = _dot_t(q_ref[hd], k_ref[hd])
            outs.append(_softmax_pv([s], [v_ref[p]]))
        om_ref[p] = jnp.where(low, outs[0], outs[1]).astype(BF16)
    for p in range(NA_HEADS // 2):
        sl = slice(p * LANES, (p + 1) * LANES)
        qp = nq_ref[:, sl]
        kp = nk_ref[:, sl]
        vp = nv_ref[:, sl]
        outs = []
        for hh in range(2):
            qm = jnp.where(low if hh == 0 else ~low, qp, jnp.zeros_like(qp))
            outs.append(_softmax_pv([_dot_t(qm, kp)], [vp]))
        on_ref[:, sl] = jnp.where(low, outs[0], outs[1]).astype(BF16)


def _prompt_attn(q, k, v, nq, nk, nv):
    head3 = lambda n: pl.BlockSpec((n, PROMPT_LEN, LANES), lambda b: (0, b, 0))
    row = pl.BlockSpec((PROMPT_LEN, NA_WIDTH), lambda b: (b, 0))
    return pl.pallas_call(
        _prompt_attn_kernel,
        grid=(N_PROMPT,),
        in_specs=[head3(MLA_HEADS), head3(MLA_HEADS), head3(MLA_HEADS // 2), row, row, row],
        out_specs=[head3(MLA_HEADS // 2), row],
        out_shape=[jax.ShapeDtypeStruct((MLA_HEADS // 2, T_PROMPT, LANES), BF16),
                   jax.ShapeDtypeStruct((T_PROMPT, NA_WIDTH), BF16)],
        compiler_params=_cparams(("arbitrary",)),
        name="prompt_attn",
    )(q, k, v, nq, nk, nv)


def _sample_mla_kernel(q_ref, k_ref, v_ref, kc_ref, vc_ref, o_ref):
    low = _lane_is_low((TM, LANES))
    outs = []
    for hh in range(2):
        q = q_ref[hh]
        s_lat = _dot_t(q, k_ref[hh])
        s_ctx = _dot_t(q, kc_ref[0, 0, hh])
        outs.append(_softmax_pv([s_lat, s_ctx], [v_ref[0], vc_ref[0, 0, 0]]))
    o_ref[0] = jnp.where(low, outs[0], outs[1]).astype(BF16)


def _sample_mla(layer, q, k, v, kc, vc):
    seq_blk0 = T_PROMPT // SAMPLE_LEN
    return pl.pallas_call(
        _sample_mla_kernel,
        grid=(N_SAMPLE, MLA_HEADS // 2, TILES_PER_SAMPLE),
        in_specs=[
            pl.BlockSpec((2, TM, LANES), lambda b, p, i: (p, PROMPT_TILES + b * TILES_PER_SAMPLE + i, 0)),
            pl.BlockSpec((2, SAMPLE_LEN, LANES), lambda b, p, i: (p, seq_blk0 + b, 0)),
            pl.BlockSpec((1, SAMPLE_LEN, LANES), lambda b, p, i: (p, seq_blk0 + b, 0)),
            pl.BlockSpec((1, 1, 2, PAST, LANES), lambda b, p, i: (layer, b, p, 0, 0)),
            pl.BlockSpec((1, 1, 1, PAST, LANES), lambda b, p, i: (layer, b, p, 0, 0)),
        ],
        out_specs=pl.BlockSpec((1, TM, LANES), lambda b, p, i: (p, b * TILES_PER_SAMPLE + i, 0)),
        out_shape=jax.ShapeDtypeStruct((MLA_HEADS // 2, T_SAMPLE, LANES), BF16),
        compiler_params=_cparams(("arbitrary", "arbitrary", "arbitrary")),
        name="sample_mla",
    )(q, k, v, kc, vc)


def _sample_na_kernel(q_ref, k_ref, v_ref, kc_ref, vc_ref, b_ref, o_ref):
    i = pl.program_id(2)
    start = pl.multiple_of(
        jnp.where(i < TILES_PER_SAMPLE // 2, 0, (GRID_ROWS - NA_KEY_ROWS) * GRID_W), TM)
    k_win = k_ref[pl.ds(start, NA_KEYS), :]
    v_win = v_ref[pl.ds(start, NA_KEYS), :]
    kc = kc_ref[0, 0].astype(BF16)
    vc = vc_ref[0, 0].astype(BF16)
    q = q_ref[...]
    low = _lane_is_low((TM, LANES))
    outs = []
    for hh in range(2):
        qm = jnp.where(low if hh == 0 else ~low, q, jnp.zeros_like(q))
        s_loc = _dot_t(qm, k_win) + b_ref[0, hh, 0]
        s_ctx = _dot_t(qm, kc)
        outs.append(_softmax_pv([s_loc, s_ctx], [v_win, vc]))
    o_ref[...] = jnp.where(low, outs[0], outs[1]).astype(BF16)


def _sample_na(layer, nq, nk, nv, cache_k, cache_v, bias):
    seq_blk0 = T_PROMPT // SAMPLE_LEN
    return pl.pallas_call(
        _sample_na_kernel,
        grid=(N_SAMPLE, NA_HEADS // 2, TILES_PER_SAMPLE),
        in_specs=[
            pl.BlockSpec((TM, LANES), lambda b, p, i: (PROMPT_TILES + b * TILES_PER_SAMPLE + i, p)),
            pl.BlockSpec((SAMPLE_LEN, LANES), lambda b, p, i: (seq_blk0 + b, p)),
            pl.BlockSpec((SAMPLE_LEN, LANES), lambda b, p, i: (seq_blk0 + b, p)),
            pl.BlockSpec((1, 1, PAST, LANES), lambda b, p, i: (b, layer, 0, p)),
            pl.BlockSpec((1, 1, PAST, LANES), lambda b, p, i: (b, layer, 0, p)),
            pl.BlockSpec((1, 2, 1, TM, NA_KEYS), lambda b, p, i: (layer, p, i, 0, 0)),
        ],
        out_specs=pl.BlockSpec((TM, LANES), lambda b, p, i: (b * TILES_PER_SAMPLE + i, p)),
        out_shape=jax.ShapeDtypeStruct((T_SAMPLE, NA_WIDTH), BF16),
        compiler_params=_cparams(("arbitrary", "arbitrary", "arbitrary")),
        name="sample_na",
    )(nq, nk, nv, cache_k, cache_v, bias)


CONV_PAD = 16


def _argmax_first(cols):
    best = cols[0]
    idx = jnp.zeros(best.shape, jnp.int32)
    for j in range(1, len(cols)):
        upd = cols[j] > best
        idx = jnp.where(upd, j, idx)
        best = jnp.where(upd, cols[j], best)
    return idx


def _route(aff, bias):
    sel = aff + bias
    rows = [sel[e:e + 1, :] for e in range(N_EXPERTS)]
    group_scores = []
    for g in range(N_GROUPS):
        c = rows[g * EPG:(g + 1) * EPG]
        best = None
        for a in range(EPG):
            for b in range(a + 1, EPG):
                s = c[a] + c[b]
                best = s if best is None else jnp.maximum(best, s)
        group_scores.append(best)
    g_idx = _argmax_first(group_scores)
    in_group = []
    for j in range(EPG):
        v = rows[j]
        for g in range(1, N_GROUPS):
            v = jnp.where(g_idx == g, rows[g * EPG + j], v)
        in_group.append(v)
    i1 = _argmax_first(in_group)
    masked = [jnp.where(i1 == j, -jnp.inf, in_group[j]) for j in range(EPG)]
    i2 = _argmax_first(masked)
    e1 = g_idx * EPG + i1
    e2 = g_idx * EPG + i2
    sub = lax.broadcasted_iota(jnp.int32, aff.shape, 0)
    w1 = jnp.sum(jnp.where(sub == e1, aff, 0.0), axis=0, keepdims=True)
    w2 = jnp.sum(jnp.where(sub == e2, aff, 0.0), axis=0, keepdims=True)
    tot = w1 + w2
    return e1, e2, w1 / tot, w2 / tot


def _sort_tile(e1, e2):
    sub = lax.broadcasted_iota(jnp.int32, (N_EXPERTS, TM), 0)
    oh1 = sub == e1
    oh2 = sub == e2
    one = jnp.ones((N_EXPERTS, TM), F32)
    cnt1 = jnp.sum(jnp.where(oh1, one, 0.0), axis=1, keepdims=True) * one
    cnt2 = jnp.sum(jnp.where(oh2, one, 0.0), axis=1, keepdims=True) * one
    tot = (cnt1 + cnt2).astype(jnp.int32)
    padded = (tot + (RUN_ALIGN - 1)) & (-RUN_ALIGN)
    ee = lax.broadcasted_iota(jnp.int32, (N_EXPERTS, N_EXPERTS), 0)
    ef = lax.broadcasted_iota(jnp.int32, (N_EXPERTS, N_EXPERTS), 1)
    start = _dot((ef < ee).astype(BF16), padded.astype(BF16))
    ta = lax.broadcasted_iota(jnp.int32, (TM, TM), 0)
    tb = lax.broadcasted_iota(jnp.int32, (TM, TM), 1)
    before = (ta < tb).astype(BF16)
    rank1 = _dot(oh1.astype(BF16), before)
    rank2 = _dot(oh2.astype(BF16), before)
    pos1 = jnp.sum(jnp.where(oh1, start + rank1, 0.0), axis=0, keepdims=True)
    pos2 = jnp.sum(jnp.where(oh2, start + cnt1 + rank2, 0.0), axis=0, keepdims=True)
    return pos1.astype(jnp.int32), pos2.astype(jnp.int32), start.astype(jnp.int32), padded


def _post_kernel(x_ref, om_ref, up_ref, uc_ref, un_ref, on_ref, mod_ref, cw_ref, cb_ref, lg_ref, lb_ref,
                 wo_ref, n2g_ref, wr_ref, br_ref, x1_out, xs_out, pc_out, st_out, pd_out, wo_bf, ext_ref):
    i = pl.program_id(0)

    @pl.when(i == 0)
    def _():
        wo_bf[...] = wo_ref[0].astype(BF16)

    pos_in_seq = (i - PROMPT_TILES) % TILES_PER_SAMPLE
    is_first = (i < PROMPT_TILES) | (pos_in_seq == 0)
    is_last = (i < PROMPT_TILES) | (pos_in_seq == TILES_PER_SAMPLE - 1)
    ext_ref[0:CONV_PAD, :] = jnp.where(is_first, 0.0, up_ref[TM - CONV_PAD:TM, :])
    ext_ref[CONV_PAD:CONV_PAD + TM, :] = uc_ref[...]
    ext_ref[CONV_PAD + TM:2 * CONV_PAD + TM, :] = jnp.where(is_last, 0.0, un_ref[0:CONV_PAD, :])
    acc = jnp.zeros((TM, CONV_CH), F32)
    for j in range(CONV_K):
        off = CONV_PAD - CONV_K // 2 + j
        acc = acc + ext_ref[off:off + TM, :] * cw_ref[0, j:j + 1, :]
    conv = acc + cb_ref[0]
    cc = conv - jnp.mean(conv, axis=-1, keepdims=True)
    var = jnp.mean(cc * cc, axis=-1, keepdims=True)
    o_conv = _silu(cc * lax.rsqrt(var + EPS) * lg_ref[0] + lb_ref[0])

    mix = jnp.concatenate([om_ref[p] for p in range(MLA_HEADS // 2)]
                          + [o_conv.astype(BF16), on_ref[...]], axis=-1)
    att = _dot(mix, wo_bf[...])
    gate1 = mod_ref[0, :, 2 * D:3 * D]
    x1 = x_ref[...] + gate1 * att
    x1_out[...] = x1
    shift2 = mod_ref[0, :, 3 * D:4 * D]
    scale2 = mod_ref[0, :, 4 * D:5 * D]
    h2 = _rms(x1, D) * n2g_ref[0] * (1.0 + scale2) + shift2

    logits = lax.dot_general(wr_ref[...], h2, (((1,), (1,)), ((), ())), preferred_element_type=F32,
                             precision=lax.Precision.HIGHEST)
    e1, e2, w1, w2 = _route(jax.nn.sigmoid(logits), br_ref[...])
    pos1, pos2, start, padded = _sort_tile(e1, e2)
    st_out[0] = start[:, 0:LANES]
    pd_out[0] = padded[:, 0:LANES]
    r = lax.broadcasted_iota(jnp.int32, (TILE_CAP, TM), 0)
    perm = ((r == pos1) | (r == pos2)).astype(BF16)
    xs_out[...] = _dot(perm, h2.astype(BF16)).astype(BF16)
    sub = lax.broadcasted_iota(jnp.int32, (LANES, TM), 0)
    packed = jnp.where(sub == 0, pos1.astype(F32),
                       jnp.where(sub == 1, pos2.astype(F32),
                                 jnp.where(sub == 2, w1, jnp.where(sub == 3, w2, 0.0))))
    pc_out[...] = packed.T


def _post(layer, x, o_mla, u, o_na, mods3, prm):
    row = lambda i: (i, 0)
    lay3 = lambda i: (layer, 0, 0)
    return pl.pallas_call(
        _post_kernel,
        grid=(N_TILES,),
        in_specs=[
            pl.BlockSpec((TM, D), row),
            pl.BlockSpec((MLA_HEADS // 2, TM, LANES), lambda i: (0, i, 0)),
            pl.BlockSpec((TM, CONV_CH), lambda i: (jnp.maximum(i - 1, 0), 0)),
            pl.BlockSpec((TM, CONV_CH), row),
            pl.BlockSpec((TM, CONV_CH), lambda i: (jnp.minimum(i + 1, N_TILES - 1), 0)),
            pl.BlockSpec((TM, NA_WIDTH), row),
            pl.BlockSpec((1, 1, MOD_W), lambda i: (layer * COND_ROWS + _mod_row(i), 0, 0)),
            pl.BlockSpec((1, CONV_K, CONV_CH), lay3),
            pl.BlockSpec((1, 1, CONV_CH), lay3),
            pl.BlockSpec((1, 1, CONV_CH), lay3),
            pl.BlockSpec((1, 1, CONV_CH), lay3),
            pl.BlockSpec((1, D, D), lay3),
            pl.BlockSpec((1, 1, D), lay3),
            pl.BlockSpec((N_EXPERTS, D), lambda i: (0, 0)),
            pl.BlockSpec((N_EXPERTS, TM), lambda i: (0, 0)),
        ],
        out_specs=[pl.BlockSpec((TM, D), row), pl.BlockSpec((TILE_CAP, D), row),
                   pl.BlockSpec((TM, LANES), row),
                   pl.BlockSpec((1, N_EXPERTS, LANES), lambda i: (i, 0, 0)),
                   pl.BlockSpec((1, N_EXPERTS, LANES), lambda i: (i, 0, 0))],
        out_shape=[jax.ShapeDtypeStruct((T_ALL, D), F32),
                   jax.ShapeDtypeStruct((N_TILES * TILE_CAP, D), BF16),
                   jax.ShapeDtypeStruct((T_ALL, LANES), F32),
                   jax.ShapeDtypeStruct((N_TILES, N_EXPERTS, LANES), jnp.int32),
                   jax.ShapeDtypeStruct((N_TILES, N_EXPERTS, LANES), jnp.int32)],
        scratch_shapes=[pltpu.VMEM((D, D), BF16), pltpu.VMEM((TM + 2 * CONV_PAD, CONV_CH), F32)],
        compiler_params=_cparams(("arbitrary",)),
        name="post_l%d" % layer,
    )(x, o_mla, u, u, u, o_na, mods3, prm["conv_w"], prm["conv_b"], prm["conv_ln_g"], prm["conv_ln_b"],
      prm["w_out"], prm["norm2_g"], prm["w_router_t"], prm["b_router_t"])


def _for_each_piece(n, fn):
    for sz in RUN_SIZES:
        @pl.when((n & sz) != 0)
        def _():
            fn(n & (-2 * sz), sz)


def _expert_kernel(st_ref, pd_ref, xs_hbm, wg_ref, wu_ref, wd_ref, ys_hbm, xy, wg_bf, wu_bf, wd_bf, zeros_buf,
                   sem_in, sem_out):
    e = pl.program_id(0)

    @pl.when(e == 0)
    def _():
        xy[...] = jnp.zeros(xy.shape, BF16)
        zeros_buf[...] = jnp.zeros(zeros_buf.shape, BF16)

    wg_bf[...] = wg_ref[0, 0].astype(BF16)
    wu_bf[...] = wu_ref[0, 0].astype(BF16)
    wd_bf[...] = wd_ref[0, 0].astype(BF16)

    def run_copy(j, cur, inbound):
        start = st_ref[j * N_EXPERTS + e]
        n = pd_ref[j * N_EXPERTS + e]

        def piece(off, sz):
            rows_hbm = pl.ds(pl.multiple_of(j * TILE_CAP + start + off, RUN_ALIGN), sz)
            rows_buf = pl.ds(pl.multiple_of(cur + off, RUN_ALIGN), sz)
            if inbound:
                return pltpu.make_async_copy(xs_hbm.at[rows_hbm], xy.at[rows_buf], sem_in)
            return pltpu.make_async_copy(xy.at[rows_buf], ys_hbm.at[rows_hbm], sem_out)

        return n, piece

    def sweep(inbound, start_not_wait):
        def body(j, cur):
            n, piece = run_copy(j, cur, inbound)
            if start_not_wait:
                _for_each_piece(n, lambda off, sz: piece(off, sz).start())
            else:
                _for_each_piece(n, lambda off, sz: piece(off, sz).wait())
            return cur + n
        return lax.fori_loop(0, N_TILES, body, 0)

    total = sweep(True, True)
    sweep(True, False)

    def chunk(c, carry):
        rows = pl.ds(pl.multiple_of(c * MOE_CHUNK, MOE_CHUNK), MOE_CHUNK)
        x = xy[rows, :]
        hid = _silu(_dot(x, wg_bf[...])) * _dot(x, wu_bf[...])
        xy[rows, :] = _dot(hid.astype(BF16), wd_bf[...]).astype(BF16)
        return carry

    lax.fori_loop(0, (total + (MOE_CHUNK - 1)) // MOE_CHUNK, chunk, 0)

    sweep(False, True)
    sweep(False, False)

    @pl.when(e == N_EXPERTS - 1)
    def _():
        def tail_copy(j, off, sz):
            used = st_ref[j * N_EXPERTS + e] + pd_ref[j * N_EXPERTS + e]
            rows_hbm = pl.ds(pl.multiple_of(j * TILE_CAP + used + off, RUN_ALIGN), sz)
            return pltpu.make_async_copy(zeros_buf.at[pl.ds(0, sz)], ys_hbm.at[rows_hbm], sem_out)

        def tail_len(j):
            return TILE_CAP - (st_ref[j * N_EXPERTS + e] + pd_ref[j * N_EXPERTS + e])

        def t_start(j, c):
            _for_each_piece(tail_len(j), lambda off, sz: tail_copy(j, off, sz).start())
            return c

        def t_wait(j, c):
            _for_each_piece(tail_len(j), lambda off, sz: tail_copy(j, off, sz).wait())
            return c

        lax.fori_loop(0, N_TILES, t_start, 0)
        lax.fori_loop(0, N_TILES, t_wait, 0)


def _experts(layer, starts, pads, xs, prm):
    wspec = lambda shape: pl.BlockSpec((1, 1) + shape, lambda e, st, pd: (layer, e, 0, 0))
    return pl.pallas_call(
        _expert_kernel,
        grid_spec=pltpu.PrefetchScalarGridSpec(
            num_scalar_prefetch=2,
            grid=(N_EXPERTS,),
            in_specs=[pl.BlockSpec(memory_space=pl.ANY), wspec((D, EXPERT_FF)), wspec((D, EXPERT_FF)),
                      wspec((EXPERT_FF, D))],
            out_specs=pl.BlockSpec(memory_space=pl.ANY),
            scratch_shapes=[pltpu.VMEM((XY_ROWS, D), BF16), pltpu.VMEM((D, EXPERT_FF), BF16),
                            pltpu.VMEM((D, EXPERT_FF), BF16), pltpu.VMEM((EXPERT_FF, D), BF16),
                            pltpu.VMEM((RUN_SIZES[0], D), BF16),
                            pltpu.SemaphoreType.DMA, pltpu.SemaphoreType.DMA]),
        out_shape=jax.ShapeDtypeStruct((N_TILES * TILE_CAP, D), BF16),
        compiler_params=_cparams(("arbitrary",)),
        name="experts_l%d" % layer,
    )(starts, pads, xs, prm["w_e_gate"], prm["w_e_up"], prm["w_e_down"])


def _final_kernel(x1_ref, ys_ref, pc_ref, mod_ref, o_ref):
    o_ref[...] = _moe_combine(x1_ref, ys_ref, pc_ref, mod_ref)


def _final(x1, ys, pcol, mods3):
    row = lambda i: (i, 0)
    return pl.pallas_call(
        _final_kernel,
        grid=(N_TILES,),
        in_specs=[pl.BlockSpec((TM, D), row), pl.BlockSpec((TILE_CAP, D), row), pl.BlockSpec((TM, LANES), row),
                  pl.BlockSpec((1, 1, MOD_W), lambda i: ((DEPTH - 1) * COND_ROWS + _mod_row(i), 0, 0))],
        out_specs=pl.BlockSpec((TM, D), row),
        out_shape=jax.ShapeDtypeStruct((T_ALL, D), F32),
        compiler_params=_cparams(("arbitrary",)),
        name="final_residual",
    )(x1, ys, pcol, mods3)


def _rope_tables():
    t = np.arange(SAMPLE_LEN)
    n_freq = QK_ROPE // 4
    inv_freq = ROPE_THETA ** (-np.arange(n_freq, dtype=np.float32) / n_freq)
    row = (t // GRID_W).astype(np.float32)
    col = (t % GRID_W).astype(np.float32)
    ang = jnp.concatenate([jnp.asarray(row)[:, None] * inv_freq, jnp.asarray(col)[:, None] * inv_freq],
                          axis=-1)
    cos_p = jnp.repeat(jnp.cos(ang), 2, axis=-1)
    sin_p = jnp.repeat(jnp.sin(ang), 2, axis=-1) * jnp.tile(jnp.asarray([-1.0, 1.0], F32), QK_ROPE // 2)
    pad_l = jnp.ones((SAMPLE_LEN, QK_NOPE), F32)
    pad_r = jnp.ones((SAMPLE_LEN, LANES - QK_HEAD), F32)
    cos = jnp.concatenate([pad_l, cos_p, pad_r], axis=-1)
    sin = jnp.concatenate([0 * pad_l, sin_p, 0 * pad_r], axis=-1)
    ident_c = jnp.ones((TM, LANES), F32)
    return jnp.concatenate([ident_c, cos], axis=0), jnp.concatenate([0 * ident_c, sin], axis=0)


def _pad_lanes(x, lo, total):
    pad = [(0, 0)] * (x.ndim - 1) + [(lo, total - lo - x.shape[-1])]
    return jnp.pad(x, pad)


def kernel(x_prompt, x_sample, cache_mla_ckv, cache_mla_krope, cache_na_k, cache_na_v, c, c_ctx, w_ada, b_ada, norm1_g, norm2_g, w_in, q_a_g, kv_a_g, w_q_up, w_kv_up, mla_q_norm_g, mla_k_norm_g, conv_dw_w, conv_dw_b, conv_ln_g, conv_ln_b, na_q_norm_g, na_k_norm_g, na_rpb, w_out, w_router, b_router, w_e_gate, w_e_up, w_e_down):
    off_q, off_kv, off_kr = Q_LORA, Q_LORA + KV_LORA, Q_LORA + KV_LORA + QK_ROPE
    w_in_p = jnp.concatenate(
        [w_in[..., :off_kv], _pad_lanes(w_in[..., off_kv:off_kr], QK_NOPE, LANES), w_in[..., off_kr:]], axis=-1)
    w_q = _pad_lanes(w_q_up.reshape(DEPTH, Q_LORA, MLA_HEADS, QK_HEAD), 0, LANES).reshape(
        DEPTH, Q_LORA, MLA_HEADS * LANES)
    w_kv = w_kv_up.reshape(DEPTH, KV_LORA, MLA_HEADS, QK_NOPE + V_HEAD)
    w_k = _pad_lanes(w_kv[..., :QK_NOPE], 0, LANES).reshape(DEPTH, KV_LORA, MLA_HEADS * LANES)
    w_v = w_kv[..., QK_NOPE:].reshape(DEPTH, KV_LORA, MLA_HEADS * V_HEAD)
    cos, sin = _rope_tables()
    vec = lambda a: a.reshape(DEPTH, 1, a.shape[-1])
    prm = {
        "norm1_g": vec(norm1_g), "norm2_g": vec(norm2_g), "w_in": w_in_p, "q_a_g": vec(q_a_g),
        "kv_a_g": vec(kv_a_g), "w_q": w_q, "w_k": w_k, "w_v": w_v,
        "q_norm_g": vec(_pad_lanes(mla_q_norm_g, 0, LANES)), "k_norm_g": vec(_pad_lanes(mla_k_norm_g, 0, LANES)),
        "na_q_g": vec(jnp.tile(na_q_norm_g, (1, 2))), "na_k_g": vec(jnp.tile(na_k_norm_g, (1, 2))),
        "cos": cos, "sin": sin,
        "conv_w": conv_dw_w, "conv_b": vec(conv_dw_b), "conv_ln_g": vec(conv_ln_g), "conv_ln_b": vec(conv_ln_b),
        "w_out": w_out, "w_router_t": w_router.T,
        "b_router_t": jnp.broadcast_to(b_router[:, None], (N_EXPERTS, TM)),
        "w_e_gate": w_e_gate, "w_e_up": w_e_up, "w_e_down": w_e_down,
    }

    cond = jnp.concatenate([c_ctx[None], c, jnp.zeros((COND_ROWS - N_COND, D), F32)], axis=0)
    mods3 = _modulation(cond.T, w_ada, b_ada).reshape(DEPTH * COND_ROWS, 1, MOD_W)

    kc, vc = _ctx_kv(cache_mla_ckv, _pad_lanes(cache_mla_krope, QK_NOPE, LANES), prm)
    blocks = _na_bias_blocks(na_rpb)
    bias = blocks[:, :, _na_block_index()]
    bias = bias.transpose(0, 1, 2, 3, 5, 4, 6).reshape(DEPTH, NA_HEADS, TILES_PER_SAMPLE, TM, NA_KEYS)
    cache_k = cache_na_k.reshape(N_SAMPLE, DEPTH, PAST, NA_WIDTH)
    cache_v = cache_na_v.reshape(N_SAMPLE, DEPTH, PAST, NA_WIDTH)

    x = jnp.concatenate([x_prompt.reshape(T_PROMPT, D), x_sample.reshape(T_SAMPLE, D)], axis=0)
    x1 = ys = pcol = None
    new_ckv, new_kr, new_nak, new_nav = [], [], [], []
    for layer in range(DEPTH):
        if layer == 0:
            outs = _pre(layer, False, x, mods3, prm)
        else:
            outs = _pre(layer, True, (x1, ys, pcol), mods3, prm)
            x, outs = outs[0], outs[1:]
        q, k, v, ckv_n, kr, u, nq, nk, nv, nkf, nvf = outs
        om_p, on_p = _prompt_attn(q, k, v, nq, nk, nv)
        om_s = _sample_mla(layer, q, k, v, kc, vc)
        on_s = _sample_na(layer, nq, nk, nv, cache_k, cache_v, bias)
        o_mla = jnp.concatenate([om_p, om_s], axis=1)
        o_na = jnp.concatenate([on_p, on_s], axis=0)
        x1, xs, pcol, run_start, run_len = _post(layer, x, o_mla, u, o_na, mods3, prm)
        ys = _experts(layer, run_start[:, :, 0].reshape(-1), run_len[:, :, 0].reshape(-1), xs, prm)
        new_ckv.append(ckv_n[:T_PROMPT].reshape(N_PROMPT, PROMPT_LEN, KV_LORA))
        new_kr.append(kr[:T_PROMPT, QK_NOPE:QK_HEAD].reshape(N_PROMPT, PROMPT_LEN, QK_ROPE))
        new_nak.append(nkf[:T_PROMPT].reshape(N_PROMPT, PROMPT_LEN, NA_HEADS, NA_DIM))
        new_nav.append(nvf[:T_PROMPT].reshape(N_PROMPT, PROMPT_LEN, NA_HEADS, NA_DIM))
    xf = _final(x1, ys, pcol, mods3)
    return (xf[:T_PROMPT].reshape(N_PROMPT, PROMPT_LEN, D), xf[T_PROMPT:].reshape(N_SAMPLE, SAMPLE_LEN, D),
            jnp.stack(new_ckv, axis=1), jnp.stack(new_kr, axis=1), jnp.stack(new_nak, axis=1),
            jnp.stack(new_nav, axis=1))
```

```python
import functools

import numpy as np
import jax
import jax.numpy as jnp
from jax import lax
from jax.experimental import pallas as pl
from jax.experimental.pallas import tpu as pltpu

F32 = jnp.float32
BF16 = jnp.bfloat16

D = 1024
N_PROMPT = 16
PROMPT_LEN = 256
N_SAMPLE = 4
SAMPLE_LEN = 1024
DEPTH = 4
PAST = 512
GRID_W = 64
GRID_ROWS = SAMPLE_LEN // GRID_W
MLA_HEADS = 8
QK_NOPE = 64
QK_ROPE = 32
QK_HEAD = QK_NOPE + QK_ROPE
V_HEAD = 64
Q_LORA = 256
KV_LORA = 128
CONV_CH = 256
CONV_K = 31
NA_HEADS = 4
NA_DIM = 64
NA_WIDTH = NA_HEADS * NA_DIM
NA_WIN_H = 8
NA_WIN_W = 16
N_EXPERTS = 16
N_GROUPS = 4
EPG = 4
EXPERT_FF = 512
ROPE_THETA = 10000.0
EPS = 1e-6
NEG_INF = -1e30
MLA_SCALE = QK_HEAD ** -0.5
NA_SCALE = NA_DIM ** -0.5

LANES = 128
TM = 256
T_PROMPT = N_PROMPT * PROMPT_LEN
T_SAMPLE = N_SAMPLE * SAMPLE_LEN
T_ALL = T_PROMPT + T_SAMPLE
N_TILES = T_ALL // TM
PROMPT_TILES = T_PROMPT // TM
TILES_PER_SAMPLE = SAMPLE_LEN // TM
N_COND = 1 + N_SAMPLE
COND_ROWS = 8
MOD_W = 6 * D

COL_Q = 0
COL_KV = Q_LORA
COL_KR = COL_KV + KV_LORA
COL_CONV = COL_KR + LANES
COL_NA = COL_CONV + 2 * CONV_CH
IN_COLS_P = COL_NA + 3 * NA_WIDTH
IN_COLS = IN_COLS_P - (LANES - QK_ROPE)

NA_KEY_ROWS = 12
NA_KEYS = NA_KEY_ROWS * GRID_W

VMEM_LIMIT = 56 * 1024 * 1024

RUN_ALIGN = 16
TILE_CAP = 768
RUN_SIZES = (256, 128, 64, 32, 16)
MOE_CHUNK = 256
HALF_TILES = N_TILES // 2
HALF_ROWS = 4352
WAIT_SIZES = (4096, 2048, 1024, 512, 256, 128, 64, 32, 16)


def _cparams(sem):
    return pltpu.CompilerParams(dimension_semantics=sem, vmem_limit_bytes=VMEM_LIMIT)


def _mod_row(i):
    return jnp.where(i < PROMPT_TILES, 0, 1 + (i - PROMPT_TILES) // TILES_PER_SAMPLE)


def _rope_block(i):
    return jnp.where(i < PROMPT_TILES, 0, 1 + (i - PROMPT_TILES) % TILES_PER_SAMPLE)


def _dot(a, b):
    return jnp.dot(a, b, preferred_element_type=F32)


def _dot_t(a, b):
    return lax.dot_general(a, b, (((1,), (1,)), ((), ())), preferred_element_type=F32)


def _silu(x):
    return x * jax.nn.sigmoid(x)


def _lane_is_low(shape):
    return lax.broadcasted_iota(jnp.int32, shape, len(shape) - 1) < (LANES // 2)


MOD_BLK = 768


def _mod_kernel(ct_ref, w_ref, b_ref, o_ref, cb_ref):
    first = (pl.program_id(0) == 0) & (pl.program_id(1) == 0)

    @pl.when(first)
    def _():
        s = _silu(ct_ref[...])
        for m in range(N_COND):
            cb_ref[m] = jnp.broadcast_to(s[:, m:m + 1], (D, LANES))

    for j in range(MOD_BLK // LANES):
        w = w_ref[0, :, j * LANES:(j + 1) * LANES]
        rows = [jnp.sum(w * cb_ref[m], axis=0, keepdims=True) for m in range(N_COND)]
        rows.append(jnp.zeros((COND_ROWS - N_COND, LANES), F32))
        o_ref[0, :, j * LANES:(j + 1) * LANES] = (
            jnp.concatenate(rows, axis=0) + b_ref[0, :, j * LANES:(j + 1) * LANES])


def _modulation(cond_t, w_ada, b_ada):
    nb = MOD_W // MOD_BLK
    return pl.pallas_call(
        _mod_kernel,
        grid=(DEPTH, nb),
        in_specs=[
            pl.BlockSpec((D, COND_ROWS), lambda l, j: (0, 0)),
            pl.BlockSpec((1, D, MOD_BLK), lambda l, j: (l, 0, j)),
            pl.BlockSpec((1, 1, MOD_BLK), lambda l, j: (l, 0, j)),
        ],
        out_specs=pl.BlockSpec((1, COND_ROWS, MOD_BLK), lambda l, j: (l, 0, j)),
        out_shape=jax.ShapeDtypeStruct((DEPTH, COND_ROWS, MOD_W), F32),
        scratch_shapes=[pltpu.VMEM((N_COND, D, LANES), F32)],
        compiler_params=_cparams(("arbitrary", "arbitrary")),
        name="modulation",
    )(cond_t, w_ada, b_ada.reshape(DEPTH, 1, MOD_W))


def _rms(x, width):
    return x * lax.rsqrt(jnp.sum(x * x, axis=-1, keepdims=True) * (1.0 / width) + EPS)


def _rope(x, cos, sin):
    lane = lax.broadcasted_iota(jnp.int32, x.shape, 1)
    partner = jnp.where((lane & 1) == 0, pltpu.roll(x, LANES - 1, 1), pltpu.roll(x, 1, 1))
    return x * cos + partner * sin


def _pair_rms(x, gain):
    low = _lane_is_low(x.shape)
    sq = x * x
    tot = jnp.sum(sq, axis=-1, keepdims=True)
    lo = jnp.sum(jnp.where(low, sq, 0.0), axis=-1, keepdims=True)
    ms = jnp.where(low, lo, tot - lo) * (1.0 / NA_DIM)
    return x * lax.rsqrt(ms + EPS) * gain


def _softmax_pv(scores, values):
    m = scores[0].max(axis=-1, keepdims=True)
    for s in scores[1:]:
        m = jnp.maximum(m, s.max(axis=-1, keepdims=True))
    acc = None
    den = None
    for s, v in zip(scores, values):
        p = jnp.exp(s - m)
        d = jnp.sum(p, axis=-1, keepdims=True)
        o = _dot(p.astype(BF16), v)
        acc = o if acc is None else acc + o
        den = d if den is None else den + d
    return acc * (1.0 / den)


def _moe_combine(x1_ref, ys_ref, pc_ref, modp_ref):
    pc = pc_ref[...]
    lane = lax.broadcasted_iota(jnp.int32, (TM, TILE_CAP), 1)
    ys = ys_ref[...]
    y1 = _dot((lane == pc[:, 0:1].astype(jnp.int32)).astype(BF16), ys)
    y2 = _dot((lane == pc[:, 1:2].astype(jnp.int32)).astype(BF16), ys)
    gate2 = modp_ref[0, :, 5 * D:6 * D]
    return x1_ref[...] + gate2 * (pc[:, 2:3] * y1 + pc[:, 3:4] * y2)


def _place_w_in(w_ref, w_bf):
    off_kv = Q_LORA + KV_LORA
    off_kr = off_kv + QK_ROPE
    w_bf[:, 0:off_kv] = w_ref[0, :, 0:off_kv].astype(BF16)
    w_bf[:, COL_KR:COL_KR + LANES] = jnp.zeros((D, LANES), BF16)
    w_bf[:, COL_KR + QK_NOPE:COL_KR + QK_HEAD] = w_ref[0, :, off_kv:off_kr].astype(BF16)
    w_bf[:, COL_CONV:IN_COLS_P] = w_ref[0, :, off_kr:off_kr + (IN_COLS_P - COL_CONV)].astype(BF16)


def _place_w_q(w_ref, w_bf):
    w_bf[...] = jnp.zeros(w_bf.shape, BF16)
    for hd in range(MLA_HEADS):
        w_bf[:, hd * LANES:hd * LANES + QK_HEAD] = w_ref[0, :, hd * QK_HEAD:(hd + 1) * QK_HEAD].astype(BF16)


def _place_w_kv(w_ref, wk_bf, wv_bf):
    wk_bf[...] = jnp.zeros(wk_bf.shape, BF16)
    for hd in range(MLA_HEADS):
        c0 = hd * (QK_NOPE + V_HEAD)
        wk_bf[:, hd * LANES:hd * LANES + QK_NOPE] = w_ref[0, :, c0:c0 + QK_NOPE].astype(BF16)
        wv_bf[:, hd * V_HEAD:(hd + 1) * V_HEAD] = w_ref[0, :, c0 + QK_NOPE:c0 + QK_NOPE + V_HEAD].astype(BF16)


def _pre_kernel(has_moe, *refs):
    if has_moe:
        x1_ref, ys_ref, pc_ref, modp_ref = refs[:4]
        refs = refs[4:]
    else:
        xp_ref, xs_ref = refs[:2]
        refs = refs[2:]
    (mod_ref, n1g_ref, win_ref, qag_ref, kvag_ref, wq_ref, wkv_ref, qng_ref, kng_ref,
     naqg_ref, nakg_ref, cos_ref, sin_ref) = refs[:13]
    refs = refs[13:]
    xo_ref = refs[0]
    refs = refs[1:]
    (q_out, k_out, v_out, ckv_out, kr_out, u_out, naq_out, nak_out, nav_out, nakf_out, navf_out,
     win_bf, wq_bf, wk_bf, wv_bf) = refs

    i = pl.program_id(0)

    @pl.when(i == 0)
    def _():
        _place_w_in(win_ref, win_bf)
        _place_w_q(wq_ref, wq_bf)
        _place_w_kv(wkv_ref, wk_bf, wv_bf)

    if has_moe:
        x = _moe_combine(x1_ref, ys_ref, pc_ref, modp_ref)
    else:
        x = jnp.where(i < PROMPT_TILES, xp_ref[...], xs_ref[...])
    xo_ref[...] = x
    shift = mod_ref[0, :, 0:D]
    scale = mod_ref[0, :, D:2 * D]
    h = _rms(x, D) * n1g_ref[0] * (1.0 + scale) + shift
    proj = _dot(h.astype(BF16), win_bf[...])

    cos = cos_ref[...]
    sin = sin_ref[...]

    q_lat = _rms(proj[:, COL_Q:COL_Q + Q_LORA], Q_LORA) * qag_ref[0]
    q = _dot(q_lat.astype(BF16), wq_bf[...])
    for hd in range(MLA_HEADS):
        qh = q[:, hd * LANES:(hd + 1) * LANES]
        qh = _rope(_rms(qh, QK_HEAD) * qng_ref[0], cos, sin)
        q_out[hd] = (qh * MLA_SCALE).astype(BF16)

    ckv_n = _rms(proj[:, COL_KV:COL_KV + KV_LORA], KV_LORA) * kvag_ref[0]
    ckv_out[...] = ckv_n
    kr = proj[:, COL_KR:COL_KR + LANES]
    kr_out[...] = kr
    ckv_b = ckv_n.astype(BF16)
    k_nope = _dot(ckv_b, wk_bf[...])
    kr_rot = _rope(kr * kng_ref[0], cos, sin)
    kr_ss = jnp.sum(kr * kr, axis=-1, keepdims=True)
    for hd in range(MLA_HEADS):
        kn = k_nope[:, hd * LANES:(hd + 1) * LANES]
        rinv = lax.rsqrt((jnp.sum(kn * kn, axis=-1, keepdims=True) + kr_ss) * (1.0 / QK_HEAD) + EPS)
        k_out[hd] = ((kn * kng_ref[0] + kr_rot) * rinv).astype(BF16)
    v = _dot(ckv_b, wv_bf[...])
    for p in range(MLA_HEADS // 2):
        v_out[p] = v[:, p * LANES:(p + 1) * LANES].astype(BF16)

    a = proj[:, COL_CONV:COL_CONV + CONV_CH]
    g = proj[:, COL_CONV + CONV_CH:COL_CONV + 2 * CONV_CH]
    u_out[...] = a * jax.nn.sigmoid(g)

    for p in range(NA_HEADS // 2):
        sl = slice(p * LANES, (p + 1) * LANES)
        qn = _pair_rms(proj[:, COL_NA + p * LANES:COL_NA + (p + 1) * LANES], naqg_ref[0])
        kn = _pair_rms(proj[:, COL_NA + NA_WIDTH + p * LANES:COL_NA + NA_WIDTH + (p + 1) * LANES],
                       nakg_ref[0])
        vn = proj[:, COL_NA + 2 * NA_WIDTH + p * LANES:COL_NA + 2 * NA_WIDTH + (p + 1) * LANES]
        naq_out[:, sl] = (qn * NA_SCALE).astype(BF16)
        nak_out[:, sl] = kn.astype(BF16)
        nav_out[:, sl] = vn.astype(BF16)
        nakf_out[:, sl] = kn
        navf_out[:, sl] = vn


def _pre(layer, has_moe, x_or_parts, mods3, prm):
    row = lambda i: (i, 0)
    const2 = lambda i: (0, 0)
    lay3 = lambda i: (layer, 0, 0)
    in_specs = []
    args = []
    if has_moe:
        x1, ys, pcol = x_or_parts
        in_specs += [pl.BlockSpec((TM, D), row), pl.BlockSpec((TILE_CAP, D), row),
                     pl.BlockSpec((TM, LANES), row),
                     pl.BlockSpec((1, 1, MOD_W), lambda i: ((layer - 1) * COND_ROWS + _mod_row(i), 0, 0))]
        args += [x1, ys, pcol, mods3]
    else:
        x_prompt, x_sample = x_or_parts
        in_specs += [pl.BlockSpec((TM, D), lambda i: (jnp.minimum(i, PROMPT_TILES - 1), 0)),
                     pl.BlockSpec((TM, D), lambda i: (jnp.maximum(i - PROMPT_TILES, 0), 0))]
        args += [x_prompt, x_sample]
    in_specs += [
        pl.BlockSpec((1, 1, MOD_W), lambda i: (layer * COND_ROWS + _mod_row(i), 0, 0)),
        pl.BlockSpec((1, 1, D), lay3),
        pl.BlockSpec((1, D, IN_COLS), lay3),
        pl.BlockSpec((1, 1, Q_LORA), lay3),
        pl.BlockSpec((1, 1, KV_LORA), lay3),
        pl.BlockSpec((1, Q_LORA, MLA_HEADS * QK_HEAD), lay3),
        pl.BlockSpec((1, KV_LORA, MLA_HEADS * (QK_NOPE + V_HEAD)), lay3),
        pl.BlockSpec((1, 1, LANES), lay3),
        pl.BlockSpec((1, 1, LANES), lay3),
        pl.BlockSpec((1, 1, LANES), lay3),
        pl.BlockSpec((1, 1, LANES), lay3),
        pl.BlockSpec((TM, LANES), lambda i: (_rope_block(i), 0)),
        pl.BlockSpec((TM, LANES), lambda i: (_rope_block(i), 0)),
    ]
    args += [mods3, prm["norm1_g"], prm["w_in"], prm["q_a_g"], prm["kv_a_g"], prm["w_q_up"], prm["w_kv_up"],
             prm["q_norm_g"], prm["k_norm_g"], prm["na_q_g"], prm["na_k_g"], prm["cos"], prm["sin"]]
    out_specs = [pl.BlockSpec((TM, D), row)]
    out_shape = [jax.ShapeDtypeStruct((T_ALL, D), F32)]
    head3 =lambda n: pl.BlockSpec((n, TM, LANES), lambda i: (0, i, 0))
    out_specs += [head3(MLA_HEADS), head3(MLA_HEADS), head3(MLA_HEADS // 2),
                  pl.BlockSpec((TM, KV_LORA), row), pl.BlockSpec((TM, LANES), row),
                  pl.BlockSpec((TM, CONV_CH), row)] + [pl.BlockSpec((TM, NA_WIDTH), row)] * 5
    out_shape += [
        jax.ShapeDtypeStruct((MLA_HEADS, T_ALL, LANES), BF16),
        jax.ShapeDtypeStruct((MLA_HEADS, T_ALL, LANES), BF16),
        jax.ShapeDtypeStruct((MLA_HEADS // 2, T_ALL, LANES), BF16),
        jax.ShapeDtypeStruct((T_ALL, KV_LORA), F32),
        jax.ShapeDtypeStruct((T_ALL, LANES), F32),
        jax.ShapeDtypeStruct((T_ALL, CONV_CH), F32),
        jax.ShapeDtypeStruct((T_ALL, NA_WIDTH), BF16),
        jax.ShapeDtypeStruct((T_ALL, NA_WIDTH), BF16),
        jax.ShapeDtypeStruct((T_ALL, NA_WIDTH), BF16),
        jax.ShapeDtypeStruct((T_ALL, NA_WIDTH), F32),
        jax.ShapeDtypeStruct((T_ALL, NA_WIDTH), F32),
    ]
    return pl.pallas_call(
        functools.partial(_pre_kernel, has_moe),
        grid=(N_TILES,),
        in_specs=in_specs,
        out_specs=out_specs,
        out_shape=out_shape,
        scratch_shapes=[pltpu.VMEM((D, IN_COLS_P), BF16), pltpu.VMEM((Q_LORA, MLA_HEADS * LANES), BF16),
                        pltpu.VMEM((KV_LORA, MLA_HEADS * LANES), BF16),
                        pltpu.VMEM((KV_LORA, MLA_HEADS * V_HEAD), BF16)],
        compiler_params=_cparams(("arbitrary",)),
        name="pre_l%d" % layer,
    )(*args)


def _ctx_kernel(ckv_ref, kr_ref, wkv_ref, kng_ref, kc_out, vc_out, wk_bf, wv_bf):
    @pl.when(pl.program_id(1) == 0)
    def _():
        _place_w_kv(wkv_ref, wk_bf, wv_bf)

    ckv = ckv_ref[0, 0].astype(BF16)
    kr = jnp.concatenate([jnp.zeros((PAST, QK_NOPE), F32), kr_ref[0, 0],
                          jnp.zeros((PAST, LANES - QK_HEAD), F32)], axis=1)
    k_nope = _dot(ckv, wk_bf[...])
    for hd in range(MLA_HEADS):
        kh = k_nope[:, hd * LANES:(hd + 1) * LANES] + kr
        kc_out[0, 0, hd] = (_rms(kh, QK_HEAD) * kng_ref[0]).astype(BF16)
    v = _dot(ckv, wv_bf[...])
    for p in range(MLA_HEADS // 2):
        vc_out[0, 0, p] = v[:, p * LANES:(p + 1) * LANES].astype(BF16)


def _ctx_kv(cache_ckv, cache_krope, prm):
    return pl.pallas_call(
        _ctx_kernel,
        grid=(DEPTH, N_SAMPLE),
        in_specs=[
            pl.BlockSpec((1, 1, PAST, KV_LORA), lambda l, b: (b, l, 0, 0)),
            pl.BlockSpec((1, 1, PAST, QK_ROPE), lambda l, b: (b, l, 0, 0)),
            pl.BlockSpec((1, KV_LORA, MLA_HEADS * (QK_NOPE + V_HEAD)), lambda l, b: (l, 0, 0)),
            pl.BlockSpec((1, 1, LANES), lambda l, b: (l, 0, 0)),
        ],
        out_specs=[
            pl.BlockSpec((1, 1, MLA_HEADS, PAST, LANES), lambda l, b: (l, b, 0, 0, 0)),
            pl.BlockSpec((1, 1, MLA_HEADS // 2, PAST, LANES), lambda l, b: (l, b, 0, 0, 0)),
        ],
        out_shape=[
            jax.ShapeDtypeStruct((DEPTH, N_SAMPLE, MLA_HEADS, PAST, LANES), BF16),
            jax.ShapeDtypeStruct((DEPTH, N_SAMPLE, MLA_HEADS // 2, PAST, LANES), BF16),
        ],
        scratch_shapes=[pltpu.VMEM((KV_LORA, MLA_HEADS * LANES), BF16),
                        pltpu.VMEM((KV_LORA, MLA_HEADS * V_HEAD), BF16)],
        compiler_params=_cparams(("arbitrary", "arbitrary")),
        name="ctx_kv",
    )(cache_ckv, cache_krope, prm["w_kv_up"], prm["k_norm_g"])


N_DR = 2 * NA_WIN_H - 1
N_DC = 2 * NA_WIN_W - 1


def _bias_kernel(rpb_ref, o_ref, blk_ref):
    cq = lax.broadcasted_iota(jnp.int32, (GRID_W, GRID_W), 0)
    ck = lax.broadcasted_iota(jnp.int32, (GRID_W, GRID_W), 1)
    dc = jnp.clip(ck - cq + (NA_WIN_W - 1), 0, N_DC - 1)
    c0 = jnp.clip(cq - NA_WIN_W // 2, 0, GRID_W - NA_WIN_W)
    in_win = (ck >= c0) & (ck < c0 + NA_WIN_W)
    for dr in range(N_DR):
        acc = jnp.zeros((GRID_W, GRID_W), F32)
        for j in range(N_DC):
            acc = jnp.where(dc == j, rpb_ref[pl.program_id(0), pl.program_id(1), dr, j], acc)
        blk_ref[dr] = jnp.where(in_win, acc, NEG_INF)
    blk_ref[N_DR] = jnp.full((GRID_W, GRID_W), NEG_INF, F32)
    idx = _na_block_index()
    for i in range(TILES_PER_SAMPLE):
        for rq in range(TM // GRID_W):
            for pp in range(NA_KEY_ROWS // 2):
                pair = jnp.concatenate([blk_ref[int(idx[i, rq, 2 * pp])], blk_ref[int(idx[i, rq, 2 * pp + 1])]],
                                       axis=1)
                o_ref[0, 0, i, rq * GRID_W:(rq + 1) * GRID_W, pp * LANES:(pp + 1) * LANES] = pair


def _na_bias(na_rpb):
    return pl.pallas_call(
        _bias_kernel,
        grid=(DEPTH, NA_HEADS),
        in_specs=[pl.BlockSpec(memory_space=pltpu.SMEM)],
        out_specs=pl.BlockSpec((1, 1, TILES_PER_SAMPLE, TM, NA_KEYS), lambda l, h: (l, h, 0, 0, 0)),
        out_shape=jax.ShapeDtypeStruct((DEPTH, NA_HEADS, TILES_PER_SAMPLE, TM, NA_KEYS), F32),
        scratch_shapes=[pltpu.VMEM((N_DR + 1, GRID_W, GRID_W), F32)],
        compiler_params=_cparams(("arbitrary", "arbitrary")),
        name="na_bias",
    )(na_rpb)


def _na_block_index():
    idx = np.full((TILES_PER_SAMPLE, TM // GRID_W, NA_KEY_ROWS), N_DR, np.int32)
    for i in range(TILES_PER_SAMPLE):
        ks = _na_key_start_row(i)
        for rq in range(TM // GRID_W):
            r = i * (TM // GRID_W) + rq
            r0 = min(max(r - NA_WIN_H // 2, 0), GRID_ROWS - NA_WIN_H)
            for rk in range(NA_KEY_ROWS):
                kr = ks + rk
                if r0 <= kr < r0 + NA_WIN_H:
                    idx[i, rq, rk] = kr - r + (NA_WIN_H - 1)
    return idx


def _na_key_start_row(i):
    return 0 if i < TILES_PER_SAMPLE // 2 else GRID_ROWS - NA_KEY_ROWS


def _prompt_attn_kernel(q_ref, k_ref, v_ref, nq_ref, nk_ref, nv_ref, om_ref, on_ref):
    low = _lane_is_low((PROMPT_LEN, LANES))
    for p in range(MLA_HEADS // 2):
        outs = []
        for hh in range(2):
            hd = 2 * p + hh
            s = _dot_t(q_ref[hd], k_ref[hd])
            outs.append(_softmax_pv([s], [v_ref[p]]))
        om_ref[p] = jnp.where(low, outs[0], outs[1]).astype(BF16)
    for p in range(NA_HEADS // 2):
        sl = slice(p * LANES, (p + 1) * LANES)
        qp = nq_ref[:, sl]
        kp = nk_ref[:, sl]
        vp = nv_ref[:, sl]
        outs = []
        for hh in range(2):
            qm = jnp.where(low if hh == 0 else ~low, qp, jnp.zeros_like(qp))
            outs.append(_softmax_pv([_dot_t(qm, kp)], [vp]))
        on_ref[:, sl] = jnp.where(low, outs[0], outs[1]).astype(BF16)


def _prompt_attn(q, k, v, nq, nk, nv):
    head3 = lambda n: pl.BlockSpec((n, PROMPT_LEN, LANES), lambda b: (0, b, 0))
    row = pl.BlockSpec((PROMPT_LEN, NA_WIDTH), lambda b: (b, 0))
    return pl.pallas_call(
        _prompt_attn_kernel,
        grid=(N_PROMPT,),
        in_specs=[head3(MLA_HEADS), head3(MLA_HEADS), head3(MLA_HEADS // 2), row, row, row],
        out_specs=[head3(MLA_HEADS // 2), row],
        out_shape=[jax.ShapeDtypeStruct((MLA_HEADS // 2, T_PROMPT, LANES), BF16),
                   jax.ShapeDtypeStruct((T_PROMPT, NA_WIDTH), BF16)],
        compiler_params=_cparams(("arbitrary",)),
        name="prompt_attn",
    )(q, k, v, nq, nk, nv)


def _sample_mla_kernel(q_ref, k_ref, v_ref, kc_ref, vc_ref, o_ref):
    low = _lane_is_low((TM, LANES))
    outs = []
    for hh in range(2):
        q = q_ref[hh]
        s_lat = _dot_t(q, k_ref[hh])
        s_ctx = _dot_t(q, kc_ref[0, 0, hh])
        outs.append(_softmax_pv([s_lat, s_ctx], [v_ref[0], vc_ref[0, 0, 0]]))
    o_ref[0] = jnp.where(low, outs[0], outs[1]).astype(BF16)


def _sample_mla(layer, q, k, v, kc, vc):
    seq_blk0 = T_PROMPT // SAMPLE_LEN
    return pl.pallas_call(
        _sample_mla_kernel,
        grid=(N_SAMPLE, MLA_HEADS // 2, TILES_PER_SAMPLE),
        in_specs=[
            pl.BlockSpec((2, TM, LANES), lambda b, p, i: (p, PROMPT_TILES + b * TILES_PER_SAMPLE + i, 0)),
            pl.BlockSpec((2, SAMPLE_LEN, LANES), lambda b, p, i: (p, seq_blk0 + b, 0)),
            pl.BlockSpec((1, SAMPLE_LEN, LANES), lambda b, p, i: (p, seq_blk0 + b, 0)),
            pl.BlockSpec((1, 1, 2, PAST, LANES), lambda b, p, i: (layer, b, p, 0, 0)),
            pl.BlockSpec((1, 1, 1, PAST, LANES), lambda b, p, i: (layer, b, p, 0, 0)),
        ],
        out_specs=pl.BlockSpec((1, TM, LANES), lambda b, p, i: (p, b * TILES_PER_SAMPLE + i, 0)),
        out_shape=jax.ShapeDtypeStruct((MLA_HEADS // 2, T_SAMPLE, LANES), BF16),
        compiler_params=_cparams(("arbitrary", "arbitrary", "arbitrary")),
        name="sample_mla",
    )(q, k, v, kc, vc)


def _sample_na_kernel(q_ref, k_ref, v_ref, kc_ref, vc_ref, b_ref, o_ref):
    i = pl.program_id(2)
    start = pl.multiple_of(
        jnp.where(i < TILES_PER_SAMPLE // 2, 0, (GRID_ROWS - NA_KEY_ROWS) * GRID_W), TM)
    k_win = k_ref[pl.ds(start, NA_KEYS), :]
    v_win = v_ref[pl.ds(start, NA_KEYS), :]
    kc = kc_ref[0, 0].astype(BF16)
    vc = vc_ref[0, 0].astype(BF16)
    q = q_ref[...]
    low = _lane_is_low((TM, LANES))
    outs = []
    for hh in range(2):
        qm = jnp.where(low if hh == 0 else ~low, q, jnp.zeros_like(q))
        s_loc = _dot_t(qm, k_win) + b_ref[0, hh, 0]
        s_ctx = _dot_t(qm, kc)
        outs.append(_softmax_pv([s_loc, s_ctx], [v_win, vc]))
    o_ref[...] = jnp.where(low, outs[0], outs[1]).astype(BF16)


def _sample_na(layer, nq, nk, nv, cache_k, cache_v, bias):
    seq_blk0 = T_PROMPT // SAMPLE_LEN
    return pl.pallas_call(
        _sample_na_kernel,
        grid=(N_SAMPLE, NA_HEADS // 2, TILES_PER_SAMPLE),
        in_specs=[
            pl.BlockSpec((TM, LANES), lambda b, p, i: (PROMPT_TILES + b * TILES_PER_SAMPLE + i, p)),
            pl.BlockSpec((SAMPLE_LEN, LANES), lambda b, p, i: (seq_blk0 + b, p)),
            pl.BlockSpec((SAMPLE_LEN, LANES), lambda b, p, i: (seq_blk0 + b, p)),
            pl.BlockSpec((1, 1, PAST, LANES), lambda b, p, i: (b, layer, 0, p)),
            pl.BlockSpec((1, 1, PAST, LANES), lambda b, p, i: (b, layer, 0, p)),
            pl.BlockSpec((1, 2, 1, TM, NA_KEYS), lambda b, p, i: (layer, p, i, 0, 0)),
        ],
        out_specs=pl.BlockSpec((TM, LANES), lambda b, p, i: (b * TILES_PER_SAMPLE + i, p)),
        out_shape=jax.ShapeDtypeStruct((T_SAMPLE, NA_WIDTH), BF16),
        compiler_params=_cparams(("arbitrary", "arbitrary", "arbitrary")),
        name="sample_na",
    )(nq, nk, nv, cache_k, cache_v, bias)


CONV_PAD = 16


def _argmax_first(cols):
    best = cols[0]
    idx = jnp.zeros(best.shape, jnp.int32)
    for j in range(1, len(cols)):
        upd = cols[j] > best
        idx = jnp.where(upd, j, idx)
        best = jnp.where(upd, cols[j], best)
    return idx


def _route(aff, bias):
    sel = aff + bias
    rows = [sel[e:e + 1, :] for e in range(N_EXPERTS)]
    group_scores = []
    for g in range(N_GROUPS):
        c = rows[g * EPG:(g + 1) * EPG]
        best = None
        for a in range(EPG):
            for b in range(a + 1, EPG):
                s = c[a] + c[b]
                best = s if best is None else jnp.maximum(best, s)
        group_scores.append(best)
    g_idx = _argmax_first(group_scores)
    in_group = []
    for j in range(EPG):
        v = rows[j]
        for g in range(1, N_GROUPS):
            v = jnp.where(g_idx == g, rows[g * EPG + j], v)
        in_group.append(v)
    i1 = _argmax_first(in_group)
    masked = [jnp.where(i1 == j, -jnp.inf, in_group[j]) for j in range(EPG)]
    i2 = _argmax_first(masked)
    e1 = g_idx * EPG + i1
    e2 = g_idx * EPG + i2
    sub = lax.broadcasted_iota(jnp.int32, aff.shape, 0)
    w1 = jnp.sum(jnp.where(sub == e1, aff, 0.0), axis=0, keepdims=True)
    w2 = jnp.sum(jnp.where(sub == e2, aff, 0.0), axis=0, keepdims=True)
    tot = w1 + w2
    return e1, e2, w1 / tot, w2 / tot


def _sort_tile(e1, e2):
    sub = lax.broadcasted_iota(jnp.int32, (N_EXPERTS, TM), 0)
    oh1 = sub == e1
    oh2 = sub == e2
    one = jnp.ones((N_EXPERTS, TM), F32)
    cnt1 = jnp.sum(jnp.where(oh1, one, 0.0), axis=1, keepdims=True) * one
    cnt2 = jnp.sum(jnp.where(oh2, one, 0.0), axis=1, keepdims=True) * one
    tot = (cnt1 + cnt2).astype(jnp.int32)
    padded = (tot + (RUN_ALIGN - 1)) & (-RUN_ALIGN)
    ee = lax.broadcasted_iota(jnp.int32, (N_EXPERTS, N_EXPERTS), 0)
    ef = lax.broadcasted_iota(jnp.int32, (N_EXPERTS, N_EXPERTS), 1)
    start = _dot((ef < ee).astype(BF16), padded.astype(BF16))
    ta = lax.broadcasted_iota(jnp.int32, (TM, TM), 0)
    tb = lax.broadcasted_iota(jnp.int32, (TM, TM), 1)
    before = (ta < tb).astype(BF16)
    rank1 = _dot(oh1.astype(BF16), before)
    rank2 = _dot(oh2.astype(BF16), before)
    pos1 = jnp.sum(jnp.where(oh1, start + rank1, 0.0), axis=0, keepdims=True)
    pos2 = jnp.sum(jnp.where(oh2, start + cnt1 + rank2, 0.0), axis=0, keepdims=True)
    return pos1.astype(jnp.int32), pos2.astype(jnp.int32), start.astype(jnp.int32), padded


def _post_kernel(x_ref, omp_ref, oms_ref, up_ref, uc_ref, un_ref, onp_ref, ons_ref, mod_ref, cw_ref, cb_ref,
                 lg_ref, lb_ref, wo_ref, n2g_ref, wr_ref, br_ref, x1_out, xs_out, pc_out, st_out, pd_out,
                 wo_bf, ext_ref):
    i = pl.program_id(0)

    @pl.when(i == 0)
    def _():
        wo_bf[...] = wo_ref[0].astype(BF16)

    pos_in_seq = (i - PROMPT_TILES) % TILES_PER_SAMPLE
    is_first = (i < PROMPT_TILES) | (pos_in_seq == 0)
    is_last = (i < PROMPT_TILES) | (pos_in_seq == TILES_PER_SAMPLE - 1)
    ext_ref[0:CONV_PAD, :] = jnp.where(is_first, 0.0, up_ref[TM - CONV_PAD:TM, :])
    ext_ref[CONV_PAD:CONV_PAD + TM, :] = uc_ref[...]
    ext_ref[CONV_PAD + TM:2 * CONV_PAD + TM, :] = jnp.where(is_last, 0.0, un_ref[0:CONV_PAD, :])
    acc = jnp.zeros((TM, CONV_CH), F32)
    for j in range(CONV_K):
        off = CONV_PAD - CONV_K // 2 + j
        acc = acc + ext_ref[off:off + TM, :] * cw_ref[0, j:j + 1, :]
    conv = acc + cb_ref[0]
    cc = conv - jnp.mean(conv, axis=-1, keepdims=True)
    var = jnp.mean(cc * cc, axis=-1, keepdims=True)
    o_conv = _silu(cc * lax.rsqrt(var + EPS) * lg_ref[0] + lb_ref[0])

    is_prompt = i < PROMPT_TILES
    mix = jnp.concatenate([jnp.where(is_prompt, omp_ref[p], oms_ref[p]) for p in range(MLA_HEADS // 2)]
                          + [o_conv.astype(BF16), jnp.where(is_prompt, onp_ref[...], ons_ref[...])],
                          axis=-1)
    att = _dot(mix, wo_bf[...])
    gate1 = mod_ref[0, :, 2 * D:3 * D]
    x1 = x_ref[...] + gate1 * att
    x1_out[...] = x1
    shift2 = mod_ref[0, :, 3 * D:4 * D]
    scale2 = mod_ref[0, :, 4 * D:5 * D]
    h2 = _rms(x1, D) * n2g_ref[0] * (1.0 + scale2) + shift2

    logits = lax.dot_general(wr_ref[...], h2, (((1,), (1,)), ((), ())), preferred_element_type=F32,
                             precision=lax.Precision.HIGHEST)
    e1, e2, w1, w2 = _route(jax.nn.sigmoid(logits), br_ref[...])
    pos1, pos2, start, padded = _sort_tile(e1, e2)
    st_out[0] = start[:, 0:LANES]
    pd_out[0] = padded[:, 0:LANES]
    r = lax.broadcasted_iota(jnp.int32, (TILE_CAP, TM), 0)
    perm = ((r == pos1) | (r == pos2)).astype(BF16)
    xs_out[...] = _dot(perm, h2.astype(BF16)).astype(BF16)
    sub = lax.broadcasted_iota(jnp.int32, (LANES, TM), 0)
    packed = jnp.where(sub == 0, pos1.astype(F32),
                       jnp.where(sub == 1, pos2.astype(F32),
                                 jnp.where(sub == 2, w1, jnp.where(sub == 3, w2, 0.0))))
    pc_out[...] = packed.T


def _post(layer, x, om_p, om_s, u, on_p, on_s, mods3, prm):
    row = lambda i: (i, 0)
    lay3 = lambda i: (layer, 0, 0)
    return pl.pallas_call(
        _post_kernel,
        grid=(N_TILES,),
        in_specs=[
            pl.BlockSpec((TM, D), row),
            pl.BlockSpec((MLA_HEADS // 2, TM, LANES), lambda i: (0, jnp.minimum(i, PROMPT_TILES - 1), 0)),
            pl.BlockSpec((MLA_HEADS // 2, TM, LANES), lambda i: (0, jnp.maximum(i - PROMPT_TILES, 0), 0)),
            pl.BlockSpec((TM, CONV_CH), lambda i: (jnp.maximum(i - 1, 0), 0)),
            pl.BlockSpec((TM, CONV_CH), row),
            pl.BlockSpec((TM, CONV_CH), lambda i: (jnp.minimum(i + 1, N_TILES - 1), 0)),
            pl.BlockSpec((TM, NA_WIDTH), lambda i: (jnp.minimum(i, PROMPT_TILES - 1), 0)),
            pl.BlockSpec((TM, NA_WIDTH), lambda i: (jnp.maximum(i - PROMPT_TILES, 0), 0)),
            pl.BlockSpec((1, 1, MOD_W), lambda i: (layer * COND_ROWS + _mod_row(i), 0, 0)),
            pl.BlockSpec((1, CONV_K, CONV_CH), lay3),
            pl.BlockSpec((1, 1, CONV_CH), lay3),
            pl.BlockSpec((1, 1, CONV_CH), lay3),
            pl.BlockSpec((1, 1, CONV_CH), lay3),
            pl.BlockSpec((1, D, D), lay3),
            pl.BlockSpec((1, 1, D), lay3),
            pl.BlockSpec((N_EXPERTS, D), lambda i: (0, 0)),
            pl.BlockSpec((N_EXPERTS, TM), lambda i: (0, 0)),
        ],
        out_specs=[pl.BlockSpec((TM, D), row), pl.BlockSpec((TILE_CAP, D), row),
                   pl.BlockSpec((TM, LANES), row),
                   pl.BlockSpec((1, N_EXPERTS, LANES), lambda i: (i, 0, 0)),
                   pl.BlockSpec((1, N_EXPERTS, LANES), lambda i: (i, 0, 0))],
        out_shape=[jax.ShapeDtypeStruct((T_ALL, D), F32),
                   jax.ShapeDtypeStruct((N_TILES * TILE_CAP, D), BF16),
                   jax.ShapeDtypeStruct((T_ALL, LANES), F32),
                   jax.ShapeDtypeStruct((N_TILES, N_EXPERTS, LANES), jnp.int32),
                   jax.ShapeDtypeStruct((N_TILES, N_EXPERTS, LANES), jnp.int32)],
        scratch_shapes=[pltpu.VMEM((D, D), BF16), pltpu.VMEM((TM + 2 * CONV_PAD, CONV_CH), F32)],
        compiler_params=_cparams(("arbitrary",)),
        name="post_l%d" % layer,
    )(x, om_p, om_s, u, u, u, on_p, on_s, mods3, prm["conv_w"], prm["conv_b"], prm["conv_ln_g"], prm["conv_ln_b"],
      prm["w_out"], prm["norm2_g"], prm["w_router_t"], prm["b_router_t"])


def _for_each_piece(n, fn):
    for sz in RUN_SIZES:
        @pl.when((n & sz) != 0)
        def _():
            fn(n & (-2 * sz), sz)


def _expert_kernel(st_ref, pd_ref, xs_hbm, wg_ref, wu_ref, wd_ref, ys_hbm, xy, wg_bf, wu_bf, wd_bf, zeros_buf,
                   sems):
    e = pl.program_id(0)
    sem_in = (sems.at[0], sems.at[1])
    sem_out = (sems.at[2], sems.at[3])
    sem_tail = sems.at[4]

    def start_half(ex, half, inbound):
        def body(jj, cur):
            j = half * HALF_TILES + jj
            start = st_ref[j * N_EXPERTS + ex]
            n = pd_ref[j * N_EXPERTS + ex]

            def piece(off, sz):
                rows_hbm = pl.ds(pl.multiple_of(j * TILE_CAP + start + off, RUN_ALIGN), sz)
                rows_buf = pl.ds(pl.multiple_of(cur + off, RUN_ALIGN), sz)
                if inbound:
                    pltpu.make_async_copy(xs_hbm.at[rows_hbm], xy.at[half, rows_buf], sem_in[half]).start()
                else:
                    pltpu.make_async_copy(xy.at[half, rows_buf], ys_hbm.at[rows_hbm], sem_out[half]).start()

            _for_each_piece(n, piece)
            return cur + n
        lax.fori_loop(0, HALF_TILES, body, 0)

    def half_rows(ex, half):
        return lax.fori_loop(
            0, HALF_TILES, lambda jj, c: c + pd_ref[(half * HALF_TILES + jj) * N_EXPERTS + ex], 0)

    def wait_half(ex, half, inbound):
        total = half_rows(ex, half)
        for sz in WAIT_SIZES:
            @pl.when((total & sz) != 0)
            def _():
                rows = pl.ds(0, sz)
                if inbound:
                    pltpu.make_async_copy(xs_hbm.at[rows], xy.at[half, rows], sem_in[half]).wait()
                else:
                    pltpu.make_async_copy(xy.at[half, rows], ys_hbm.at[rows], sem_out[half]).wait()
        return total

    def compute_half(half, total):
        def chunk(c, carry):
            rows = pl.ds(pl.multiple_of(c * MOE_CHUNK, MOE_CHUNK), MOE_CHUNK)
            x = xy[half, rows, :]
            hid = _silu(_dot(x, wg_bf[...])) * _dot(x, wu_bf[...])
            xy[half, rows, :] = _dot(hid.astype(BF16), wd_bf[...]).astype(BF16)
            return carry
        lax.fori_loop(0, (total + (MOE_CHUNK - 1)) // MOE_CHUNK, chunk, 0)

    @pl.when(e == 0)
    def _():
        xy[...] = jnp.zeros(xy.shape, BF16)
        zeros_buf[...] = jnp.zeros(zeros_buf.shape, BF16)
        start_half(e, 0, True)

    @pl.when(e > 0)
    def _():
        wait_half(e - 1, 1, False)

    start_half(e, 1, True)
    wg_bf[...] = wg_ref[0, 0].astype(BF16)
    wu_bf[...] = wu_ref[0, 0].astype(BF16)
    wd_bf[...] = wd_ref[0, 0].astype(BF16)
    for half in range(2):
        compute_half(half, wait_half(e, half, True))
        start_half(e, half, False)
    wait_half(e, 0, False)

    @pl.when(e < N_EXPERTS - 1)
    def _():
        start_half(e + 1, 0, True)

    @pl.when(e == N_EXPERTS - 1)
    def _():
        wait_half(e, 1, False)

        def tail_copy(j, off, sz):
            used = st_ref[j * N_EXPERTS + e] + pd_ref[j * N_EXPERTS + e]
            rows_hbm = pl.ds(pl.multiple_of(j * TILE_CAP + used + off, RUN_ALIGN), sz)
            return pltpu.make_async_copy(zeros_buf.at[pl.ds(0, sz)], ys_hbm.at[rows_hbm], sem_tail)

        def tail_len(j):
            return TILE_CAP - (st_ref[j * N_EXPERTS + e] + pd_ref[j * N_EXPERTS + e])

        def t_start(j, c):
            _for_each_piece(tail_len(j), lambda off, sz: tail_copy(j, off, sz).start())
            return c

        def t_wait(j, c):
            _for_each_piece(tail_len(j), lambda off, sz: tail_copy(j, off, sz).wait())
            return c

        lax.fori_loop(0, N_TILES, t_start, 0)
        lax.fori_loop(0, N_TILES, t_wait, 0)


def _experts(layer, starts, pads, xs, prm):
    wspec = lambda shape: pl.BlockSpec((1, 1) + shape, lambda e, st, pd: (layer, e, 0, 0))
    return pl.pallas_call(
        _expert_kernel,
        grid_spec=pltpu.PrefetchScalarGridSpec(
            num_scalar_prefetch=2,
            grid=(N_EXPERTS,),
            in_specs=[pl.BlockSpec(memory_space=pl.ANY), wspec((D, EXPERT_FF)), wspec((D, EXPERT_FF)),
                      wspec((EXPERT_FF, D))],
            out_specs=pl.BlockSpec(memory_space=pl.ANY),
            scratch_shapes=[pltpu.VMEM((2, HALF_ROWS, D), BF16), pltpu.VMEM((D, EXPERT_FF), BF16),
                            pltpu.VMEM((D, EXPERT_FF), BF16), pltpu.VMEM((EXPERT_FF, D), BF16),
                            pltpu.VMEM((RUN_SIZES[0], D), BF16),
                            pltpu.SemaphoreType.DMA((5,))]),
        out_shape=jax.ShapeDtypeStruct((N_TILES * TILE_CAP, D), BF16),
        compiler_params=_cparams(("arbitrary",)),
        name="experts_l%d" % layer,
    )(starts, pads, xs, prm["w_e_gate"], prm["w_e_up"], prm["w_e_down"])


def _final_kernel(x1_ref, ys_ref, pc_ref, mod_ref, op_ref, os_ref):
    x = _moe_combine(x1_ref, ys_ref, pc_ref, mod_ref)
    i = pl.program_id(0)

    @pl.when(i < PROMPT_TILES)
    def _():
        op_ref[...] = x

    @pl.when(i >= PROMPT_TILES)
    def _():
        os_ref[...] = x


def _final(x1, ys, pcol, mods3):
    row = lambda i: (i, 0)
    return pl.pallas_call(
        _final_kernel,
        grid=(N_TILES,),
        in_specs=[pl.BlockSpec((TM, D), row), pl.BlockSpec((TILE_CAP, D), row), pl.BlockSpec((TM, LANES), row),
                  pl.BlockSpec((1, 1, MOD_W), lambda i: ((DEPTH - 1) * COND_ROWS + _mod_row(i), 0, 0))],
        out_specs=[pl.BlockSpec((TM, D), lambda i: (jnp.minimum(i, PROMPT_TILES - 1), 0)),
                   pl.BlockSpec((TM, D), lambda i: (jnp.maximum(i - PROMPT_TILES, 0), 0))],
        out_shape=[jax.ShapeDtypeStruct((T_PROMPT, D), F32), jax.ShapeDtypeStruct((T_SAMPLE, D), F32)],
        compiler_params=_cparams(("arbitrary",)),
        name="final_residual",
    )(x1, ys, pcol, mods3)


def _rope_tables():
    t = np.arange(SAMPLE_LEN)
    n_freq = QK_ROPE // 4
    inv_freq = ROPE_THETA ** (-np.arange(n_freq, dtype=np.float32) / n_freq)
    row = (t // GRID_W).astype(np.float32)
    col = (t % GRID_W).astype(np.float32)
    ang = jnp.concatenate([jnp.asarray(row)[:, None] * inv_freq, jnp.asarray(col)[:, None] * inv_freq],
                          axis=-1)
    cos_p = jnp.repeat(jnp.cos(ang), 2, axis=-1)
    sin_p = jnp.repeat(jnp.sin(ang), 2, axis=-1) * jnp.tile(jnp.asarray([-1.0, 1.0], F32), QK_ROPE // 2)
    pad_l = jnp.ones((SAMPLE_LEN, QK_NOPE), F32)
    pad_r = jnp.ones((SAMPLE_LEN, LANES - QK_HEAD), F32)
    cos = jnp.concatenate([pad_l, cos_p, pad_r], axis=-1)
    sin = jnp.concatenate([0 * pad_l, sin_p, 0 * pad_r], axis=-1)
    ident_c = jnp.ones((TM, LANES), F32)
    return jnp.concatenate([ident_c, cos], axis=0), jnp.concatenate([0 * ident_c, sin], axis=0)


def _pad_lanes(x, lo, total):
    pad = [(0, 0)] * (x.ndim - 1) + [(lo, total - lo - x.shape[-1])]
    return jnp.pad(x, pad)


def kernel(x_prompt, x_sample, cache_mla_ckv, cache_mla_krope, cache_na_k, cache_na_v, c, c_ctx, w_ada, b_ada, norm1_g, norm2_g, w_in, q_a_g, kv_a_g, w_q_up, w_kv_up, mla_q_norm_g, mla_k_norm_g, conv_dw_w, conv_dw_b, conv_ln_g, conv_ln_b, na_q_norm_g, na_k_norm_g, na_rpb, w_out, w_router, b_router, w_e_gate, w_e_up, w_e_down):
    cos, sin = _rope_tables()
    vec = lambda a: a.reshape(DEPTH, 1, a.shape[-1])
    prm = {
        "norm1_g": vec(norm1_g), "norm2_g": vec(norm2_g), "w_in": w_in, "q_a_g": vec(q_a_g),
        "kv_a_g": vec(kv_a_g), "w_q_up": w_q_up, "w_kv_up": w_kv_up,
        "q_norm_g": vec(_pad_lanes(mla_q_norm_g, 0, LANES)), "k_norm_g": vec(_pad_lanes(mla_k_norm_g, 0, LANES)),
        "na_q_g": vec(jnp.tile(na_q_norm_g, (1, 2))), "na_k_g": vec(jnp.tile(na_k_norm_g, (1, 2))),
        "cos": cos, "sin": sin,
        "conv_w": conv_dw_w, "conv_b": vec(conv_dw_b), "conv_ln_g": vec(conv_ln_g), "conv_ln_b": vec(conv_ln_b),
        "w_out": w_out, "w_router_t": w_router.T,
        "b_router_t": jnp.broadcast_to(b_router[:, None], (N_EXPERTS, TM)),
        "w_e_gate": w_e_gate, "w_e_up": w_e_up, "w_e_down": w_e_down,
    }

    cond = jnp.concatenate([c_ctx[None], c, jnp.zeros((COND_ROWS - N_COND, D), F32)], axis=0)
    mods3 = _modulation(cond.T, w_ada, b_ada).reshape(DEPTH * COND_ROWS, 1, MOD_W)

    kc, vc = _ctx_kv(cache_mla_ckv, cache_mla_krope, prm)
    bias = _na_bias(na_rpb)
    cache_k = cache_na_k.reshape(N_SAMPLE, DEPTH, PAST, NA_WIDTH)
    cache_v = cache_na_v.reshape(N_SAMPLE, DEPTH, PAST, NA_WIDTH)

    x = x1 = ys = pcol = None
    new_ckv, new_kr, new_nak, new_nav = [], [], [], []
    for layer in range(DEPTH):
        if layer == 0:
            outs = _pre(layer, False, (x_prompt.reshape(T_PROMPT, D), x_sample.reshape(T_SAMPLE, D)), mods3, prm)
        else:
            outs = _pre(layer, True, (x1, ys, pcol), mods3, prm)
        x, q, k, v, ckv_n, kr, u, nq, nk, nv, nkf, nvf = outs
        om_p, on_p = _prompt_attn(q, k, v, nq, nk, nv)
        om_s = _sample_mla(layer, q, k, v, kc, vc)
        on_s = _sample_na(layer, nq, nk, nv, cache_k, cache_v, bias)
        x1, xs, pcol, run_start, run_len = _post(layer, x, om_p, om_s, u, on_p, on_s, mods3, prm)
        ys = _experts(layer, run_start[:, :, 0].reshape(-1), run_len[:, :, 0].reshape(-1), xs, prm)
        new_ckv.append(ckv_n[:T_PROMPT].reshape(N_PROMPT, PROMPT_LEN, KV_LORA))
        new_kr.append(kr[:T_PROMPT, QK_NOPE:QK_HEAD].reshape(N_PROMPT, PROMPT_LEN, QK_ROPE))
        new_nak.append(nkf[:T_PROMPT].reshape(N_PROMPT, PROMPT_LEN, NA_HEADS, NA_DIM))
        new_nav.append(nvf[:T_PROMPT].reshape(N_PROMPT, PROMPT_LEN, NA_HEADS, NA_DIM))
    y_prompt, y_sample = _final(x1, ys, pcol, mods3)
    return (y_prompt.reshape(N_PROMPT, PROMPT_LEN, D), y_sample.reshape(N_SAMPLE, SAMPLE_LEN, D),
            jnp.stack(new_ckv, axis=1), jnp.stack(new_kr, axis=1), jnp.stack(new_nak, axis=1),
            jnp.stack(new_nav, axis=1))
```

```python
import functools

import numpy as np
import jax
import jax.numpy as jnp
from jax import lax
from jax.experimental import pallas as pl
from jax.experimental.pallas import tpu as pltpu

F32 = jnp.float32
BF16 = jnp.bfloat16

D = 1024
N_PROMPT = 16
PROMPT_LEN = 256
N_SAMPLE = 4
SAMPLE_LEN = 1024
DEPTH = 4
PAST = 512
GRID_W = 64
GRID_ROWS = SAMPLE_LEN // GRID_W
MLA_HEADS = 8
QK_NOPE = 64
QK_ROPE = 32
QK_HEAD = QK_NOPE + QK_ROPE
V_HEAD = 64
Q_LORA = 256
KV_LORA = 128
CONV_CH = 256
CONV_K = 31
NA_HEADS = 4
NA_DIM = 64
NA_WIDTH = NA_HEADS * NA_DIM
NA_WIN_H = 8
NA_WIN_W = 16
N_EXPERTS = 16
N_GROUPS = 4
EPG = 4
EXPERT_FF = 512
ROPE_THETA = 10000.0
EPS = 1e-6
NEG_INF = -1e30
MLA_SCALE = QK_HEAD ** -0.5
NA_SCALE = NA_DIM ** -0.5

LANES = 128
SUBLANES = 8
TM = 256
T_PROMPT = N_PROMPT * PROMPT_LEN
T_SAMPLE = N_SAMPLE * SAMPLE_LEN
T_ALL = T_PROMPT + T_SAMPLE
N_TILES = T_ALL // TM
PROMPT_TILES = T_PROMPT // TM
TILES_PER_SAMPLE = SAMPLE_LEN // TM
N_COND = 1 + N_SAMPLE
COND_ROWS = 8
MOD_W = 6 * D

COL_Q = 0
COL_KV = Q_LORA
COL_KR = COL_KV + KV_LORA
COL_CONV = COL_KR + LANES
COL_NA = COL_CONV + 2 * CONV_CH
IN_COLS_P = COL_NA + 3 * NA_WIDTH
IN_COLS = IN_COLS_P - (LANES - QK_ROPE)

NA_KEY_ROWS = 12
NA_KEYS = NA_KEY_ROWS * GRID_W

VMEM_LIMIT = 56 * 1024 * 1024

RUN_ALIGN = 16
TILE_CAP = 768
RUN_SIZES = (256, 128, 64, 32, 16)
MOE_CHUNK = 256
HALF_TILES = N_TILES // 2
HALF_ROWS = 4352
WAIT_SIZES = (4096, 2048, 1024, 512, 256, 128, 64, 32, 16)


def _cparams(sem):
    return pltpu.CompilerParams(dimension_semantics=sem, vmem_limit_bytes=VMEM_LIMIT)


def _mod_row(i):
    return jnp.where(i < PROMPT_TILES, 0, 1 + (i - PROMPT_TILES) // TILES_PER_SAMPLE)


def _rope_block(i):
    return jnp.where(i < PROMPT_TILES, 0, 1 + (i - PROMPT_TILES) % TILES_PER_SAMPLE)


def _dot(a, b):
    return jnp.dot(a, b, preferred_element_type=F32)


def _dot_t(a, b):
    return lax.dot_general(a, b, (((1,), (1,)), ((), ())), preferred_element_type=F32)


def _silu(x):
    return x * jax.nn.sigmoid(x)


def _lane_is_low(shape):
    return lax.broadcasted_iota(jnp.int32, shape, len(shape) - 1) < (LANES // 2)


MOD_BLK = 768


def _mod_kernel(ct_ref, w_ref, b_ref, o_ref, cb_ref):
    first = (pl.program_id(0) == 0) & (pl.program_id(1) == 0)

    @pl.when(first)
    def _():
        s = _silu(ct_ref[...])
        for m in range(N_COND):
            cb_ref[m] = jnp.broadcast_to(s[:, m:m + 1], (D, LANES))

    for j in range(MOD_BLK // LANES):
        w = w_ref[0, :, j * LANES:(j + 1) * LANES]
        rows = [jnp.sum(w * cb_ref[m], axis=0, keepdims=True) for m in range(N_COND)]
        rows.append(jnp.zeros((COND_ROWS - N_COND, LANES), F32))
        o_ref[0, :, j * LANES:(j + 1) * LANES] = (
            jnp.concatenate(rows, axis=0) + b_ref[0, :, j * LANES:(j + 1) * LANES])


def _modulation(cond_t, w_ada, b_ada):
    nb = MOD_W // MOD_BLK
    return pl.pallas_call(
        _mod_kernel,
        grid=(DEPTH, nb),
        in_specs=[
            pl.BlockSpec((D, COND_ROWS), lambda l, j: (0, 0)),
            pl.BlockSpec((1, D, MOD_BLK), lambda l, j: (l, 0, j)),
            pl.BlockSpec((1, 1, MOD_BLK), lambda l, j: (l, 0, j)),
        ],
        out_specs=pl.BlockSpec((1, COND_ROWS, MOD_BLK), lambda l, j: (l, 0, j)),
        out_shape=jax.ShapeDtypeStruct((DEPTH, COND_ROWS, MOD_W), F32),
        scratch_shapes=[pltpu.VMEM((N_COND, D, LANES), F32)],
        compiler_params=_cparams(("arbitrary", "arbitrary")),
        name="modulation",
    )(cond_t, w_ada, b_ada.reshape(DEPTH, 1, MOD_W))


def _lane_sums(sq, group):
    k = sq.shape[1]
    hi = sq.astype(BF16)
    lo = (sq - hi.astype(F32)).astype(BF16)
    if group == k:
        sel = jnp.ones((k, LANES), BF16)
    else:
        r = lax.broadcasted_iota(jnp.int32, (k, LANES), 0)
        c = lax.broadcasted_iota(jnp.int32, (k, LANES), 1)
        sel = ((r // group) == (c // group)).astype(BF16)
    return _dot(hi, sel) + _dot(lo, sel)


def _rms(x, width):
    rinv = lax.rsqrt(_lane_sums(x * x, x.shape[1]) * (1.0 / width) + EPS)
    reps = x.shape[1] // LANES
    return x * (rinv if reps == 1 else jnp.concatenate([rinv] * reps, axis=1))


def _rope(x, cos, sin):
    lane = lax.broadcasted_iota(jnp.int32, x.shape, 1)
    partner = jnp.where((lane & 1) == 0, pltpu.roll(x, LANES - 1, 1), pltpu.roll(x, 1, 1))
    return x * cos + partner * sin


def _pair_rms(x, gain):
    ms = _lane_sums(x * x, NA_DIM) * (1.0 / NA_DIM)
    return x * lax.rsqrt(ms + EPS) * gain


def _softmax_pv(scores, values):
    m = scores[0].max(axis=-1, keepdims=True)
    for s in scores[1:]:
        m = jnp.maximum(m, s.max(axis=-1, keepdims=True))
    acc = None
    den = None
    for s, v in zip(scores, values):
        p = jnp.exp(s - m)
        d = jnp.sum(p, axis=-1, keepdims=True)
        o = _dot(p.astype(BF16), v)
        acc = o if acc is None else acc + o
        den = d if den is None else den + d
    return acc * (1.0 / den)


def _moe_combine(x1_ref, ys_ref, pc_ref, modp_ref):
    pc = pc_ref[...]
    lane = lax.broadcasted_iota(jnp.int32, (TM, TILE_CAP), 1)
    ys = ys_ref[...]
    y1 = _dot((lane == pc[:, 0:1].astype(jnp.int32)).astype(BF16), ys)
    y2 = _dot((lane == pc[:, 1:2].astype(jnp.int32)).astype(BF16), ys)
    gate2 = modp_ref[0, :, 5 * D:6 * D]
    return x1_ref[...] + gate2 * (pc[:, 2:3] * y1 + pc[:, 3:4] * y2)


def _place_w_in(w_ref, w_bf):
    off_kv = Q_LORA + KV_LORA
    off_kr = off_kv + QK_ROPE
    w_bf[:, 0:off_kv] = w_ref[0, :, 0:off_kv].astype(BF16)
    w_bf[:, COL_KR:COL_KR + LANES] = jnp.zeros((D, LANES), BF16)
    w_bf[:, COL_KR + QK_NOPE:COL_KR + QK_HEAD] = w_ref[0, :, off_kv:off_kr].astype(BF16)
    w_bf[:, COL_CONV:IN_COLS_P] = w_ref[0, :, off_kr:off_kr + (IN_COLS_P - COL_CONV)].astype(BF16)


def _place_w_q(w_ref, w_bf):
    w_bf[...] = jnp.zeros(w_bf.shape, BF16)
    for hd in range(MLA_HEADS):
        w_bf[:, hd * LANES:hd * LANES + QK_HEAD] = w_ref[0, :, hd * QK_HEAD:(hd + 1) * QK_HEAD].astype(BF16)


def _place_w_kv(w_ref, wk_bf, wv_bf):
    wk_bf[...] = jnp.zeros(wk_bf.shape, BF16)
    for hd in range(MLA_HEADS):
        c0 = hd * (QK_NOPE + V_HEAD)
        wk_bf[:, hd * LANES:hd * LANES + QK_NOPE] = w_ref[0, :, c0:c0 + QK_NOPE].astype(BF16)
        wv_bf[:, hd * V_HEAD:(hd + 1) * V_HEAD] = w_ref[0, :, c0 + QK_NOPE:c0 + QK_NOPE + V_HEAD].astype(BF16)


def _pre_kernel(has_moe, *refs):
    if has_moe:
        x1_ref, ys_ref, pc_ref, modp_ref = refs[:4]
        refs = refs[4:]
    else:
        xp_ref, xs_ref = refs[:2]
        refs = refs[2:]
    (mod_ref, n1g_ref, win_ref, qag_ref, kvag_ref, wq_ref, wkv_ref, qng_ref, kng_ref,
     naqg_ref, nakg_ref, cos_ref, sin_ref) = refs[:13]
    refs = refs[13:]
    xo_ref = refs[0]
    refs = refs[1:]
    (q_out, k_out, v_out, ckv_out, kr_out, u_out, naq_out, nak_out, nav_out, nakf_out, navf_out,
     win_bf, wq_bf, wk_bf, wv_bf, proj_buf) = refs

    s = pl.program_id(0)
    i = jnp.minimum(s, N_TILES - 1)

    @pl.when(s == 0)
    def _():
        _place_w_in(win_ref, win_bf)
        _place_w_q(wq_ref, wq_bf)
        _place_w_kv(wkv_ref, wk_bf, wv_bf)
        proj_buf[...] = jnp.zeros(proj_buf.shape, F32)

    if has_moe:
        x = _moe_combine(x1_ref, ys_ref, pc_ref, modp_ref)
    else:
        x = jnp.where(i < PROMPT_TILES, xp_ref[...], xs_ref[...])
    xo_ref[...] = x
    shift = mod_ref[0, :, 0:D]
    scale = mod_ref[0, :, D:2 * D]
    h = _rms(x, D) * n1g_ref[0] * (1.0 + scale) + shift
    proj = proj_buf[(s + 1) % 2]
    proj_buf[s % 2] = _dot(h.astype(BF16), win_bf[...])

    cos = cos_ref[...]
    sin = sin_ref[...]

    q_lat = _rms(proj[:, COL_Q:COL_Q + Q_LORA], Q_LORA) * qag_ref[0]
    q = _dot(q_lat.astype(BF16), wq_bf[...])
    for hd in range(MLA_HEADS):
        qh = q[:, hd * LANES:(hd + 1) * LANES]
        qh = _rope(_rms(qh, QK_HEAD) * qng_ref[0], cos, sin)
        q_out[hd] = (qh * MLA_SCALE).astype(BF16)

    ckv_n = _rms(proj[:, COL_KV:COL_KV + KV_LORA], KV_LORA) * kvag_ref[0]
    ckv_out[...] = ckv_n
    kr = proj[:, COL_KR:COL_KR + LANES]
    kr_out[...] = kr
    ckv_b = ckv_n.astype(BF16)
    k_nope = _dot(ckv_b, wk_bf[...])
    kr_rot = _rope(kr * kng_ref[0], cos, sin)
    kr_sq = kr * kr
    for hd in range(MLA_HEADS):
        kn = k_nope[:, hd * LANES:(hd + 1) * LANES]
        rinv = lax.rsqrt(_lane_sums(kn * kn + kr_sq, LANES) * (1.0 / QK_HEAD) + EPS)
        k_out[hd] = ((kn * kng_ref[0] + kr_rot) * rinv).astype(BF16)
    v = _dot(ckv_b, wv_bf[...])
    for p in range(MLA_HEADS // 2):
        v_out[p] = v[:, p * LANES:(p + 1) * LANES].astype(BF16)

    a = proj[:, COL_CONV:COL_CONV + CONV_CH]
    g = proj[:, COL_CONV + CONV_CH:COL_CONV + 2 * CONV_CH]
    u_out[...] = a * jax.nn.sigmoid(g)

    for p in range(NA_HEADS // 2):
        sl = slice(p * LANES, (p + 1) * LANES)
        qn = _pair_rms(proj[:, COL_NA + p * LANES:COL_NA + (p + 1) * LANES], naqg_ref[0])
        kn = _pair_rms(proj[:, COL_NA + NA_WIDTH + p * LANES:COL_NA + NA_WIDTH + (p + 1) * LANES],
                       nakg_ref[0])
        vn = proj[:, COL_NA + 2 * NA_WIDTH + p * LANES:COL_NA + 2 * NA_WIDTH + (p + 1) * LANES]
        naq_out[:, sl] = (qn * NA_SCALE).astype(BF16)
        nak_out[:, sl] = kn.astype(BF16)
        nav_out[:, sl] = vn.astype(BF16)
        nakf_out[:, sl] = kn
        navf_out[:, sl] = vn


def _pre(layer, has_moe, x_or_parts, mods3, prm):
    cur = lambda s: jnp.minimum(s, N_TILES - 1)
    prev = lambda s: jnp.maximum(s - 1, 0)
    row = lambda s: (cur(s), 0)
    row2 = lambda s: (prev(s), 0)
    lay3 = lambda s: (layer, 0, 0)
    in_specs = []
    args = []
    if has_moe:
        x1, ys, pcol = x_or_parts
        in_specs += [pl.BlockSpec((TM, D), row), pl.BlockSpec((TILE_CAP, D), row),
                     pl.BlockSpec((TM, LANES), row),
                     pl.BlockSpec((1, 1, MOD_W), lambda s: ((layer - 1) * COND_ROWS + _mod_row(cur(s)), 0, 0))]
        args += [x1, ys, pcol, mods3]
    else:
        x_prompt, x_sample = x_or_parts
        in_specs += [pl.BlockSpec((TM, D), lambda s: (jnp.minimum(s, PROMPT_TILES - 1), 0)),
                     pl.BlockSpec((TM, D), lambda s: (jnp.maximum(cur(s) - PROMPT_TILES, 0), 0))]
        args += [x_prompt, x_sample]
    in_specs += [
        pl.BlockSpec((1, 1, MOD_W), lambda s: (layer * COND_ROWS + _mod_row(cur(s)), 0, 0)),
        pl.BlockSpec((1, 1, D), lay3),
        pl.BlockSpec((1, D, IN_COLS), lay3),
        pl.BlockSpec((1, 1, Q_LORA), lay3),
        pl.BlockSpec((1, 1, KV_LORA), lay3),
        pl.BlockSpec((1, Q_LORA, MLA_HEADS * QK_HEAD), lay3),
        pl.BlockSpec((1, KV_LORA, MLA_HEADS * (QK_NOPE + V_HEAD)), lay3),
        pl.BlockSpec((1, 1, LANES), lay3),
        pl.BlockSpec((1, 1, LANES), lay3),
        pl.BlockSpec((1, 1, LANES), lay3),
        pl.BlockSpec((1, 1, LANES), lay3),
        pl.BlockSpec((TM, LANES), lambda s: (_rope_block(prev(s)), 0)),
        pl.BlockSpec((TM, LANES), lambda s: (_rope_block(prev(s)), 0)),
    ]
    args += [mods3, prm["norm1_g"], prm["w_in"], prm["q_a_g"], prm["kv_a_g"], prm["w_q_up"], prm["w_kv_up"],
             prm["q_norm_g"], prm["k_norm_g"], prm["na_q_g"], prm["na_k_g"], prm["cos"], prm["sin"]]
    out_specs = [pl.BlockSpec((TM, D), row)]
    out_shape = [jax.ShapeDtypeStruct((T_ALL, D), F32)]
    head3 = lambda n: pl.BlockSpec((n, TM, LANES), lambda s: (0, prev(s), 0))
    out_specs += [head3(MLA_HEADS), head3(MLA_HEADS), head3(MLA_HEADS // 2),
                  pl.BlockSpec((TM, KV_LORA), row2), pl.BlockSpec((TM, LANES), row2),
                  pl.BlockSpec((TM, CONV_CH), row2)] + [pl.BlockSpec((TM, NA_WIDTH), row2)] * 5
    out_shape += [
        jax.ShapeDtypeStruct((MLA_HEADS, T_ALL, LANES), BF16),
        jax.ShapeDtypeStruct((MLA_HEADS, T_ALL, LANES), BF16),
        jax.ShapeDtypeStruct((MLA_HEADS // 2, T_ALL, LANES), BF16),
        jax.ShapeDtypeStruct((T_ALL, KV_LORA), F32),
        jax.ShapeDtypeStruct((T_ALL, LANES), F32),
        jax.ShapeDtypeStruct((T_ALL, CONV_CH), F32),
        jax.ShapeDtypeStruct((T_ALL, NA_WIDTH), BF16),
        jax.ShapeDtypeStruct((T_ALL, NA_WIDTH), BF16),
        jax.ShapeDtypeStruct((T_ALL, NA_WIDTH), BF16),
        jax.ShapeDtypeStruct((T_ALL, NA_WIDTH), F32),
        jax.ShapeDtypeStruct((T_ALL, NA_WIDTH), F32),
    ]
    return pl.pallas_call(
        functools.partial(_pre_kernel, has_moe),
        grid=(N_TILES + 1,),
        in_specs=in_specs,
        out_specs=out_specs,
        out_shape=out_shape,
        scratch_shapes=[pltpu.VMEM((D, IN_COLS_P), BF16), pltpu.VMEM((Q_LORA, MLA_HEADS * LANES), BF16),
                        pltpu.VMEM((KV_LORA, MLA_HEADS * LANES), BF16),
                        pltpu.VMEM((KV_LORA, MLA_HEADS * V_HEAD), BF16),
                        pltpu.VMEM((2, TM, IN_COLS_P), F32)],
        compiler_params=_cparams(("arbitrary",)),
        name="pre_l%d" % layer,
    )(*args)


def _ctx_kernel(ckv_ref, kr_ref, wkv_ref, kng_ref, kc_out, vc_out, wk_bf, wv_bf):
    @pl.when(pl.program_id(1) == 0)
    def _():
        _place_w_kv(wkv_ref, wk_bf, wv_bf)

    ckv = ckv_ref[0, 0].astype(BF16)
    kr = jnp.concatenate([jnp.zeros((PAST, QK_NOPE), F32), kr_ref[0, 0],
                          jnp.zeros((PAST, LANES - QK_HEAD), F32)], axis=1)
    k_nope = _dot(ckv, wk_bf[...])
    for hd in range(MLA_HEADS):
        kh = k_nope[:, hd * LANES:(hd + 1) * LANES] + kr
        kc_out[0, 0, hd] = (_rms(kh, QK_HEAD) * kng_ref[0]).astype(BF16)
    v = _dot(ckv, wv_bf[...])
    for p in range(MLA_HEADS // 2):
        vc_out[0, 0, p] = v[:, p * LANES:(p + 1) * LANES].astype(BF16)


def _ctx_kv(cache_ckv, cache_krope, prm):
    return pl.pallas_call(
        _ctx_kernel,
        grid=(DEPTH, N_SAMPLE),
        in_specs=[
            pl.BlockSpec((1, 1, PAST, KV_LORA), lambda l, b: (b, l, 0, 0)),
            pl.BlockSpec((1, 1, PAST, QK_ROPE), lambda l, b: (b, l, 0, 0)),
            pl.BlockSpec((1, KV_LORA, MLA_HEADS * (QK_NOPE + V_HEAD)), lambda l, b: (l, 0, 0)),
            pl.BlockSpec((1, 1, LANES), lambda l, b: (l, 0, 0)),
        ],
        out_specs=[
            pl.BlockSpec((1, 1, MLA_HEADS, PAST, LANES), lambda l, b: (l, b, 0, 0, 0)),
            pl.BlockSpec((1, 1, MLA_HEADS // 2, PAST, LANES), lambda l, b: (l, b, 0, 0, 0)),
        ],
        out_shape=[
            jax.ShapeDtypeStruct((DEPTH, N_SAMPLE, MLA_HEADS, PAST, LANES), BF16),
            jax.ShapeDtypeStruct((DEPTH, N_SAMPLE, MLA_HEADS // 2, PAST, LANES), BF16),
        ],
        scratch_shapes=[pltpu.VMEM((KV_LORA, MLA_HEADS * LANES), BF16),
                        pltpu.VMEM((KV_LORA, MLA_HEADS * V_HEAD), BF16)],
        compiler_params=_cparams(("arbitrary", "arbitrary")),
        name="ctx_kv",
    )(cache_ckv, cache_krope, prm["w_kv_up"], prm["k_norm_g"])


N_DR = 2 * NA_WIN_H - 1
N_DC = 2 * NA_WIN_W - 1


def _bias_kernel(rpb_ref, o_ref, blk_ref):
    cq = lax.broadcasted_iota(jnp.int32, (GRID_W, GRID_W), 0)
    ck = lax.broadcasted_iota(jnp.int32, (GRID_W, GRID_W), 1)
    dc = jnp.clip(ck - cq + (NA_WIN_W - 1), 0, N_DC - 1)
    c0 = jnp.clip(cq - NA_WIN_W // 2, 0, GRID_W - NA_WIN_W)
    in_win = (ck >= c0) & (ck < c0 + NA_WIN_W)
    for dr in range(N_DR):
        acc = jnp.zeros((GRID_W, GRID_W), F32)
        for j in range(N_DC):
            acc = jnp.where(dc == j, rpb_ref[pl.program_id(0), pl.program_id(1), dr, j], acc)
        blk_ref[dr] = jnp.where(in_win, acc, NEG_INF)
    blk_ref[N_DR] = jnp.full((GRID_W, GRID_W), NEG_INF, F32)
    idx = _na_block_index()
    for i in range(TILES_PER_SAMPLE):
        for rq in range(TM // GRID_W):
            for pp in range(NA_KEY_ROWS // 2):
                pair = jnp.concatenate([blk_ref[int(idx[i, rq, 2 * pp])], blk_ref[int(idx[i, rq, 2 * pp + 1])]],
                                       axis=1)
                o_ref[0, 0, i, rq * GRID_W:(rq + 1) * GRID_W, pp * LANES:(pp + 1) * LANES] = pair


def _na_bias(na_rpb):
    return pl.pallas_call(
        _bias_kernel,
        grid=(DEPTH, NA_HEADS),
        in_specs=[pl.BlockSpec(memory_space=pltpu.SMEM)],
        out_specs=pl.BlockSpec((1, 1, TILES_PER_SAMPLE, TM, NA_KEYS), lambda l, h: (l, h, 0, 0, 0)),
        out_shape=jax.ShapeDtypeStruct((DEPTH, NA_HEADS, TILES_PER_SAMPLE, TM, NA_KEYS), F32),
        scratch_shapes=[pltpu.VMEM((N_DR + 1, GRID_W, GRID_W), F32)],
        compiler_params=_cparams(("arbitrary", "arbitrary")),
        name="na_bias",
    )(na_rpb)


def _na_block_index():
    idx = np.full((TILES_PER_SAMPLE, TM // GRID_W, NA_KEY_ROWS), N_DR, np.int32)
    for i in range(TILES_PER_SAMPLE):
        ks = _na_key_start_row(i)
        for rq in range(TM // GRID_W):
            r = i * (TM // GRID_W) + rq
            r0 = min(max(r - NA_WIN_H // 2, 0), GRID_ROWS - NA_WIN_H)
            for rk in range(NA_KEY_ROWS):
                kr = ks + rk
                if r0 <= kr < r0 + NA_WIN_H:
                    idx[i, rq, rk] = kr - r + (NA_WIN_H - 1)
    return idx


def _na_key_start_row(i):
    return 0 if i < TILES_PER_SAMPLE // 2 else GRID_ROWS - NA_KEY_ROWS


def _prompt_attn_kernel(q_ref, k_ref, v_ref, nq_ref, nk_ref, nv_ref, om_ref, on_ref):
    low = _lane_is_low((PROMPT_LEN, LANES))
    for p in range(MLA_HEADS // 2):
        outs = []
        for hh in range(2):
            hd = 2 * p + hh
            s = _dot_t(q_ref[hd], k_ref[hd])
            outs.append(_softmax_pv([s], [v_ref[p]]))
        om_ref[p] = jnp.where(low, outs[0], outs[1]).astype(BF16)
    for p in range(NA_HEADS // 2):
        sl = slice(p * LANES, (p + 1) * LANES)
        qp = nq_ref[:, sl]
        kp = nk_ref[:, sl]
        vp = nv_ref[:, sl]
        outs = []
        for hh in range(2):
            qm = jnp.where(low if hh == 0 else ~low, qp, jnp.zeros_like(qp))
            outs.append(_softmax_pv([_dot_t(qm, kp)], [vp]))
        on_ref[:, sl] = jnp.where(low, outs[0], outs[1]).astype(BF16)


def _prompt_attn(q, k, v, nq, nk, nv):
    head3 = lambda n: pl.BlockSpec((n, PROMPT_LEN, LANES), lambda b: (0, b, 0))
    row = pl.BlockSpec((PROMPT_LEN, NA_WIDTH), lambda b: (b, 0))
    return pl.pallas_call(
        _prompt_attn_kernel,
        grid=(N_PROMPT,),
        in_specs=[head3(MLA_HEADS), head3(MLA_HEADS), head3(MLA_HEADS // 2), row, row, row],
        out_specs=[head3(MLA_HEADS // 2), row],
        out_shape=[jax.ShapeDtypeStruct((MLA_HEADS // 2, T_PROMPT, LANES), BF16),
                   jax.ShapeDtypeStruct((T_PROMPT, NA_WIDTH), BF16)],
        compiler_params=_cparams(("arbitrary",)),
        name="prompt_attn",
    )(q, k, v, nq, nk, nv)


def _sample_mla_kernel(q_ref, k_ref, v_ref, kc_ref, vc_ref, o_ref):
    low = _lane_is_low((TM, LANES))
    outs = []
    for hh in range(2):
        q = q_ref[hh]
        s_lat = _dot_t(q, k_ref[hh])
        s_ctx = _dot_t(q, kc_ref[0, 0, hh])
        outs.append(_softmax_pv([s_lat, s_ctx], [v_ref[0], vc_ref[0, 0, 0]]))
    o_ref[0] = jnp.where(low, outs[0], outs[1]).astype(BF16)


def _sample_mla(layer, q, k, v, kc, vc):
    seq_blk0 = T_PROMPT // SAMPLE_LEN
    return pl.pallas_call(
        _sample_mla_kernel,
        grid=(N_SAMPLE, MLA_HEADS // 2, TILES_PER_SAMPLE),
        in_specs=[
            pl.BlockSpec((2, TM, LANES), lambda b, p, i: (p, PROMPT_TILES + b * TILES_PER_SAMPLE + i, 0)),
            pl.BlockSpec((2, SAMPLE_LEN, LANES), lambda b, p, i: (p, seq_blk0 + b, 0)),
            pl.BlockSpec((1, SAMPLE_LEN, LANES), lambda b, p, i: (p, seq_blk0 + b, 0)),
            pl.BlockSpec((1, 1, 2, PAST, LANES), lambda b, p, i: (layer, b, p, 0, 0)),
            pl.BlockSpec((1, 1, 1, PAST, LANES), lambda b, p, i: (layer, b, p, 0, 0)),
        ],
        out_specs=pl.BlockSpec((1, TM, LANES), lambda b, p, i: (p, b * TILES_PER_SAMPLE + i, 0)),
        out_shape=jax.ShapeDtypeStruct((MLA_HEADS // 2, T_SAMPLE, LANES), BF16),
        compiler_params=_cparams(("arbitrary", "arbitrary", "arbitrary")),
        name="sample_mla",
    )(q, k, v, kc, vc)


def _sample_na_kernel(q_ref, k_ref, v_ref, kc_ref, vc_ref, b_ref, o_ref):
    i = pl.program_id(2)
    start = pl.multiple_of(
        jnp.where(i < TILES_PER_SAMPLE // 2, 0, (GRID_ROWS - NA_KEY_ROWS) * GRID_W), TM)
    k_win = k_ref[pl.ds(start, NA_KEYS), :]
    v_win = v_ref[pl.ds(start, NA_KEYS), :]
    kc = kc_ref[0, 0].astype(BF16)
    vc = vc_ref[0, 0].astype(BF16)
    q = q_ref[...]
    low = _lane_is_low((TM, LANES))
    outs = []
    for hh in range(2):
        qm = jnp.where(low if hh == 0 else ~low, q, jnp.zeros_like(q))
        s_loc = _dot_t(qm, k_win) + b_ref[0, hh, 0]
        s_ctx = _dot_t(qm, kc)
        outs.append(_softmax_pv([s_loc, s_ctx], [v_win, vc]))
    o_ref[...] = jnp.where(low, outs[0], outs[1]).astype(BF16)


def _sample_na(layer, nq, nk, nv, cache_k, cache_v, bias):
    seq_blk0 = T_PROMPT // SAMPLE_LEN
    return pl.pallas_call(
        _sample_na_kernel,
        grid=(N_SAMPLE, NA_HEADS // 2, TILES_PER_SAMPLE),
        in_specs=[
            pl.BlockSpec((TM, LANES), lambda b, p, i: (PROMPT_TILES + b * TILES_PER_SAMPLE + i, p)),
            pl.BlockSpec((SAMPLE_LEN, LANES), lambda b, p, i: (seq_blk0 + b, p)),
            pl.BlockSpec((SAMPLE_LEN, LANES), lambda b, p, i: (seq_blk0 + b, p)),
            pl.BlockSpec((1, 1, PAST, LANES), lambda b, p, i: (b, layer, 0, p)),
            pl.BlockSpec((1, 1, PAST, LANES), lambda b, p, i: (b, layer, 0, p)),
            pl.BlockSpec((1, 2, 1, TM, NA_KEYS), lambda b, p, i: (layer, p, i, 0, 0)),
        ],
        out_specs=pl.BlockSpec((TM, LANES), lambda b, p, i: (b * TILES_PER_SAMPLE + i, p)),
        out_shape=jax.ShapeDtypeStruct((T_SAMPLE, NA_WIDTH), BF16),
        compiler_params=_cparams(("arbitrary", "arbitrary", "arbitrary")),
        name="sample_na",
    )(nq, nk, nv, cache_k, cache_v, bias)


CONV_PAD = 16


def _argmax_first(cols):
    best = cols[0]
    idx = jnp.zeros(best.shape, jnp.int32)
    for j in range(1, len(cols)):
        upd = cols[j] > best
        idx = jnp.where(upd, j, idx)
        best = jnp.where(upd, cols[j], best)
    return idx


def _route(aff, bias):
    sel = aff + bias
    rows = [sel[e:e + 1, :] for e in range(N_EXPERTS)]
    group_scores = []
    for g in range(N_GROUPS):
        c = rows[g * EPG:(g + 1) * EPG]
        best = None
        for a in range(EPG):
            for b in range(a + 1, EPG):
                s = c[a] + c[b]
                best = s if best is None else jnp.maximum(best, s)
        group_scores.append(best)
    g_idx = _argmax_first(group_scores)
    in_group = []
    for j in range(EPG):
        v = rows[j]
        for g in range(1, N_GROUPS):
            v = jnp.where(g_idx == g, rows[g * EPG + j], v)
        in_group.append(v)
    i1 = _argmax_first(in_group)
    masked = [jnp.where(i1 == j, -jnp.inf, in_group[j]) for j in range(EPG)]
    i2 = _argmax_first(masked)
    e1 = g_idx * EPG + i1
    e2 = g_idx * EPG + i2
    sub = lax.broadcasted_iota(jnp.int32, aff.shape, 0)
    w1 = jnp.sum(jnp.where(sub == e1, aff, 0.0), axis=0, keepdims=True)
    w2 = jnp.sum(jnp.where(sub == e2, aff, 0.0), axis=0, keepdims=True)
    tot = w1 + w2
    return e1, e2, w1 / tot, w2 / tot


def _sort_tile(e1, e2):
    sub = lax.broadcasted_iota(jnp.int32, (N_EXPERTS, TM), 0)
    oh1 = sub == e1
    oh2 = sub == e2
    one = jnp.ones((N_EXPERTS, TM), F32)
    cnt1 = jnp.sum(jnp.where(oh1, one, 0.0), axis=1, keepdims=True) * one
    cnt2 = jnp.sum(jnp.where(oh2, one, 0.0), axis=1, keepdims=True) * one
    tot = (cnt1 + cnt2).astype(jnp.int32)
    padded = (tot + (RUN_ALIGN - 1)) & (-RUN_ALIGN)
    ee = lax.broadcasted_iota(jnp.int32, (N_EXPERTS, N_EXPERTS), 0)
    ef = lax.broadcasted_iota(jnp.int32, (N_EXPERTS, N_EXPERTS), 1)
    start = _dot((ef < ee).astype(BF16), padded.astype(BF16))
    ta = lax.broadcasted_iota(jnp.int32, (TM, TM), 0)
    tb = lax.broadcasted_iota(jnp.int32, (TM, TM), 1)
    before = (ta < tb).astype(BF16)
    rank1 = _dot(oh1.astype(BF16), before)
    rank2 = _dot(oh2.astype(BF16), before)
    pos1 = jnp.sum(jnp.where(oh1, start + rank1, 0.0), axis=0, keepdims=True)
    pos2 = jnp.sum(jnp.where(oh2, start + cnt1 + rank2, 0.0), axis=0, keepdims=True)
    return pos1.astype(jnp.int32), pos2.astype(jnp.int32), start.astype(jnp.int32), padded


def _post_kernel(x_ref, omp_ref, oms_ref, up_ref, uc_ref, un_ref, onp_ref, ons_ref, mod_ref, cw_ref, cb_ref,
                 lg_ref, lb_ref, wo_ref, n2g_ref, wr_ref, br_ref, x1_out, xs_out, pc_out, st_out, pd_out,
                 wo_bf, ext_ref, h2_buf, logit_buf):
    s = pl.program_id(0)
    i = jnp.minimum(s, N_TILES - 1)

    @pl.when(s == 0)
    def _():
        wo_bf[...] = wo_ref[0].astype(BF16)
        h2_buf[...] = jnp.zeros(h2_buf.shape, BF16)
        logit_buf[...] = jnp.zeros(logit_buf.shape, F32)

    pos_in_seq = (i - PROMPT_TILES) % TILES_PER_SAMPLE
    is_first = (i < PROMPT_TILES) | (pos_in_seq == 0)
    is_last = (i < PROMPT_TILES) | (pos_in_seq == TILES_PER_SAMPLE - 1)
    ext_ref[0:CONV_PAD, :] = jnp.where(is_first, 0.0, up_ref[TM - CONV_PAD:TM, :])
    ext_ref[CONV_PAD:CONV_PAD + TM, :] = uc_ref[...]
    ext_ref[CONV_PAD + TM:2 * CONV_PAD + TM, :] = jnp.where(is_last, 0.0, un_ref[0:CONV_PAD, :])
    acc = None
    first_off = CONV_PAD - CONV_K // 2
    for b in range(SUBLANES):
        taps = [j for j in range(CONV_K) if (first_off + j) % SUBLANES == b]
        if not taps:
            continue
        reach = max(first_off + j - b for j in taps)
        shifted = ext_ref[b:b + TM + reach, :]
        for j in taps:
            a8 = first_off + j - b
            term = shifted[a8:a8 + TM, :] * cw_ref[0, j:j + 1, :]
            acc = term if acc is None else acc + term
    conv = acc + cb_ref[0]
    wide = lambda t: jnp.concatenate([t] * (CONV_CH // LANES), axis=1)
    cc = conv - wide(_lane_sums(conv, CONV_CH) * (1.0 / CONV_CH))
    var = _lane_sums(cc * cc, CONV_CH) * (1.0 / CONV_CH)
    o_conv = _silu(cc * wide(lax.rsqrt(var + EPS)) * lg_ref[0] + lb_ref[0])

    is_prompt = i < PROMPT_TILES
    mix = jnp.concatenate([jnp.where(is_prompt, omp_ref[p], oms_ref[p]) for p in range(MLA_HEADS // 2)]
                          + [o_conv.astype(BF16), jnp.where(is_prompt, onp_ref[...], ons_ref[...])],
                          axis=-1)
    att = _dot(mix, wo_bf[...])
    gate1 = mod_ref[0, :, 2 * D:3 * D]
    x1 = x_ref[...] + gate1 * att
    x1_out[...] = x1
    shift2 = mod_ref[0, :, 3 * D:4 * D]
    scale2 = mod_ref[0, :, 4 * D:5 * D]
    h2 = _rms(x1, D) * n2g_ref[0] * (1.0 + scale2) + shift2

    logits = lax.dot_general(wr_ref[...], h2, (((1,), (1,)), ((), ())), preferred_element_type=F32,
                             precision=lax.Precision.HIGHEST)
    h2_prev = h2_buf[(s + 1) % 2]
    logits_prev = logit_buf[(s + 1) % 2]
    h2_buf[s % 2] = h2.astype(BF16)
    logit_buf[s % 2] = logits

    e1, e2, w1, w2 = _route(jax.nn.sigmoid(logits_prev), br_ref[...])
    pos1, pos2, start, padded = _sort_tile(e1, e2)
    st_out[0] = start[:, 0:LANES]
    pd_out[0] = padded[:, 0:LANES]
    r = lax.broadcasted_iota(jnp.int32, (TILE_CAP, TM), 0)
    perm = ((r == pos1) | (r == pos2)).astype(BF16)
    xs_out[...] = _dot(perm, h2_prev).astype(BF16)
    sub = lax.broadcasted_iota(jnp.int32, (LANES, TM), 0)
    packed = jnp.where(sub == 0, pos1.astype(F32),
                       jnp.where(sub == 1, pos2.astype(F32),
                                 jnp.where(sub == 2, w1, jnp.where(sub == 3, w2, 0.0))))
    pc_out[...] = packed.T


def _post(layer, x, om_p, om_s, u, on_p, on_s, mods3, prm):
    cur = lambda s: jnp.minimum(s, N_TILES - 1)
    prev = lambda s: jnp.maximum(s - 1, 0)
    row = lambda s: (cur(s), 0)
    row2 = lambda s: (prev(s), 0)
    lay3 = lambda s: (layer, 0, 0)
    return pl.pallas_call(
        _post_kernel,
        grid=(N_TILES + 1,),
        in_specs=[
            pl.BlockSpec((TM, D), row),
            pl.BlockSpec((MLA_HEADS // 2, TM, LANES), lambda s: (0, jnp.minimum(s, PROMPT_TILES - 1), 0)),
            pl.BlockSpec((MLA_HEADS // 2, TM, LANES), lambda s: (0, jnp.maximum(cur(s) - PROMPT_TILES, 0), 0)),
            pl.BlockSpec((TM, CONV_CH), lambda s: (jnp.maximum(cur(s) - 1, 0), 0)),
            pl.BlockSpec((TM, CONV_CH), row),
            pl.BlockSpec((TM, CONV_CH), lambda s: (jnp.minimum(s + 1, N_TILES - 1), 0)),
            pl.BlockSpec((TM, NA_WIDTH), lambda s: (jnp.minimum(s, PROMPT_TILES - 1), 0)),
            pl.BlockSpec((TM, NA_WIDTH), lambda s: (jnp.maximum(cur(s) - PROMPT_TILES, 0), 0)),
            pl.BlockSpec((1, 1, MOD_W), lambda s: (layer * COND_ROWS + _mod_row(cur(s)), 0, 0)),
            pl.BlockSpec((1, CONV_K, CONV_CH), lay3),
            pl.BlockSpec((1, 1, CONV_CH), lay3),
            pl.BlockSpec((1, 1, CONV_CH), lay3),
            pl.BlockSpec((1, 1, CONV_CH), lay3),
            pl.BlockSpec((1, D, D), lay3),
            pl.BlockSpec((1, 1, D), lay3),
            pl.BlockSpec((N_EXPERTS, D), lambda s: (0, 0)),
            pl.BlockSpec((N_EXPERTS, TM), lambda s: (0, 0)),
        ],
        out_specs=[pl.BlockSpec((TM, D), row), pl.BlockSpec((TILE_CAP, D), row2),
                   pl.BlockSpec((TM, LANES), row2),
                   pl.BlockSpec((1, N_EXPERTS, LANES), lambda s: (prev(s), 0, 0)),
                   pl.BlockSpec((1, N_EXPERTS, LANES), lambda s: (prev(s), 0, 0))],
        out_shape=[jax.ShapeDtypeStruct((T_ALL, D), F32),
                   jax.ShapeDtypeStruct((N_TILES * TILE_CAP, D), BF16),
                   jax.ShapeDtypeStruct((T_ALL, LANES), F32),
                   jax.ShapeDtypeStruct((N_TILES, N_EXPERTS, LANES), jnp.int32),
                   jax.ShapeDtypeStruct((N_TILES, N_EXPERTS, LANES), jnp.int32)],
        scratch_shapes=[pltpu.VMEM((D, D), BF16), pltpu.VMEM((TM + 2 * CONV_PAD, CONV_CH), F32),
                        pltpu.VMEM((2, TM, D), BF16), pltpu.VMEM((2, N_EXPERTS, TM), F32)],
        compiler_params=_cparams(("arbitrary",)),
        name="post_l%d" % layer,
    )(x, om_p, om_s, u, u, u, on_p, on_s, mods3, prm["conv_w"], prm["conv_b"], prm["conv_ln_g"], prm["conv_ln_b"],
      prm["w_out"], prm["norm2_g"], prm["w_router_t"], prm["b_router_t"])


def _for_each_piece(n, fn):
    for sz in RUN_SIZES:
        @pl.when((n & sz) != 0)
        def _():
            fn(n & (-2 * sz), sz)


def _expert_kernel(st_ref, pd_ref, xs_hbm, wg_ref, wu_ref, wd_ref, ys_hbm, xy, wg_bf, wu_bf, wd_bf, zeros_buf,
                   sems):
    e = pl.program_id(0)
    sem_in = (sems.at[0], sems.at[1])
    sem_out = (sems.at[2], sems.at[3])
    sem_tail = sems.at[4]

    def start_half(ex, half, inbound):
        def body(jj, cur):
            j = half * HALF_TILES + jj
            start = st_ref[j * N_EXPERTS + ex]
            n = pd_ref[j * N_EXPERTS + ex]

            def piece(off, sz):
                rows_hbm = pl.ds(pl.multiple_of(j * TILE_CAP + start + off, RUN_ALIGN), sz)
                rows_buf = pl.ds(pl.multiple_of(cur + off, RUN_ALIGN), sz)
                if inbound:
                    pltpu.make_async_copy(xs_hbm.at[rows_hbm], xy.at[half, rows_buf], sem_in[half]).start()
                else:
                    pltpu.make_async_copy(xy.at[half, rows_buf], ys_hbm.at[rows_hbm], sem_out[half]).start()

            _for_each_piece(n, piece)
            return cur + n
        lax.fori_loop(0, HALF_TILES, body, 0)

    def half_rows(ex, half):
        return lax.fori_loop(
            0, HALF_TILES, lambda jj, c: c + pd_ref[(half * HALF_TILES + jj) * N_EXPERTS + ex], 0)

    def wait_half(ex, half, inbound):
        total = half_rows(ex, half)
        for sz in WAIT_SIZES:
            @pl.when((total & sz) != 0)
            def _():
                rows = pl.ds(0, sz)
                if inbound:
                    pltpu.make_async_copy(xs_hbm.at[rows], xy.at[half, rows], sem_in[half]).wait()
                else:
                    pltpu.make_async_copy(xy.at[half, rows], ys_hbm.at[rows], sem_out[half]).wait()
        return total

    def compute_half(half, total):
        def chunk(c, carry):
            rows = pl.ds(pl.multiple_of(c * MOE_CHUNK, MOE_CHUNK), MOE_CHUNK)
            x = xy[half, rows, :]
            hid = _silu(_dot(x, wg_bf[...])) * _dot(x, wu_bf[...])
            xy[half, rows, :] = _dot(hid.astype(BF16), wd_bf[...]).astype(BF16)
            return carry
        lax.fori_loop(0, (total + (MOE_CHUNK - 1)) // MOE_CHUNK, chunk, 0)

    @pl.when(e == 0)
    def _():
        xy[...] = jnp.zeros(xy.shape, BF16)
        zeros_buf[...] = jnp.zeros(zeros_buf.shape, BF16)
        start_half(e, 0, True)

    @pl.when(e > 0)
    def _():
        wait_half(e - 1, 1, False)

    start_half(e, 1, True)
    wg_bf[...] = wg_ref[0, 0].astype(BF16)
    wu_bf[...] = wu_ref[0, 0].astype(BF16)
    wd_bf[...] = wd_ref[0, 0].astype(BF16)
    for half in range(2):
        compute_half(half, wait_half(e, half, True))
        start_half(e, half, False)
    wait_half(e, 0, False)

    @pl.when(e < N_EXPERTS - 1)
    def _():
        start_half(e + 1, 0, True)

    @pl.when(e == N_EXPERTS - 1)
    def _():
        wait_half(e, 1, False)

        def tail_copy(j, off, sz):
            used = st_ref[j * N_EXPERTS + e] + pd_ref[j * N_EXPERTS + e]
            rows_hbm = pl.ds(pl.multiple_of(j * TILE_CAP + used + off, RUN_ALIGN), sz)
            return pltpu.make_async_copy(zeros_buf.at[pl.ds(0, sz)], ys_hbm.at[rows_hbm], sem_tail)

        def tail_len(j):
            return TILE_CAP - (st_ref[j * N_EXPERTS + e] + pd_ref[j * N_EXPERTS + e])

        def t_start(j, c):
            _for_each_piece(tail_len(j), lambda off, sz: tail_copy(j, off, sz).start())
            return c

        def t_wait(j, c):
            _for_each_piece(tail_len(j), lambda off, sz: tail_copy(j, off, sz).wait())
            return c

        lax.fori_loop(0, N_TILES, t_start, 0)
        lax.fori_loop(0, N_TILES, t_wait, 0)


def _experts(layer, starts, pads, xs, prm):
    wspec = lambda shape: pl.BlockSpec((1, 1) + shape, lambda e, st, pd: (layer, e, 0, 0))
    return pl.pallas_call(
        _expert_kernel,
        grid_spec=pltpu.PrefetchScalarGridSpec(
            num_scalar_prefetch=2,
            grid=(N_EXPERTS,),
            in_specs=[pl.BlockSpec(memory_space=pl.ANY), wspec((D, EXPERT_FF)), wspec((D, EXPERT_FF)),
                      wspec((EXPERT_FF, D))],
            out_specs=pl.BlockSpec(memory_space=pl.ANY),
            scratch_shapes=[pltpu.VMEM((2, HALF_ROWS, D), BF16), pltpu.VMEM((D, EXPERT_FF), BF16),
                            pltpu.VMEM((D, EXPERT_FF), BF16), pltpu.VMEM((EXPERT_FF, D), BF16),
                            pltpu.VMEM((RUN_SIZES[0], D), BF16),
                            pltpu.SemaphoreType.DMA((5,))]),
        out_shape=jax.ShapeDtypeStruct((N_TILES * TILE_CAP, D), BF16),
        compiler_params=_cparams(("arbitrary",)),
        name="experts_l%d" % layer,
    )(starts, pads, xs, prm["w_e_gate"], prm["w_e_up"], prm["w_e_down"])


def _final_kernel(x1_ref, ys_ref, pc_ref, mod_ref, op_ref, os_ref):
    x = _moe_combine(x1_ref, ys_ref, pc_ref, mod_ref)
    i = pl.program_id(0)

    @pl.when(i < PROMPT_TILES)
    def _():
        op_ref[...] = x

    @pl.when(i >= PROMPT_TILES)
    def _():
        os_ref[...] = x


def _final(x1, ys, pcol, mods3):
    row = lambda i: (i, 0)
    return pl.pallas_call(
        _final_kernel,
        grid=(N_TILES,),
        in_specs=[pl.BlockSpec((TM, D), row), pl.BlockSpec((TILE_CAP, D), row), pl.BlockSpec((TM, LANES), row),
                  pl.BlockSpec((1, 1, MOD_W), lambda i: ((DEPTH - 1) * COND_ROWS + _mod_row(i), 0, 0))],
        out_specs=[pl.BlockSpec((TM, D), lambda i: (jnp.minimum(i, PROMPT_TILES - 1), 0)),
                   pl.BlockSpec((TM, D), lambda i: (jnp.maximum(i - PROMPT_TILES, 0), 0))],
        out_shape=[jax.ShapeDtypeStruct((T_PROMPT, D), F32), jax.ShapeDtypeStruct((T_SAMPLE, D), F32)],
        compiler_params=_cparams(("arbitrary",)),
        name="final_residual",
    )(x1, ys, pcol, mods3)


def _rope_tables():
    t = np.arange(SAMPLE_LEN)
    n_freq = QK_ROPE // 4
    inv_freq = ROPE_THETA ** (-np.arange(n_freq, dtype=np.float32) / n_freq)
    row = (t // GRID_W).astype(np.float32)
    col = (t % GRID_W).astype(np.float32)
    ang = jnp.concatenate([jnp.asarray(row)[:, None] * inv_freq, jnp.asarray(col)[:, None] * inv_freq],
                          axis=-1)
    cos_p = jnp.repeat(jnp.cos(ang), 2, axis=-1)
    sin_p = jnp.repeat(jnp.sin(ang), 2, axis=-1) * jnp.tile(jnp.asarray([-1.0, 1.0], F32), QK_ROPE // 2)
    pad_l = jnp.ones((SAMPLE_LEN, QK_NOPE), F32)
    pad_r = jnp.ones((SAMPLE_LEN, LANES - QK_HEAD), F32)
    cos = jnp.concatenate([pad_l, cos_p, pad_r], axis=-1)
    sin = jnp.concatenate([0 * pad_l, sin_p, 0 * pad_r], axis=-1)
    ident_c = jnp.ones((TM, LANES), F32)
    return jnp.concatenate([ident_c, cos], axis=0), jnp.concatenate([0 * ident_c, sin], axis=0)


def _pad_lanes(x, lo, total):
    pad = [(0, 0)] * (x.ndim - 1) + [(lo, total - lo - x.shape[-1])]
    return jnp.pad(x, pad)


def kernel(x_prompt, x_sample, cache_mla_ckv, cache_mla_krope, cache_na_k, cache_na_v, c, c_ctx, w_ada, b_ada, norm1_g, norm2_g, w_in, q_a_g, kv_a_g, w_q_up, w_kv_up, mla_q_norm_g, mla_k_norm_g, conv_dw_w, conv_dw_b, conv_ln_g, conv_ln_b, na_q_norm_g, na_k_norm_g, na_rpb, w_out, w_router, b_router, w_e_gate, w_e_up, w_e_down):
    cos, sin = _rope_tables()
    vec = lambda a: a.reshape(DEPTH, 1, a.shape[-1])
    prm = {
        "norm1_g": vec(norm1_g), "norm2_g": vec(norm2_g), "w_in": w_in, "q_a_g": vec(q_a_g),
        "kv_a_g": vec(kv_a_g), "w_q_up": w_q_up, "w_kv_up": w_kv_up,
        "q_norm_g": vec(_pad_lanes(mla_q_norm_g, 0, LANES)), "k_norm_g": vec(_pad_lanes(mla_k_norm_g, 0, LANES)),
        "na_q_g": vec(jnp.tile(na_q_norm_g, (1, 2))), "na_k_g": vec(jnp.tile(na_k_norm_g, (1, 2))),
        "cos": cos, "sin": sin,
        "conv_w": conv_dw_w, "conv_b": vec(conv_dw_b), "conv_ln_g": vec(conv_ln_g), "conv_ln_b": vec(conv_ln_b),
        "w_out": w_out, "w_router_t": w_router.T,
        "b_router_t": jnp.broadcast_to(b_router[:, None], (N_EXPERTS, TM)),
        "w_e_gate": w_e_gate, "w_e_up": w_e_up, "w_e_down": w_e_down,
    }

    cond = jnp.concatenate([c_ctx[None], c, jnp.zeros((COND_ROWS - N_COND, D), F32)], axis=0)
    mods3 = _modulation(cond.T, w_ada, b_ada).reshape(DEPTH * COND_ROWS, 1, MOD_W)

    kc, vc = _ctx_kv(cache_mla_ckv, cache_mla_krope, prm)
    bias = _na_bias(na_rpb)
    cache_k = cache_na_k.reshape(N_SAMPLE, DEPTH, PAST, NA_WIDTH)
    cache_v = cache_na_v.reshape(N_SAMPLE, DEPTH, PAST, NA_WIDTH)

    x = x1 = ys = pcol = None
    new_ckv, new_kr, new_nak, new_nav = [], [], [], []
    for layer in range(DEPTH):
        if layer == 0:
            outs = _pre(layer, False, (x_prompt.reshape(T_PROMPT, D), x_sample.reshape(T_SAMPLE, D)), mods3, prm)
        else:
            outs = _pre(layer, True, (x1, ys, pcol), mods3, prm)
        x, q, k, v, ckv_n, kr, u, nq, nk, nv, nkf, nvf = outs
        om_p, on_p = _prompt_attn(q, k, v, nq, nk, nv)
        om_s = _sample_mla(layer, q, k, v, kc, vc)
        on_s = _sample_na(layer, nq, nk, nv, cache_k, cache_v, bias)
        x1, xs, pcol, run_start, run_len = _post(layer, x, om_p, om_s, u, on_p, on_s, mods3, prm)
        ys = _experts(layer, run_start[:, :, 0].reshape(-1), run_len[:, :, 0].reshape(-1), xs, prm)
        new_ckv.append(ckv_n[:T_PROMPT].reshape(N_PROMPT, PROMPT_LEN, KV_LORA))
        new_kr.append(kr[:T_PROMPT, QK_NOPE:QK_HEAD].reshape(N_PROMPT, PROMPT_LEN, QK_ROPE))
        new_nak.append(nkf[:T_PROMPT].reshape(N_PROMPT, PROMPT_LEN, NA_HEADS, NA_DIM))
        new_nav.append(nvf[:T_PROMPT].reshape(N_PROMPT, PROMPT_LEN, NA_HEADS, NA_DIM))
    y_prompt, y_sample = _final(x1, ys, pcol, mods3)
    return (y_prompt.reshape(N_PROMPT, PROMPT_LEN, D), y_sample.reshape(N_SAMPLE, SAMPLE_LEN, D),
            jnp.stack(new_ckv, axis=1), jnp.stack(new_kr, axis=1), jnp.stack(new_nak, axis=1),
            jnp.stack(new_nav, axis=1))
```

```python
import functools

import numpy as np
import jax
import jax.numpy as jnp
from jax import lax
from jax.experimental import pallas as pl
from jax.experimental.pallas import tpu as pltpu

F32 = jnp.float32
BF16 = jnp.bfloat16

D = 1024
N_PROMPT = 16
PROMPT_LEN = 256
N_SAMPLE = 4
SAMPLE_LEN = 1024
DEPTH = 4
PAST = 512
GRID_W = 64
GRID_ROWS = SAMPLE_LEN // GRID_W
MLA_HEADS = 8
QK_NOPE = 64
QK_ROPE = 32
QK_HEAD = QK_NOPE + QK_ROPE
V_HEAD = 64
Q_LORA = 256
KV_LORA = 128
CONV_CH = 256
CONV_K = 31
NA_HEADS = 4
NA_DIM = 64
NA_WIDTH = NA_HEADS * NA_DIM
NA_WIN_H = 8
NA_WIN_W = 16
N_EXPERTS = 16
N_GROUPS = 4
EPG = 4
EXPERT_FF = 512
ROPE_THETA = 10000.0
EPS = 1e-6
NEG_INF = -1e30
MLA_SCALE = QK_HEAD ** -0.5
NA_SCALE = NA_DIM ** -0.5

LANES = 128
SUBLANES = 8
TM = 256
T_PROMPT = N_PROMPT * PROMPT_LEN
T_SAMPLE = N_SAMPLE * SAMPLE_LEN
T_ALL = T_PROMPT + T_SAMPLE
N_TILES = T_ALL // TM
PROMPT_TILES = T_PROMPT // TM
TILES_PER_SAMPLE = SAMPLE_LEN // TM
N_COND = 1 + N_SAMPLE
COND_ROWS = 8
MOD_W = 6 * D

COL_Q = 0
COL_KV = Q_LORA
COL_KR = COL_KV + KV_LORA
COL_CONV = COL_KR + LANES
COL_NA = COL_CONV + 2 * CONV_CH
IN_COLS_P = COL_NA + 3 * NA_WIDTH
IN_COLS = IN_COLS_P - (LANES - QK_ROPE)

NA_KEY_ROWS = 12
NA_KEYS = NA_KEY_ROWS * GRID_W

VMEM_LIMIT = 56 * 1024 * 1024

RUN_ALIGN = 16
TILE_CAP = 768
RUN_SIZES = (256, 128, 64, 32, 16)
MOE_CHUNK = 256
XY_ROWS = N_TILES * TM
WAIT_SIZES = (8192, 4096, 2048, 1024, 512, 256, 128, 64, 32, 16)
EXPERT_VMEM_LIMIT = 60 * 1024 * 1024


def _cparams(sem):
    return pltpu.CompilerParams(dimension_semantics=sem, vmem_limit_bytes=VMEM_LIMIT)


def _mod_row(i):
    return jnp.where(i < PROMPT_TILES, 0, 1 + (i - PROMPT_TILES) // TILES_PER_SAMPLE)


def _rope_block(i):
    return jnp.where(i < PROMPT_TILES, 0, 1 + (i - PROMPT_TILES) % TILES_PER_SAMPLE)


def _dot(a, b):
    return jnp.dot(a, b, preferred_element_type=F32)


def _dot_t(a, b):
    return lax.dot_general(a, b, (((1,), (1,)), ((), ())), preferred_element_type=F32)


def _silu(x):
    return x * jax.nn.sigmoid(x)


def _lane_is_low(shape):
    return lax.broadcasted_iota(jnp.int32, shape, len(shape) - 1) < (LANES // 2)


MOD_BLK = 768


def _mod_kernel(ct_ref, w_ref, b_ref, o_ref, cb_ref):
    first = (pl.program_id(0) == 0) & (pl.program_id(1) == 0)

    @pl.when(first)
    def _():
        s = _silu(ct_ref[...])
        for m in range(N_COND):
            cb_ref[m] = jnp.broadcast_to(s[:, m:m + 1], (D, LANES))

    for j in range(MOD_BLK // LANES):
        w = w_ref[0, :, j * LANES:(j + 1) * LANES]
        rows = [jnp.sum(w * cb_ref[m], axis=0, keepdims=True) for m in range(N_COND)]
        rows.append(jnp.zeros((COND_ROWS - N_COND, LANES), F32))
        o_ref[0, :, j * LANES:(j + 1) * LANES] = (
            jnp.concatenate(rows, axis=0) + b_ref[0, :, j * LANES:(j + 1) * LANES])


def _modulation(cond_t, w_ada, b_ada):
    nb = MOD_W // MOD_BLK
    return pl.pallas_call(
        _mod_kernel,
        grid=(DEPTH, nb),
        in_specs=[
            pl.BlockSpec((D, COND_ROWS), lambda l, j: (0, 0)),
            pl.BlockSpec((1, D, MOD_BLK), lambda l, j: (l, 0, j)),
            pl.BlockSpec((1, 1, MOD_BLK), lambda l, j: (l, 0, j)),
        ],
        out_specs=pl.BlockSpec((1, COND_ROWS, MOD_BLK), lambda l, j: (l, 0, j)),
        out_shape=jax.ShapeDtypeStruct((DEPTH, COND_ROWS, MOD_W), F32),
        scratch_shapes=[pltpu.VMEM((N_COND, D, LANES), F32)],
        compiler_params=_cparams(("arbitrary", "arbitrary")),
        name="modulation",
    )(cond_t, w_ada, b_ada.reshape(DEPTH, 1, MOD_W))


def _lane_sums(sq, group):
    k = sq.shape[1]
    hi = sq.astype(BF16)
    lo = (sq - hi.astype(F32)).astype(BF16)
    if group == k:
        sel = jnp.ones((k, LANES), BF16)
    else:
        r = lax.broadcasted_iota(jnp.int32, (k, LANES), 0)
        c = lax.broadcasted_iota(jnp.int32, (k, LANES), 1)
        sel = ((r // group) == (c // group)).astype(BF16)
    return _dot(hi, sel) + _dot(lo, sel)


def _rms(x, width):
    rinv = lax.rsqrt(_lane_sums(x * x, x.shape[1]) * (1.0 / width) + EPS)
    reps = x.shape[1] // LANES
    return x * (rinv if reps == 1 else jnp.concatenate([rinv] * reps, axis=1))


def _rope(x, cos, sin):
    lane = lax.broadcasted_iota(jnp.int32, x.shape, 1)
    partner = jnp.where((lane & 1) == 0, pltpu.roll(x, LANES - 1, 1), pltpu.roll(x, 1, 1))
    return x * cos + partner * sin


def _pair_rms(x, gain):
    ms = _lane_sums(x * x, NA_DIM) * (1.0 / NA_DIM)
    return x * lax.rsqrt(ms + EPS) * gain


def _softmax_pv(scores, values):
    m = scores[0].max(axis=-1, keepdims=True)
    for s in scores[1:]:
        m = jnp.maximum(m, s.max(axis=-1, keepdims=True))
    acc = None
    den = None
    for s, v in zip(scores, values):
        p = jnp.exp(s - m)
        d = jnp.sum(p, axis=-1, keepdims=True)
        o = _dot(p.astype(BF16), v)
        acc = o if acc is None else acc + o
        den = d if den is None else den + d
    return acc * (1.0 / den)


def _moe_combine(x1_ref, ys_ref, pc_ref, modp_ref):
    pc = pc_ref[...]
    lane = lax.broadcasted_iota(jnp.int32, (TM, TILE_CAP), 1)
    ys = ys_ref[...]
    y1 = _dot((lane == pc[:, 0:1].astype(jnp.int32)).astype(BF16), ys)
    y2 = _dot((lane == pc[:, 1:2].astype(jnp.int32)).astype(BF16), ys)
    gate2 = modp_ref[0, :, 5 * D:6 * D]
    return x1_ref[...] + gate2 * (pc[:, 2:3] * y1 + pc[:, 3:4] * y2)


def _place_w_in(w_ref, w_bf):
    off_kv = Q_LORA + KV_LORA
    off_kr = off_kv + QK_ROPE
    w_bf[:, 0:off_kv] = w_ref[0, :, 0:off_kv].astype(BF16)
    w_bf[:, COL_KR:COL_KR + LANES] = jnp.zeros((D, LANES), BF16)
    w_bf[:, COL_KR + QK_NOPE:COL_KR + QK_HEAD] = w_ref[0, :, off_kv:off_kr].astype(BF16)
    w_bf[:, COL_CONV:IN_COLS_P] = w_ref[0, :, off_kr:off_kr + (IN_COLS_P - COL_CONV)].astype(BF16)


def _place_w_q(w_ref, w_bf):
    w_bf[...] = jnp.zeros(w_bf.shape, BF16)
    for hd in range(MLA_HEADS):
        w_bf[:, hd * LANES:hd * LANES + QK_HEAD] = w_ref[0, :, hd * QK_HEAD:(hd + 1) * QK_HEAD].astype(BF16)


def _place_w_kv(w_ref, wk_bf, wv_bf):
    wk_bf[...] = jnp.zeros(wk_bf.shape, BF16)
    for hd in range(MLA_HEADS):
        c0 = hd * (QK_NOPE + V_HEAD)
        wk_bf[:, hd * LANES:hd * LANES + QK_NOPE] = w_ref[0, :, c0:c0 + QK_NOPE].astype(BF16)
        wv_bf[:, hd * V_HEAD:(hd + 1) * V_HEAD] = w_ref[0, :, c0 + QK_NOPE:c0 + QK_NOPE + V_HEAD].astype(BF16)


def _pre_kernel(has_moe, *refs):
    if has_moe:
        x1_ref, ys_ref, pc_ref, modp_ref = refs[:4]
        refs = refs[4:]
    else:
        xp_ref, xs_ref = refs[:2]
        refs = refs[2:]
    (mod_ref, n1g_ref, win_ref, qag_ref, kvag_ref, wq_ref, wkv_ref, qng_ref, kng_ref,
     naqg_ref, nakg_ref, cos_ref, sin_ref) = refs[:13]
    refs = refs[13:]
    xo_ref = refs[0]
    refs = refs[1:]
    (q_out, k_out, v_out, ckv_out, kr_out, u_out, naq_out, nak_out, nav_out, nakf_out, navf_out,
     win_bf, wq_bf, wk_bf, wv_bf, proj_buf) = refs

    s = pl.program_id(0)
    i = jnp.minimum(s, N_TILES - 1)

    @pl.when(s == 0)
    def _():
        _place_w_in(win_ref, win_bf)
        _place_w_q(wq_ref, wq_bf)
        _place_w_kv(wkv_ref, wk_bf, wv_bf)
        proj_buf[...] = jnp.zeros(proj_buf.shape, F32)

    if has_moe:
        x = _moe_combine(x1_ref, ys_ref, pc_ref, modp_ref)
    else:
        x = jnp.where(i < PROMPT_TILES, xp_ref[...], xs_ref[...])
    xo_ref[...] = x
    shift = mod_ref[0, :, 0:D]
    scale = mod_ref[0, :, D:2 * D]
    h = _rms(x, D) * n1g_ref[0] * (1.0 + scale) + shift
    proj = proj_buf[(s + 1) % 2]
    proj_buf[s % 2] = _dot(h.astype(BF16), win_bf[...])

    cos = cos_ref[...]
    sin = sin_ref[...]

    q_lat = _rms(proj[:, COL_Q:COL_Q + Q_LORA], Q_LORA) * qag_ref[0]
    q = _dot(q_lat.astype(BF16), wq_bf[...])
    for hd in range(MLA_HEADS):
        qh = q[:, hd * LANES:(hd + 1) * LANES]
        qh = _rope(_rms(qh, QK_HEAD) * qng_ref[0], cos, sin)
        q_out[hd] = (qh * MLA_SCALE).astype(BF16)

    ckv_n = _rms(proj[:, COL_KV:COL_KV + KV_LORA], KV_LORA) * kvag_ref[0]
    ckv_out[...] = ckv_n
    kr = proj[:, COL_KR:COL_KR + LANES]
    kr_out[...] = kr
    ckv_b = ckv_n.astype(BF16)
    k_nope = _dot(ckv_b, wk_bf[...])
    kr_rot = _rope(kr * kng_ref[0], cos, sin)
    kr_sq = kr * kr
    for hd in range(MLA_HEADS):
        kn = k_nope[:, hd * LANES:(hd + 1) * LANES]
        rinv = lax.rsqrt(_lane_sums(kn * kn + kr_sq, LANES) * (1.0 / QK_HEAD) + EPS)
        k_out[hd] = ((kn * kng_ref[0] + kr_rot) * rinv).astype(BF16)
    v = _dot(ckv_b, wv_bf[...])
    for p in range(MLA_HEADS // 2):
        v_out[p] = v[:, p * LANES:(p + 1) * LANES].astype(BF16)

    a = proj[:, COL_CONV:COL_CONV + CONV_CH]
    g = proj[:, COL_CONV + CONV_CH:COL_CONV + 2 * CONV_CH]
    u_out[...] = a * jax.nn.sigmoid(g)

    for p in range(NA_HEADS // 2):
        sl = slice(p * LANES, (p + 1) * LANES)
        qn = _pair_rms(proj[:, COL_NA + p * LANES:COL_NA + (p + 1) * LANES], naqg_ref[0])
        kn = _pair_rms(proj[:, COL_NA + NA_WIDTH + p * LANES:COL_NA + NA_WIDTH + (p + 1) * LANES],
                       nakg_ref[0])
        vn = proj[:, COL_NA + 2 * NA_WIDTH + p * LANES:COL_NA + 2 * NA_WIDTH + (p + 1) * LANES]
        naq_out[:, sl] = (qn * NA_SCALE).astype(BF16)
        nak_out[:, sl] = kn.astype(BF16)
        nav_out[:, sl] = vn.astype(BF16)
        nakf_out[:, sl] = kn
        navf_out[:, sl] = vn


def _pre(layer, has_moe, x_or_parts, mods3, prm):
    cur = lambda s: jnp.minimum(s, N_TILES - 1)
    prev = lambda s: jnp.maximum(s - 1, 0)
    row = lambda s: (cur(s), 0)
    row2 = lambda s: (prev(s), 0)
    lay3 = lambda s: (layer, 0, 0)
    in_specs = []
    args = []
    if has_moe:
        x1, ys, pcol = x_or_parts
        in_specs += [pl.BlockSpec((TM, D), row), pl.BlockSpec((TILE_CAP, D), row),
                     pl.BlockSpec((TM, LANES), row),
                     pl.BlockSpec((1, 1, MOD_W), lambda s: ((layer - 1) * COND_ROWS + _mod_row(cur(s)), 0, 0))]
        args += [x1, ys, pcol, mods3]
    else:
        x_prompt, x_sample = x_or_parts
        in_specs += [pl.BlockSpec((TM, D), lambda s: (jnp.minimum(s, PROMPT_TILES - 1), 0)),
                     pl.BlockSpec((TM, D), lambda s: (jnp.maximum(cur(s) - PROMPT_TILES, 0), 0))]
        args += [x_prompt, x_sample]
    in_specs += [
        pl.BlockSpec((1, 1, MOD_W), lambda s: (layer * COND_ROWS + _mod_row(cur(s)), 0, 0)),
        pl.BlockSpec((1, 1, D), lay3),
        pl.BlockSpec((1, D, IN_COLS), lay3),
        pl.BlockSpec((1, 1, Q_LORA), lay3),
        pl.BlockSpec((1, 1, KV_LORA), lay3),
        pl.BlockSpec((1, Q_LORA, MLA_HEADS * QK_HEAD), lay3),
        pl.BlockSpec((1, KV_LORA, MLA_HEADS * (QK_NOPE + V_HEAD)), lay3),
        pl.BlockSpec((1, 1, LANES), lay3),
        pl.BlockSpec((1, 1, LANES), lay3),
        pl.BlockSpec((1, 1, LANES), lay3),
        pl.BlockSpec((1, 1, LANES), lay3),
        pl.BlockSpec((TM, LANES), lambda s: (_rope_block(prev(s)), 0)),
        pl.BlockSpec((TM, LANES), lambda s: (_rope_block(prev(s)), 0)),
    ]
    args += [mods3, prm["norm1_g"], prm["w_in"], prm["q_a_g"], prm["kv_a_g"], prm["w_q_up"], prm["w_kv_up"],
             prm["q_norm_g"], prm["k_norm_g"], prm["na_q_g"], prm["na_k_g"], prm["cos"], prm["sin"]]
    out_specs = [pl.BlockSpec((TM, D), row)]
    out_shape = [jax.ShapeDtypeStruct((T_ALL, D), F32)]
    head3 = lambda n: pl.BlockSpec((n, TM, LANES), lambda s: (0, prev(s), 0))
    out_specs += [head3(MLA_HEADS), head3(MLA_HEADS), head3(MLA_HEADS // 2),
                  pl.BlockSpec((TM, KV_LORA), row2), pl.BlockSpec((TM, LANES), row2),
                  pl.BlockSpec((TM, CONV_CH), row2)] + [pl.BlockSpec((TM, NA_WIDTH), row2)] * 5
    out_shape += [
        jax.ShapeDtypeStruct((MLA_HEADS, T_ALL, LANES), BF16),
        jax.ShapeDtypeStruct((MLA_HEADS, T_ALL, LANES), BF16),
        jax.ShapeDtypeStruct((MLA_HEADS // 2, T_ALL, LANES), BF16),
        jax.ShapeDtypeStruct((T_ALL, KV_LORA), F32),
        jax.ShapeDtypeStruct((T_ALL, LANES), F32),
        jax.ShapeDtypeStruct((T_ALL, CONV_CH), F32),
        jax.ShapeDtypeStruct((T_ALL, NA_WIDTH), BF16),
        jax.ShapeDtypeStruct((T_ALL, NA_WIDTH), BF16),
        jax.ShapeDtypeStruct((T_ALL, NA_WIDTH), BF16),
        jax.ShapeDtypeStruct((T_ALL, NA_WIDTH), F32),
        jax.ShapeDtypeStruct((T_ALL, NA_WIDTH), F32),
    ]
    return pl.pallas_call(
        functools.partial(_pre_kernel, has_moe),
        grid=(N_TILES + 1,),
        in_specs=in_specs,
        out_specs=out_specs,
        out_shape=out_shape,
        scratch_shapes=[pltpu.VMEM((D, IN_COLS_P), BF16), pltpu.VMEM((Q_LORA, MLA_HEADS * LANES), BF16),
                        pltpu.VMEM((KV_LORA, MLA_HEADS * LANES), BF16),
                        pltpu.VMEM((KV_LORA, MLA_HEADS * V_HEAD), BF16),
                        pltpu.VMEM((2, TM, IN_COLS_P), F32)],
        compiler_params=_cparams(("arbitrary",)),
        name="pre_l%d" % layer,
    )(*args)


def _ctx_kernel(ckv_ref, kr_ref, wkv_ref, kng_ref, kc_out, vc_out, wk_bf, wv_bf):
    @pl.when(pl.program_id(1) == 0)
    def _():
        _place_w_kv(wkv_ref, wk_bf, wv_bf)

    ckv = ckv_ref[0, 0].astype(BF16)
    kr = jnp.concatenate([jnp.zeros((PAST, QK_NOPE), F32), kr_ref[0, 0],
                          jnp.zeros((PAST, LANES - QK_HEAD), F32)], axis=1)
    k_nope = _dot(ckv, wk_bf[...])
    for hd in range(MLA_HEADS):
        kh = k_nope[:, hd * LANES:(hd + 1) * LANES] + kr
        kc_out[0, 0, hd] = (_rms(kh, QK_HEAD) * kng_ref[0]).astype(BF16)
    v = _dot(ckv, wv_bf[...])
    for p in range(MLA_HEADS // 2):
        vc_out[0, 0, p] = v[:, p * LANES:(p + 1) * LANES].astype(BF16)


def _ctx_kv(cache_ckv, cache_krope, prm):
    return pl.pallas_call(
        _ctx_kernel,
        grid=(DEPTH, N_SAMPLE),
        in_specs=[
            pl.BlockSpec((1, 1, PAST, KV_LORA), lambda l, b: (b, l, 0, 0)),
            pl.BlockSpec((1, 1, PAST, QK_ROPE), lambda l, b: (b, l, 0, 0)),
            pl.BlockSpec((1, KV_LORA, MLA_HEADS * (QK_NOPE + V_HEAD)), lambda l, b: (l, 0, 0)),
            pl.BlockSpec((1, 1, LANES), lambda l, b: (l, 0, 0)),
        ],
        out_specs=[
            pl.BlockSpec((1, 1, MLA_HEADS, PAST, LANES), lambda l, b: (l, b, 0, 0, 0)),
            pl.BlockSpec((1, 1, MLA_HEADS // 2, PAST, LANES), lambda l, b: (l, b, 0, 0, 0)),
        ],
        out_shape=[
            jax.ShapeDtypeStruct((DEPTH, N_SAMPLE, MLA_HEADS, PAST, LANES), BF16),
            jax.ShapeDtypeStruct((DEPTH, N_SAMPLE, MLA_HEADS // 2, PAST, LANES), BF16),
        ],
        scratch_shapes=[pltpu.VMEM((KV_LORA, MLA_HEADS * LANES), BF16),
                        pltpu.VMEM((KV_LORA, MLA_HEADS * V_HEAD), BF16)],
        compiler_params=_cparams(("arbitrary", "arbitrary")),
        name="ctx_kv",
    )(cache_ckv, cache_krope, prm["w_kv_up"], prm["k_norm_g"])


N_DR = 2 * NA_WIN_H - 1
N_DC = 2 * NA_WIN_W - 1


def _bias_kernel(rpb_ref, o_ref, blk_ref):
    cq = lax.broadcasted_iota(jnp.int32, (GRID_W, GRID_W), 0)
    ck = lax.broadcasted_iota(jnp.int32, (GRID_W, GRID_W), 1)
    dc = jnp.clip(ck - cq + (NA_WIN_W - 1), 0, N_DC - 1)
    c0 = jnp.clip(cq - NA_WIN_W // 2, 0, GRID_W - NA_WIN_W)
    in_win = (ck >= c0) & (ck < c0 + NA_WIN_W)
    for dr in range(N_DR):
        acc = jnp.zeros((GRID_W, GRID_W), F32)
        for j in range(N_DC):
            acc = jnp.where(dc == j, rpb_ref[pl.program_id(0), pl.program_id(1), dr, j], acc)
        blk_ref[dr] = jnp.where(in_win, acc, NEG_INF)
    blk_ref[N_DR] = jnp.full((GRID_W, GRID_W), NEG_INF, F32)
    idx = _na_block_index()
    for i in range(TILES_PER_SAMPLE):
        for rq in range(TM // GRID_W):
            for pp in range(NA_KEY_ROWS // 2):
                pair = jnp.concatenate([blk_ref[int(idx[i, rq, 2 * pp])], blk_ref[int(idx[i, rq, 2 * pp + 1])]],
                                       axis=1)
                o_ref[0, 0, i, rq * GRID_W:(rq + 1) * GRID_W, pp * LANES:(pp + 1) * LANES] = pair


def _na_bias(na_rpb):
    return pl.pallas_call(
        _bias_kernel,
        grid=(DEPTH, NA_HEADS),
        in_specs=[pl.BlockSpec(memory_space=pltpu.SMEM)],
        out_specs=pl.BlockSpec((1, 1, TILES_PER_SAMPLE, TM, NA_KEYS), lambda l, h: (l, h, 0, 0, 0)),
        out_shape=jax.ShapeDtypeStruct((DEPTH, NA_HEADS, TILES_PER_SAMPLE, TM, NA_KEYS), F32),
        scratch_shapes=[pltpu.VMEM((N_DR + 1, GRID_W, GRID_W), F32)],
        compiler_params=_cparams(("arbitrary", "arbitrary")),
        name="na_bias",
    )(na_rpb)


def _na_block_index():
    idx = np.full((TILES_PER_SAMPLE, TM // GRID_W, NA_KEY_ROWS), N_DR, np.int32)
    for i in range(TILES_PER_SAMPLE):
        ks = _na_key_start_row(i)
        for rq in range(TM // GRID_W):
            r = i * (TM // GRID_W) + rq
            r0 = min(max(r - NA_WIN_H // 2, 0), GRID_ROWS - NA_WIN_H)
            for rk in range(NA_KEY_ROWS):
                kr = ks + rk
                if r0 <= kr < r0 + NA_WIN_H:
                    idx[i, rq, rk] = kr - r + (NA_WIN_H - 1)
    return idx


def _na_key_start_row(i):
    return 0 if i < TILES_PER_SAMPLE // 2 else GRID_ROWS - NA_KEY_ROWS


def _prompt_attn_kernel(q_ref, k_ref, v_ref, nq_ref, nk_ref, nv_ref, om_ref, on_ref):
    low = _lane_is_low((PROMPT_LEN, LANES))
    for p in range(MLA_HEADS // 2):
        outs = []
        for hh in range(2):
            hd = 2 * p + hh
            s = _dot_t(q_ref[hd], k_ref[hd])
            outs.append(_softmax_pv([s], [v_ref[p]]))
        om_ref[p] = jnp.where(low, outs[0], outs[1]).astype(BF16)
    for p in range(NA_HEADS // 2):
        sl = slice(p * LANES, (p + 1) * LANES)
        qp = nq_ref[:, sl]
        kp = nk_ref[:, sl]
        vp = nv_ref[:, sl]
        outs = []
        for hh in range(2):
            qm = jnp.where(low if hh == 0 else ~low, qp, jnp.zeros_like(qp))
            outs.append(_softmax_pv([_dot_t(qm, kp)], [vp]))
        on_ref[:, sl] = jnp.where(low, outs[0], outs[1]).astype(BF16)


def _prompt_attn(q, k, v, nq, nk, nv):
    head3 = lambda n: pl.BlockSpec((n, PROMPT_LEN, LANES), lambda b: (0, b, 0))
    row = pl.BlockSpec((PROMPT_LEN, NA_WIDTH), lambda b: (b, 0))
    return pl.pallas_call(
        _prompt_attn_kernel,
        grid=(N_PROMPT,),
        in_specs=[head3(MLA_HEADS), head3(MLA_HEADS), head3(MLA_HEADS // 2), row, row, row],
        out_specs=[head3(MLA_HEADS // 2), row],
        out_shape=[jax.ShapeDtypeStruct((MLA_HEADS // 2, T_PROMPT, LANES), BF16),
                   jax.ShapeDtypeStruct((T_PROMPT, NA_WIDTH), BF16)],
        compiler_params=_cparams(("arbitrary",)),
        name="prompt_attn",
    )(q, k, v, nq, nk, nv)


def _sample_mla_kernel(q_ref, k_ref, v_ref, kc_ref, vc_ref, o_ref):
    low = _lane_is_low((TM, LANES))
    for t in range(TILES_PER_SAMPLE):
        rows = slice(t * TM, (t + 1) * TM)
        outs = []
        for hh in range(2):
            q = q_ref[hh, rows, :]
            s_lat = _dot_t(q, k_ref[hh])
            s_ctx = _dot_t(q, kc_ref[0, 0, hh])
            outs.append(_softmax_pv([s_lat, s_ctx], [v_ref[0], vc_ref[0, 0, 0]]))
        o_ref[0, rows, :] = jnp.where(low, outs[0], outs[1]).astype(BF16)


def _sample_mla(layer, q, k, v, kc, vc):
    seq_blk0 = T_PROMPT // SAMPLE_LEN
    return pl.pallas_call(
        _sample_mla_kernel,
        grid=(N_SAMPLE, MLA_HEADS // 2),
        in_specs=[
            pl.BlockSpec((2, SAMPLE_LEN, LANES), lambda b, p: (p, seq_blk0 + b, 0)),
            pl.BlockSpec((2, SAMPLE_LEN, LANES), lambda b, p: (p, seq_blk0 + b, 0)),
            pl.BlockSpec((1, SAMPLE_LEN, LANES), lambda b, p: (p, seq_blk0 + b, 0)),
            pl.BlockSpec((1, 1, 2, PAST, LANES), lambda b, p: (layer, b, p, 0, 0)),
            pl.BlockSpec((1, 1, 1, PAST, LANES), lambda b, p: (layer, b, p, 0, 0)),
        ],
        out_specs=pl.BlockSpec((1, SAMPLE_LEN, LANES), lambda b, p: (p, b, 0)),
        out_shape=jax.ShapeDtypeStruct((MLA_HEADS // 2, T_SAMPLE, LANES), BF16),
        compiler_params=_cparams(("arbitrary", "arbitrary")),
        name="sample_mla",
    )(q, k, v, kc, vc)


def _sample_na_kernel(q_ref, k_ref, v_ref, kc_ref, vc_ref, b_ref, o_ref):
    kc = kc_ref[0, 0].astype(BF16)
    vc = vc_ref[0, 0].astype(BF16)
    low = _lane_is_low((TM, LANES))
    for t in range(TILES_PER_SAMPLE):
        rows = slice(t * TM, (t + 1) * TM)
        start = _na_key_start_row(t) * GRID_W
        k_win = k_ref[start:start + NA_KEYS, :]
        v_win = v_ref[start:start + NA_KEYS, :]
        q = q_ref[rows, :]
        outs = []
        for hh in range(2):
            qm = jnp.where(low if hh == 0 else ~low, q, jnp.zeros_like(q))
            s_loc = _dot_t(qm, k_win) + b_ref[0, hh, t]
            s_ctx = _dot_t(qm, kc)
            outs.append(_softmax_pv([s_loc, s_ctx], [v_win, vc]))
        o_ref[rows, :] = jnp.where(low, outs[0], outs[1]).astype(BF16)


def _sample_na(layer, nq, nk, nv, cache_k, cache_v, bias):
    seq_blk0 = T_PROMPT // SAMPLE_LEN
    return pl.pallas_call(
        _sample_na_kernel,
        grid=(NA_HEADS // 2, N_SAMPLE),
        in_specs=[
            pl.BlockSpec((SAMPLE_LEN, LANES), lambda p, b: (seq_blk0 + b, p)),
            pl.BlockSpec((SAMPLE_LEN, LANES), lambda p, b: (seq_blk0 + b, p)),
            pl.BlockSpec((SAMPLE_LEN, LANES), lambda p, b: (seq_blk0 + b, p)),
            pl.BlockSpec((1, 1, PAST, LANES), lambda p, b: (b, layer, 0, p)),
            pl.BlockSpec((1, 1, PAST, LANES), lambda p, b: (b, layer, 0, p)),
            pl.BlockSpec((1, 2, TILES_PER_SAMPLE, TM, NA_KEYS), lambda p, b: (layer, p, 0, 0, 0)),
        ],
        out_specs=pl.BlockSpec((SAMPLE_LEN, LANES), lambda p, b: (b, p)),
        out_shape=jax.ShapeDtypeStruct((T_SAMPLE, NA_WIDTH), BF16),
        compiler_params=_cparams(("arbitrary", "arbitrary")),
        name="sample_na",
    )(nq, nk, nv, cache_k, cache_v, bias)


CONV_PAD = 16


def _argmax_first(cols):
    best = cols[0]
    idx = jnp.zeros(best.shape, jnp.int32)
    for j in range(1, len(cols)):
        upd = cols[j] > best
        idx = jnp.where(upd, j, idx)
        best = jnp.where(upd, cols[j], best)
    return idx


def _route(aff, bias):
    sel = aff + bias
    rows = [sel[e:e + 1, :] for e in range(N_EXPERTS)]
    group_scores = []
    for g in range(N_GROUPS):
        c = rows[g * EPG:(g + 1) * EPG]
        best = None
        for a in range(EPG):
            for b in range(a + 1, EPG):
                s = c[a] + c[b]
                best = s if best is None else jnp.maximum(best, s)
        group_scores.append(best)
    g_idx = _argmax_first(group_scores)
    in_group = []
    for j in range(EPG):
        v = rows[j]
        for g in range(1, N_GROUPS):
            v = jnp.where(g_idx == g, rows[g * EPG + j], v)
        in_group.append(v)
    i1 = _argmax_first(in_group)
    masked = [jnp.where(i1 == j, -jnp.inf, in_group[j]) for j in range(EPG)]
    i2 = _argmax_first(masked)
    e1 = g_idx * EPG + i1
    e2 = g_idx * EPG + i2
    sub = lax.broadcasted_iota(jnp.int32, aff.shape, 0)
    w1 = jnp.sum(jnp.where(sub == e1, aff, 0.0), axis=0, keepdims=True)
    w2 = jnp.sum(jnp.where(sub == e2, aff, 0.0), axis=0, keepdims=True)
    tot = w1 + w2
    return e1, e2, w1 / tot, w2 / tot


def _sort_tile(e1, e2):
    sub = lax.broadcasted_iota(jnp.int32, (N_EXPERTS, TM), 0)
    oh1 = sub == e1
    oh2 = sub == e2
    one = jnp.ones((N_EXPERTS, TM), F32)
    cnt1 = jnp.sum(jnp.where(oh1, one, 0.0), axis=1, keepdims=True) * one
    cnt2 = jnp.sum(jnp.where(oh2, one, 0.0), axis=1, keepdims=True) * one
    tot = (cnt1 + cnt2).astype(jnp.int32)
    padded = (tot + (RUN_ALIGN - 1)) & (-RUN_ALIGN)
    ee = lax.broadcasted_iota(jnp.int32, (N_EXPERTS, N_EXPERTS), 0)
    ef = lax.broadcasted_iota(jnp.int32, (N_EXPERTS, N_EXPERTS), 1)
    start = _dot((ef < ee).astype(BF16), padded.astype(BF16))
    ta = lax.broadcasted_iota(jnp.int32, (TM, TM), 0)
    tb = lax.broadcasted_iota(jnp.int32, (TM, TM), 1)
    before = (ta < tb).astype(BF16)
    rank1 = _dot(oh1.astype(BF16), before)
    rank2 = _dot(oh2.astype(BF16), before)
    pos1 = jnp.sum(jnp.where(oh1, start + rank1, 0.0), axis=0, keepdims=True)
    pos2 = jnp.sum(jnp.where(oh2, start + cnt1 + rank2, 0.0), axis=0, keepdims=True)
    return pos1.astype(jnp.int32), pos2.astype(jnp.int32), start.astype(jnp.int32), padded


def _post_kernel(x_ref, omp_ref, oms_ref, up_ref, uc_ref, un_ref, onp_ref, ons_ref, mod_ref, cw_ref, cb_ref,
                 lg_ref, lb_ref, wo_ref, n2g_ref, wr_ref, br_ref, x1_out, xs_out, pc_out, st_out, pd_out,
                 wo_bf, ext_ref, h2_buf, logit_buf):
    s = pl.program_id(0)
    i = jnp.minimum(s, N_TILES - 1)

    @pl.when(s == 0)
    def _():
        wo_bf[...] = wo_ref[0].astype(BF16)
        h2_buf[...] = jnp.zeros(h2_buf.shape, BF16)
        logit_buf[...] = jnp.zeros(logit_buf.shape, F32)

    pos_in_seq = (i - PROMPT_TILES) % TILES_PER_SAMPLE
    is_first = (i < PROMPT_TILES) | (pos_in_seq == 0)
    is_last = (i < PROMPT_TILES) | (pos_in_seq == TILES_PER_SAMPLE - 1)
    ext_ref[0:CONV_PAD, :] = jnp.where(is_first, 0.0, up_ref[TM - CONV_PAD:TM, :])
    ext_ref[CONV_PAD:CONV_PAD + TM, :] = uc_ref[...]
    ext_ref[CONV_PAD + TM:2 * CONV_PAD + TM, :] = jnp.where(is_last, 0.0, un_ref[0:CONV_PAD, :])
    acc = None
    first_off = CONV_PAD - CONV_K // 2
    for b in range(SUBLANES):
        taps = [j for j in range(CONV_K) if (first_off + j) % SUBLANES == b]
        if not taps:
            continue
        reach = max(first_off + j - b for j in taps)
        shifted = ext_ref[b:b + TM + reach, :]
        for j in taps:
            a8 = first_off + j - b
            term = shifted[a8:a8 + TM, :] * cw_ref[0, j:j + 1, :]
            acc = term if acc is None else acc + term
    conv = acc + cb_ref[0]
    wide = lambda t: jnp.concatenate([t] * (CONV_CH // LANES), axis=1)
    cc = conv - wide(_lane_sums(conv, CONV_CH) * (1.0 / CONV_CH))
    var = _lane_sums(cc * cc, CONV_CH) * (1.0 / CONV_CH)
    o_conv = _silu(cc * wide(lax.rsqrt(var + EPS)) * lg_ref[0] + lb_ref[0])

    is_prompt = i < PROMPT_TILES
    mix = jnp.concatenate([jnp.where(is_prompt, omp_ref[p], oms_ref[p]) for p in range(MLA_HEADS // 2)]
                          + [o_conv.astype(BF16), jnp.where(is_prompt, onp_ref[...], ons_ref[...])],
                          axis=-1)
    att = _dot(mix, wo_bf[...])
    gate1 = mod_ref[0, :, 2 * D:3 * D]
    x1 = x_ref[...] + gate1 * att
    x1_out[...] = x1
    shift2 = mod_ref[0, :, 3 * D:4 * D]
    scale2 = mod_ref[0, :, 4 * D:5 * D]
    h2 = _rms(x1, D) * n2g_ref[0] * (1.0 + scale2) + shift2

    logits = lax.dot_general(wr_ref[...], h2, (((1,), (1,)), ((), ())), preferred_element_type=F32,
                             precision=lax.Precision.HIGHEST)
    h2_prev = h2_buf[(s + 1) % 2]
    logits_prev = logit_buf[(s + 1) % 2]
    h2_buf[s % 2] = h2.astype(BF16)
    logit_buf[s % 2] = logits

    e1, e2, w1, w2 = _route(jax.nn.sigmoid(logits_prev), br_ref[...])
    pos1, pos2, start, padded = _sort_tile(e1, e2)
    st_out[0] = start[:, 0:LANES]
    pd_out[0] = padded[:, 0:LANES]
    r = lax.broadcasted_iota(jnp.int32, (TILE_CAP, TM), 0)
    perm = ((r == pos1) | (r == pos2)).astype(BF16)
    xs_out[...] = _dot(perm, h2_prev).astype(BF16)
    sub = lax.broadcasted_iota(jnp.int32, (LANES, TM), 0)
    packed = jnp.where(sub == 0, pos1.astype(F32),
                       jnp.where(sub == 1, pos2.astype(F32),
                                 jnp.where(sub == 2, w1, jnp.where(sub == 3, w2, 0.0))))
    pc_out[...] = packed.T


def _post(layer, x, om_p, om_s, u, on_p, on_s, mods3, prm):
    cur = lambda s: jnp.minimum(s, N_TILES - 1)
    prev = lambda s: jnp.maximum(s - 1, 0)
    row = lambda s: (cur(s), 0)
    row2 = lambda s: (prev(s), 0)
    lay3 = lambda s: (layer, 0, 0)
    return pl.pallas_call(
        _post_kernel,
        grid=(N_TILES + 1,),
        in_specs=[
            pl.BlockSpec((TM, D), row),
            pl.BlockSpec((MLA_HEADS // 2, TM, LANES), lambda s: (0, jnp.minimum(s, PROMPT_TILES - 1), 0)),
            pl.BlockSpec((MLA_HEADS // 2, TM, LANES), lambda s: (0, jnp.maximum(cur(s) - PROMPT_TILES, 0), 0)),
            pl.BlockSpec((TM, CONV_CH), lambda s: (jnp.maximum(cur(s) - 1, 0), 0)),
            pl.BlockSpec((TM, CONV_CH), row),
            pl.BlockSpec((TM, CONV_CH), lambda s: (jnp.minimum(s + 1, N_TILES - 1), 0)),
            pl.BlockSpec((TM, NA_WIDTH), lambda s: (jnp.minimum(s, PROMPT_TILES - 1), 0)),
            pl.BlockSpec((TM, NA_WIDTH), lambda s: (jnp.maximum(cur(s) - PROMPT_TILES, 0), 0)),
            pl.BlockSpec((1, 1, MOD_W), lambda s: (layer * COND_ROWS + _mod_row(cur(s)), 0, 0)),
            pl.BlockSpec((1, CONV_K, CONV_CH), lay3),
            pl.BlockSpec((1, 1, CONV_CH), lay3),
            pl.BlockSpec((1, 1, CONV_CH), lay3),
            pl.BlockSpec((1, 1, CONV_CH), lay3),
            pl.BlockSpec((1, D, D), lay3),
            pl.BlockSpec((1, 1, D), lay3),
            pl.BlockSpec((N_EXPERTS, D), lambda s: (0, 0)),
            pl.BlockSpec((N_EXPERTS, TM), lambda s: (0, 0)),
        ],
        out_specs=[pl.BlockSpec((TM, D), row), pl.BlockSpec((TILE_CAP, D), row2),
                   pl.BlockSpec((TM, LANES), row2),
                   pl.BlockSpec((1, N_EXPERTS, LANES), lambda s: (prev(s), 0, 0)),
                   pl.BlockSpec((1, N_EXPERTS, LANES), lambda s: (prev(s), 0, 0))],
        out_shape=[jax.ShapeDtypeStruct((T_ALL, D), F32),
                   jax.ShapeDtypeStruct((N_TILES * TILE_CAP, D), BF16),
                   jax.ShapeDtypeStruct((T_ALL, LANES), F32),
                   jax.ShapeDtypeStruct((N_TILES, N_EXPERTS, LANES), jnp.int32),
                   jax.ShapeDtypeStruct((N_TILES, N_EXPERTS, LANES), jnp.int32)],
        scratch_shapes=[pltpu.VMEM((D, D), BF16), pltpu.VMEM((TM + 2 * CONV_PAD, CONV_CH), F32),
                        pltpu.VMEM((2, TM, D), BF16), pltpu.VMEM((2, N_EXPERTS, TM), F32)],
        compiler_params=_cparams(("arbitrary",)),
        name="post_l%d" % layer,
    )(x, om_p, om_s, u, u, u, on_p, on_s, mods3, prm["conv_w"], prm["conv_b"], prm["conv_ln_g"], prm["conv_ln_b"],
      prm["w_out"], prm["norm2_g"], prm["w_router_t"], prm["b_router_t"])


def _for_each_piece(n, fn):
    def pieces(sizes):
        for sz in sizes:
            @pl.when((n & sz) != 0)
            def _():
                fn(n & (-2 * sz), sz)

    n_big = len(RUN_SIZES) - 2

    @pl.when(n >= RUN_SIZES[n_big - 1])
    def _():
        pieces(RUN_SIZES[:n_big])

    pieces(RUN_SIZES[n_big:])


def _expert_kernel(st_ref, pd_ref, xs_hbm, wg_ref, wu_ref, wd_ref, ys_hbm, xy, wg_bf, wu_bf, wd_bf, zeros_buf,
                   sems):
    e = pl.program_id(0)
    slot = e % 2
    other = 1 - slot
    sem_tail = sems.at[4]

    def start_all(ex, sl, inbound):
        def body(j, cur):
            start = st_ref[j * N_EXPERTS + ex]
            n = pd_ref[j * N_EXPERTS + ex]

            def piece(off, sz):
                rows_hbm = pl.ds(pl.multiple_of(j * TILE_CAP + start + off, RUN_ALIGN), sz)
                rows_buf = pl.ds(pl.multiple_of(cur + off, RUN_ALIGN), sz)
                if inbound:
                    pltpu.make_async_copy(xs_hbm.at[rows_hbm], xy.at[sl, rows_buf], sems.at[sl]).start()
                else:
                    pltpu.make_async_copy(xy.at[sl, rows_buf], ys_hbm.at[rows_hbm], sems.at[2 + sl]).start()

            _for_each_piece(n, piece)
            return cur + n
        lax.fori_loop(0, N_TILES, body, 0)

    def expert_rows(ex):
        return lax.fori_loop(0, N_TILES, lambda j, c: c + pd_ref[j * N_EXPERTS + ex], 0, unroll=True)

    def wait_all(ex, sl, inbound):
        total = expert_rows(ex)
        for sz in WAIT_SIZES:
            @pl.when((total & sz) != 0)
            def _():
                rows = pl.ds(0, sz)
                if inbound:
                    pltpu.make_async_copy(xs_hbm.at[rows], xy.at[sl, rows], sems.at[sl]).wait()
                else:
                    pltpu.make_async_copy(xy.at[sl, rows], ys_hbm.at[rows], sems.at[2 + sl]).wait()
        return total

    def compute_chunks(first, last):
        def chunk(c, carry):
            rows = pl.ds(pl.multiple_of(c * MOE_CHUNK, MOE_CHUNK), MOE_CHUNK)
            x = xy[slot, rows, :]
            hid = _silu(_dot(x, wg_bf[...])) * _dot(x, wu_bf[...])
            xy[slot, rows, :] = _dot(hid.astype(BF16), wd_bf[...]).astype(BF16)
            return carry
        lax.fori_loop(first, last, chunk, 0)

    @pl.when(e == 0)
    def _():
        xy[...] = jnp.zeros(xy.shape, BF16)
        zeros_buf[...] = jnp.zeros(zeros_buf.shape, BF16)
        start_all(e, slot, True)

    wg_bf[...] = wg_ref[0, 0].astype(BF16)
    wu_bf[...] = wu_ref[0, 0].astype(BF16)
    wd_bf[...] = wd_ref[0, 0].astype(BF16)
    n_chunks = (wait_all(e, slot, True) + (MOE_CHUNK - 1)) // MOE_CHUNK
    mid = (n_chunks + 1) // 2
    compute_chunks(0, mid)

    @pl.when(e > 0)
    def _():
        wait_all(e - 1, other, False)

    @pl.when(e < N_EXPERTS - 1)
    def _():
        start_all(e + 1, other, True)

    compute_chunks(mid, n_chunks)
    start_all(e, slot, False)

    @pl.when(e == N_EXPERTS - 1)
    def _():
        wait_all(e, slot, False)

        def tail_copy(j, off, sz):
            used = st_ref[j * N_EXPERTS + e] + pd_ref[j * N_EXPERTS + e]
            rows_hbm = pl.ds(pl.multiple_of(j * TILE_CAP + used + off, RUN_ALIGN), sz)
            return pltpu.make_async_copy(zeros_buf.at[pl.ds(0, sz)], ys_hbm.at[rows_hbm], sem_tail)

        def tail_len(j):
            return TILE_CAP - (st_ref[j * N_EXPERTS + e] + pd_ref[j * N_EXPERTS + e])

        def t_start(j, c):
            _for_each_piece(tail_len(j), lambda off, sz: tail_copy(j, off, sz).start())
            return c

        def t_wait(j, c):
            _for_each_piece(tail_len(j), lambda off, sz: tail_copy(j, off, sz).wait())
            return c

        lax.fori_loop(0, N_TILES, t_start, 0)
        lax.fori_loop(0, N_TILES, t_wait, 0)


def _experts(layer, starts, pads, xs, prm):
    wspec = lambda shape: pl.BlockSpec((1, 1) + shape, lambda e, st, pd: (layer, e, 0, 0))
    return pl.pallas_call(
        _expert_kernel,
        grid_spec=pltpu.PrefetchScalarGridSpec(
            num_scalar_prefetch=2,
            grid=(N_EXPERTS,),
            in_specs=[pl.BlockSpec(memory_space=pl.ANY), wspec((D, EXPERT_FF)), wspec((D, EXPERT_FF)),
                      wspec((EXPERT_FF, D))],
            out_specs=pl.BlockSpec(memory_space=pl.ANY),
            scratch_shapes=[pltpu.VMEM((2, XY_ROWS, D), BF16), pltpu.VMEM((D, EXPERT_FF), BF16),
                            pltpu.VMEM((D, EXPERT_FF), BF16), pltpu.VMEM((EXPERT_FF, D), BF16),
                            pltpu.VMEM((RUN_SIZES[0], D), BF16),
                            pltpu.SemaphoreType.DMA((5,))]),
        out_shape=jax.ShapeDtypeStruct((N_TILES * TILE_CAP, D), BF16),
        compiler_params=pltpu.CompilerParams(dimension_semantics=("arbitrary",),
                                             vmem_limit_bytes=EXPERT_VMEM_LIMIT),
        name="experts_l%d" % layer,
    )(starts, pads, xs, prm["w_e_gate"], prm["w_e_up"], prm["w_e_down"])


def _final_kernel(x1_ref, ys_ref, pc_ref, mod_ref, op_ref, os_ref):
    x = _moe_combine(x1_ref, ys_ref, pc_ref, mod_ref)
    i = pl.program_id(0)

    @pl.when(i < PROMPT_TILES)
    def _():
        op_ref[...] = x

    @pl.when(i >= PROMPT_TILES)
    def _():
        os_ref[...] = x


def _final(x1, ys, pcol, mods3):
    row = lambda i: (i, 0)
    return pl.pallas_call(
        _final_kernel,
        grid=(N_TILES,),
        in_specs=[pl.BlockSpec((TM, D), row), pl.BlockSpec((TILE_CAP, D), row), pl.BlockSpec((TM, LANES), row),
                  pl.BlockSpec((1, 1, MOD_W), lambda i: ((DEPTH - 1) * COND_ROWS + _mod_row(i), 0, 0))],
        out_specs=[pl.BlockSpec((TM, D), lambda i: (jnp.minimum(i, PROMPT_TILES - 1), 0)),
                   pl.BlockSpec((TM, D), lambda i: (jnp.maximum(i - PROMPT_TILES, 0), 0))],
        out_shape=[jax.ShapeDtypeStruct((T_PROMPT, D), F32), jax.ShapeDtypeStruct((T_SAMPLE, D), F32)],
        compiler_params=_cparams(("arbitrary",)),
        name="final_residual",
    )(x1, ys, pcol, mods3)


def _rope_tables():
    t = np.arange(SAMPLE_LEN)
    n_freq = QK_ROPE // 4
    inv_freq = ROPE_THETA ** (-np.arange(n_freq, dtype=np.float32) / n_freq)
    row = (t // GRID_W).astype(np.float32)
    col = (t % GRID_W).astype(np.float32)
    ang = jnp.concatenate([jnp.asarray(row)[:, None] * inv_freq, jnp.asarray(col)[:, None] * inv_freq],
                          axis=-1)
    cos_p = jnp.repeat(jnp.cos(ang), 2, axis=-1)
    sin_p = jnp.repeat(jnp.sin(ang), 2, axis=-1) * jnp.tile(jnp.asarray([-1.0, 1.0], F32), QK_ROPE // 2)
    pad_l = jnp.ones((SAMPLE_LEN, QK_NOPE), F32)
    pad_r = jnp.ones((SAMPLE_LEN, LANES - QK_HEAD), F32)
    cos = jnp.concatenate([pad_l, cos_p, pad_r], axis=-1)
    sin = jnp.concatenate([0 * pad_l, sin_p, 0 * pad_r], axis=-1)
    ident_c = jnp.ones((TM, LANES), F32)
    return jnp.concatenate([ident_c, cos], axis=0), jnp.concatenate([0 * ident_c, sin], axis=0)


def _pad_lanes(x, lo, total):
    pad = [(0, 0)] * (x.ndim - 1) + [(lo, total - lo - x.shape[-1])]
    return jnp.pad(x, pad)


def kernel(x_prompt, x_sample, cache_mla_ckv, cache_mla_krope, cache_na_k, cache_na_v, c, c_ctx, w_ada, b_ada, norm1_g, norm2_g, w_in, q_a_g, kv_a_g, w_q_up, w_kv_up, mla_q_norm_g, mla_k_norm_g, conv_dw_w, conv_dw_b, conv_ln_g, conv_ln_b, na_q_norm_g, na_k_norm_g, na_rpb, w_out, w_router, b_router, w_e_gate, w_e_up, w_e_down):
    cos, sin = _rope_tables()
    vec = lambda a: a.reshape(DEPTH, 1, a.shape[-1])
    prm = {
        "norm1_g": vec(norm1_g), "norm2_g": vec(norm2_g), "w_in": w_in, "q_a_g": vec(q_a_g),
        "kv_a_g": vec(kv_a_g), "w_q_up": w_q_up, "w_kv_up": w_kv_up,
        "q_norm_g": vec(_pad_lanes(mla_q_norm_g, 0, LANES)), "k_norm_g": vec(_pad_lanes(mla_k_norm_g, 0, LANES)),
        "na_q_g": vec(jnp.tile(na_q_norm_g, (1, 2))), "na_k_g": vec(jnp.tile(na_k_norm_g, (1, 2))),
        "cos": cos, "sin": sin,
        "conv_w": conv_dw_w, "conv_b": vec(conv_dw_b), "conv_ln_g": vec(conv_ln_g), "conv_ln_b": vec(conv_ln_b),
        "w_out": w_out, "w_router_t": w_router.T,
        "b_router_t": jnp.broadcast_to(b_router[:, None], (N_EXPERTS, TM)),
        "w_e_gate": w_e_gate, "w_e_up": w_e_up, "w_e_down": w_e_down,
    }

    cond = jnp.concatenate([c_ctx[None], c, jnp.zeros((COND_ROWS - N_COND, D), F32)], axis=0)
    mods3 = _modulation(cond.T, w_ada, b_ada).reshape(DEPTH * COND_ROWS, 1, MOD_W)

    kc, vc = _ctx_kv(cache_mla_ckv, cache_mla_krope, prm)
    bias = _na_bias(na_rpb)
    cache_k = cache_na_k.reshape(N_SAMPLE, DEPTH, PAST, NA_WIDTH)
    cache_v = cache_na_v.reshape(N_SAMPLE, DEPTH, PAST, NA_WIDTH)

    x = x1 = ys = pcol = None
    new_ckv, new_kr, new_nak, new_nav = [], [], [], []
    for layer in range(DEPTH):
        if layer == 0:
            outs = _pre(layer, False, (x_prompt.reshape(T_PROMPT, D), x_sample.reshape(T_SAMPLE, D)), mods3, prm)
        else:
            outs = _pre(layer, True, (x1, ys, pcol), mods3, prm)
        x, q, k, v, ckv_n, kr, u, nq, nk, nv, nkf, nvf = outs
        om_p, on_p = _prompt_attn(q, k, v, nq, nk, nv)
        om_s = _sample_mla(layer, q, k, v, kc, vc)
        on_s = _sample_na(layer, nq, nk, nv, cache_k, cache_v, bias)
        x1, xs, pcol, run_start, run_len = _post(layer, x, om_p, om_s, u, on_p, on_s, mods3, prm)
        ys = _experts(layer, run_start[:, :, 0].reshape(-1), run_len[:, :, 0].reshape(-1), xs, prm)
        new_ckv.append(ckv_n[:T_PROMPT].reshape(N_PROMPT, PROMPT_LEN, KV_LORA))
        new_kr.append(kr[:T_PROMPT, QK_NOPE:QK_HEAD].reshape(N_PROMPT, PROMPT_LEN, QK_ROPE))
        new_nak.append(nkf[:T_PROMPT].reshape(N_PROMPT, PROMPT_LEN, NA_HEADS, NA_DIM))
        new_nav.append(nvf[:T_PROMPT].reshape(N_PROMPT, PROMPT_LEN, NA_HEADS, NA_DIM))
    y_prompt, y_sample = _final(x1, ys, pcol, mods3)
    return (y_prompt.reshape(N_PROMPT, PROMPT_LEN, D), y_sample.reshape(N_SAMPLE, SAMPLE_LEN, D),
            jnp.stack(new_ckv, axis=1), jnp.stack(new_kr, axis=1), jnp.stack(new_nak, axis=1),
            jnp.stack(new_nav, axis=1))
```

```python
import functools

import numpy as np
import jax
import jax.numpy as jnp
from jax import lax
from jax.experimental import pallas as pl
from jax.experimental.pallas import tpu as pltpu

F32 = jnp.float32
BF16 = jnp.bfloat16

D = 1024
N_PROMPT = 16
PROMPT_LEN = 256
N_SAMPLE = 4
SAMPLE_LEN = 1024
DEPTH = 4
PAST = 512
GRID_W = 64
GRID_ROWS = SAMPLE_LEN // GRID_W
MLA_HEADS = 8
QK_NOPE = 64
QK_ROPE = 32
QK_HEAD = QK_NOPE + QK_ROPE
V_HEAD = 64
Q_LORA = 256
KV_LORA = 128
CONV_CH = 256
CONV_K = 31
NA_HEADS = 4
NA_DIM = 64
NA_WIDTH = NA_HEADS * NA_DIM
NA_WIN_H = 8
NA_WIN_W = 16
N_EXPERTS = 16
N_GROUPS = 4
EPG = 4
EXPERT_FF = 512
ROPE_THETA = 10000.0
EPS = 1e-6
NEG_INF = -1e30
MLA_SCALE = QK_HEAD ** -0.5
NA_SCALE = NA_DIM ** -0.5

LANES = 128
SUBLANES = 8
TM = 256
T_PROMPT = N_PROMPT * PROMPT_LEN
T_SAMPLE = N_SAMPLE * SAMPLE_LEN
T_ALL = T_PROMPT + T_SAMPLE
N_TILES = T_ALL // TM
PROMPT_TILES = T_PROMPT // TM
TILES_PER_SAMPLE = SAMPLE_LEN // TM
N_COND = 1 + N_SAMPLE
COND_ROWS = 8
MOD_W = 6 * D

COL_Q = 0
COL_KV = Q_LORA
COL_KR = COL_KV + KV_LORA
COL_CONV = COL_KR + LANES
COL_NA = COL_CONV + 2 * CONV_CH
IN_COLS_P = COL_NA + 3 * NA_WIDTH
IN_COLS = IN_COLS_P - (LANES - QK_ROPE)

NA_KEY_ROWS = 12
NA_KEYS = NA_KEY_ROWS * GRID_W

VMEM_LIMIT = 56 * 1024 * 1024

RUN_ALIGN = 16
TILE_CAP = 768
RUN_SIZES = (256, 128, 64, 32, 16)
MOE_CHUNK = 256
XY_ROWS = N_TILES * TM
WAIT_SIZES = (8192, 4096, 2048, 1024, 512, 256, 128, 64, 32, 16)
EXPERT_VMEM_LIMIT = 60 * 1024 * 1024


def _cparams(sem):
    return pltpu.CompilerParams(dimension_semantics=sem, vmem_limit_bytes=VMEM_LIMIT)


def _mod_row(i):
    return jnp.where(i < PROMPT_TILES, 0, 1 + (i - PROMPT_TILES) // TILES_PER_SAMPLE)


def _rope_block(i):
    return jnp.where(i < PROMPT_TILES, 0, 1 + (i - PROMPT_TILES) % TILES_PER_SAMPLE)


def _dot(a, b):
    return jnp.dot(a, b, preferred_element_type=F32)


def _dot_t(a, b):
    return lax.dot_general(a, b, (((1,), (1,)), ((), ())), preferred_element_type=F32)


def _silu(x):
    return x * jax.nn.sigmoid(x)


def _lane_is_low(shape):
    return lax.broadcasted_iota(jnp.int32, shape, len(shape) - 1) < (LANES // 2)


MOD_BLK = 768


def _mod_kernel(ct_ref, w_ref, b_ref, o_ref, cb_ref):
    first = (pl.program_id(0) == 0) & (pl.program_id(1) == 0)

    @pl.when(first)
    def _():
        s = _silu(ct_ref[...])
        for m in range(N_COND):
            cb_ref[m] = jnp.broadcast_to(s[:, m:m + 1], (D, LANES))

    for j in range(MOD_BLK // LANES):
        w = w_ref[0, :, j * LANES:(j + 1) * LANES]
        rows = [jnp.sum(w * cb_ref[m], axis=0, keepdims=True) for m in range(N_COND)]
        rows.append(jnp.zeros((COND_ROWS - N_COND, LANES), F32))
        o_ref[0, :, j * LANES:(j + 1) * LANES] = (
            jnp.concatenate(rows, axis=0) + b_ref[0, :, j * LANES:(j + 1) * LANES])


def _modulation(cond_t, w_ada, b_ada):
    nb = MOD_W // MOD_BLK
    return pl.pallas_call(
        _mod_kernel,
        grid=(DEPTH, nb),
        in_specs=[
            pl.BlockSpec((D, COND_ROWS), lambda l, j: (0, 0)),
            pl.BlockSpec((1, D, MOD_BLK), lambda l, j: (l, 0, j)),
            pl.BlockSpec((1, 1, MOD_BLK), lambda l, j: (l, 0, j)),
        ],
        out_specs=pl.BlockSpec((1, COND_ROWS, MOD_BLK), lambda l, j: (l, 0, j)),
        out_shape=jax.ShapeDtypeStruct((DEPTH, COND_ROWS, MOD_W), F32),
        scratch_shapes=[pltpu.VMEM((N_COND, D, LANES), F32)],
        compiler_params=_cparams(("arbitrary", "arbitrary")),
        name="modulation",
    )(cond_t, w_ada, b_ada.reshape(DEPTH, 1, MOD_W))


def _lane_sums(sq, group):
    k = sq.shape[1]
    hi = sq.astype(BF16)
    lo = (sq - hi.astype(F32)).astype(BF16)
    if group == k:
        sel = jnp.ones((k, LANES), BF16)
    else:
        r = lax.broadcasted_iota(jnp.int32, (k, LANES), 0)
        c = lax.broadcasted_iota(jnp.int32, (k, LANES), 1)
        sel = ((r // group) == (c // group)).astype(BF16)
    return _dot(hi, sel) + _dot(lo, sel)


def _rms(x, width):
    rinv = lax.rsqrt(_lane_sums(x * x, x.shape[1]) * (1.0 / width) + EPS)
    reps = x.shape[1] // LANES
    return x * (rinv if reps == 1 else jnp.concatenate([rinv] * reps, axis=1))


def _rope(x, cos, sin):
    lane = lax.broadcasted_iota(jnp.int32, x.shape, 1)
    partner = jnp.where((lane & 1) == 0, pltpu.roll(x, LANES - 1, 1), pltpu.roll(x, 1, 1))
    return x * cos + partner * sin


def _pair_rms(x, gain):
    ms = _lane_sums(x * x, NA_DIM) * (1.0 / NA_DIM)
    return x * lax.rsqrt(ms + EPS) * gain


def _softmax_pv(scores, values):
    m = scores[0].max(axis=-1, keepdims=True)
    for s in scores[1:]:
        m = jnp.maximum(m, s.max(axis=-1, keepdims=True))
    acc = None
    den = None
    for s, v in zip(scores, values):
        p = jnp.exp(s - m)
        d = jnp.sum(p, axis=-1, keepdims=True)
        o = _dot(p.astype(BF16), v)
        acc = o if acc is None else acc + o
        den = d if den is None else den + d
    return acc * (1.0 / den)


def _moe_combine(x1_ref, ys_ref, pc_ref, modp_ref):
    pc = pc_ref[...]
    lane = lax.broadcasted_iota(jnp.int32, (TM, TILE_CAP), 1)
    ys = ys_ref[...]
    y1 = _dot((lane == pc[:, 0:1].astype(jnp.int32)).astype(BF16), ys)
    y2 = _dot((lane == pc[:, 1:2].astype(jnp.int32)).astype(BF16), ys)
    gate2 = modp_ref[0, :, 5 * D:6 * D]
    return x1_ref[...] + gate2 * (pc[:, 2:3] * y1 + pc[:, 3:4] * y2)


def _place_w_in(w_ref, w_bf):
    off_kv = Q_LORA + KV_LORA
    off_kr = off_kv + QK_ROPE
    w_bf[:, 0:off_kv] = w_ref[0, :, 0:off_kv].astype(BF16)
    w_bf[:, COL_KR:COL_KR + LANES] = jnp.zeros((D, LANES), BF16)
    w_bf[:, COL_KR + QK_NOPE:COL_KR + QK_HEAD] = w_ref[0, :, off_kv:off_kr].astype(BF16)
    w_bf[:, COL_CONV:IN_COLS_P] = w_ref[0, :, off_kr:off_kr + (IN_COLS_P - COL_CONV)].astype(BF16)


def _place_w_q(w_ref, w_bf):
    w_bf[...] = jnp.zeros(w_bf.shape, BF16)
    for hd in range(MLA_HEADS):
        w_bf[:, hd * LANES:hd * LANES + QK_HEAD] = w_ref[0, :, hd * QK_HEAD:(hd + 1) * QK_HEAD].astype(BF16)


def _place_w_kv(w_ref, wk_bf, wv_bf):
    wk_bf[...] = jnp.zeros(wk_bf.shape, BF16)
    for hd in range(MLA_HEADS):
        c0 = hd * (QK_NOPE + V_HEAD)
        wk_bf[:, hd * LANES:hd * LANES + QK_NOPE] = w_ref[0, :, c0:c0 + QK_NOPE].astype(BF16)
        wv_bf[:, hd * V_HEAD:(hd + 1) * V_HEAD] = w_ref[0, :, c0 + QK_NOPE:c0 + QK_NOPE + V_HEAD].astype(BF16)


def _pre_kernel(has_moe, *refs):
    if has_moe:
        x1_ref, ys_ref, pc_ref, modp_ref = refs[:4]
        refs = refs[4:]
    else:
        xp_ref, xs_ref = refs[:2]
        refs = refs[2:]
    (mod_ref, n1g_ref, win_ref, qag_ref, kvag_ref, wq_ref, wkv_ref, qng_ref, kng_ref,
     naqg_ref, nakg_ref, cos_ref, sin_ref) = refs[:13]
    refs = refs[13:]
    xo_ref = refs[0]
    refs = refs[1:]
    (q_out, k_out, v_out, ckv_out, kr_out, u_out, naq_out, nak_out, nav_out, nakf_out, navf_out,
     win_bf, wq_bf, wk_bf, wv_bf, proj_buf) = refs

    s = pl.program_id(0)
    i = jnp.minimum(s, N_TILES - 1)

    @pl.when(s == 0)
    def _():
        _place_w_in(win_ref, win_bf)
        _place_w_q(wq_ref, wq_bf)
        _place_w_kv(wkv_ref, wk_bf, wv_bf)
        proj_buf[...] = jnp.zeros(proj_buf.shape, F32)

    if has_moe:
        x = _moe_combine(x1_ref, ys_ref, pc_ref, modp_ref)
    else:
        x = jnp.where(i < PROMPT_TILES, xp_ref[...], xs_ref[...])
    xo_ref[...] = x
    shift = mod_ref[0, :, 0:D]
    scale = mod_ref[0, :, D:2 * D]
    h = _rms(x, D) * n1g_ref[0] * (1.0 + scale) + shift
    proj = proj_buf[(s + 1) % 2]
    proj_buf[s % 2] = _dot(h.astype(BF16), win_bf[...])

    cos = cos_ref[...]
    sin = sin_ref[...]

    q_lat = _rms(proj[:, COL_Q:COL_Q + Q_LORA], Q_LORA) * qag_ref[0]
    q = _dot(q_lat.astype(BF16), wq_bf[...])
    for hd in range(MLA_HEADS):
        qh = q[:, hd * LANES:(hd + 1) * LANES]
        qh = _rope(_rms(qh, QK_HEAD) * qng_ref[0], cos, sin)
        q_out[hd] = (qh * MLA_SCALE).astype(BF16)

    ckv_n = _rms(proj[:, COL_KV:COL_KV + KV_LORA], KV_LORA) * kvag_ref[0]
    kr = proj[:, COL_KR:COL_KR + LANES]
    ckv_b = ckv_n.astype(BF16)
    k_nope = _dot(ckv_b, wk_bf[...])
    kr_rot = _rope(kr * kng_ref[0], cos, sin)
    kr_sq = kr * kr
    for hd in range(MLA_HEADS):
        kn = k_nope[:, hd * LANES:(hd + 1) * LANES]
        rinv = lax.rsqrt(_lane_sums(kn * kn + kr_sq, LANES) * (1.0 / QK_HEAD) + EPS)
        k_out[hd] = ((kn * kng_ref[0] + kr_rot) * rinv).astype(BF16)
    v = _dot(ckv_b, wv_bf[...])
    for p in range(MLA_HEADS // 2):
        v_out[p] = v[:, p * LANES:(p + 1) * LANES].astype(BF16)

    a = proj[:, COL_CONV:COL_CONV + CONV_CH]
    g = proj[:, COL_CONV + CONV_CH:COL_CONV + 2 * CONV_CH]
    u_out[...] = a * jax.nn.sigmoid(g)

    k_slabs, v_slabs = [], []
    for p in range(NA_HEADS // 2):
        sl = slice(p * LANES, (p + 1) * LANES)
        qn = _pair_rms(proj[:, COL_NA + p * LANES:COL_NA + (p + 1) * LANES], naqg_ref[0])
        kn = _pair_rms(proj[:, COL_NA + NA_WIDTH + p * LANES:COL_NA + NA_WIDTH + (p + 1) * LANES],
                       nakg_ref[0])
        vn = proj[:, COL_NA + 2 * NA_WIDTH + p * LANES:COL_NA + 2 * NA_WIDTH + (p + 1) * LANES]
        naq_out[:, sl] = (qn * NA_SCALE).astype(BF16)
        nak_out[:, sl] = kn.astype(BF16)
        nav_out[:, sl] = vn.astype(BF16)
        k_slabs.append(kn)
        v_slabs.append(vn)

    kr_t = kr.T
    nak_t = jnp.concatenate(k_slabs, axis=1).T.reshape(NA_HEADS, NA_DIM, TM)
    nav_t = jnp.concatenate(v_slabs, axis=1).T.reshape(NA_HEADS, NA_DIM, TM)

    @pl.when(s <= PROMPT_TILES)
    def _():
        ckv_out[...] = ckv_n
        kr_out[0] = kr_t[QK_NOPE:QK_HEAD, :]
        nakf_out[0] = nak_t
        navf_out[0] = nav_t


def _pre(layer, has_moe, x_or_parts, mods3, prm):
    cur = lambda s: jnp.minimum(s, N_TILES - 1)
    prev = lambda s: jnp.maximum(s - 1, 0)
    row = lambda s: (cur(s), 0)
    row2 = lambda s: (prev(s), 0)
    lay3 = lambda s: (layer, 0, 0)
    in_specs = []
    args = []
    if has_moe:
        x1, ys, pcol = x_or_parts
        in_specs += [pl.BlockSpec((TM, D), row), pl.BlockSpec((TILE_CAP, D), row),
                     pl.BlockSpec((TM, LANES), row),
                     pl.BlockSpec((1, 1, MOD_W), lambda s: ((layer - 1) * COND_ROWS + _mod_row(cur(s)), 0, 0))]
        args += [x1, ys, pcol, mods3]
    else:
        x_prompt, x_sample = x_or_parts
        in_specs += [pl.BlockSpec((TM, D), lambda s: (jnp.minimum(s, PROMPT_TILES - 1), 0)),
                     pl.BlockSpec((TM, D), lambda s: (jnp.maximum(cur(s) - PROMPT_TILES, 0), 0))]
        args += [x_prompt, x_sample]
    in_specs += [
        pl.BlockSpec((1, 1, MOD_W), lambda s: (layer * COND_ROWS + _mod_row(cur(s)), 0, 0)),
        pl.BlockSpec((1, 1, D), lay3),
        pl.BlockSpec((1, D, IN_COLS), lay3),
        pl.BlockSpec((1, 1, Q_LORA), lay3),
        pl.BlockSpec((1, 1, KV_LORA), lay3),
        pl.BlockSpec((1, Q_LORA, MLA_HEADS * QK_HEAD), lay3),
        pl.BlockSpec((1, KV_LORA, MLA_HEADS * (QK_NOPE + V_HEAD)), lay3),
        pl.BlockSpec((1, 1, LANES), lay3),
        pl.BlockSpec((1, 1, LANES), lay3),
        pl.BlockSpec((1, 1, LANES), lay3),
        pl.BlockSpec((1, 1, LANES), lay3),
        pl.BlockSpec((TM, LANES), lambda s: (_rope_block(prev(s)), 0)),
        pl.BlockSpec((TM, LANES), lambda s: (_rope_block(prev(s)), 0)),
    ]
    args += [mods3, prm["norm1_g"], prm["w_in"], prm["q_a_g"], prm["kv_a_g"], prm["w_q_up"], prm["w_kv_up"],
             prm["q_norm_g"], prm["k_norm_g"], prm["na_q_g"], prm["na_k_g"], prm["cos"], prm["sin"]]
    out_specs = [pl.BlockSpec((TM, D), row)]
    out_shape = [jax.ShapeDtypeStruct((T_ALL, D), F32)]
    head3 = lambda n: pl.BlockSpec((n, TM, LANES), lambda s: (0, prev(s), 0))
    seq = lambda s: jnp.minimum(prev(s), PROMPT_TILES - 1)
    out_specs += [head3(MLA_HEADS), head3(MLA_HEADS), head3(MLA_HEADS // 2),
                  pl.BlockSpec((TM, KV_LORA), lambda s: (seq(s), 0)),
                  pl.BlockSpec((1, QK_ROPE, TM), lambda s: (seq(s), 0, 0)),
                  pl.BlockSpec((TM, CONV_CH), row2)] + [pl.BlockSpec((TM, NA_WIDTH), row2)] * 3
    out_specs += [pl.BlockSpec((1, NA_HEADS, NA_DIM, TM), lambda s: (seq(s), 0, 0, 0))] * 2
    out_shape += [
        jax.ShapeDtypeStruct((MLA_HEADS, T_ALL, LANES), BF16),
        jax.ShapeDtypeStruct((MLA_HEADS, T_ALL, LANES), BF16),
        jax.ShapeDtypeStruct((MLA_HEADS // 2, T_ALL, LANES), BF16),
        jax.ShapeDtypeStruct((T_PROMPT, KV_LORA), F32),
        jax.ShapeDtypeStruct((N_PROMPT, QK_ROPE, PROMPT_LEN), F32),
        jax.ShapeDtypeStruct((T_ALL, CONV_CH), F32),
        jax.ShapeDtypeStruct((T_ALL, NA_WIDTH), BF16),
        jax.ShapeDtypeStruct((T_ALL, NA_WIDTH), BF16),
        jax.ShapeDtypeStruct((T_ALL, NA_WIDTH), BF16),
        jax.ShapeDtypeStruct((N_PROMPT, NA_HEADS, NA_DIM, PROMPT_LEN), F32),
        jax.ShapeDtypeStruct((N_PROMPT, NA_HEADS, NA_DIM, PROMPT_LEN), F32),
    ]
    return pl.pallas_call(
        functools.partial(_pre_kernel, has_moe),
        grid=(N_TILES + 1,),
        in_specs=in_specs,
        out_specs=out_specs,
        out_shape=out_shape,
        scratch_shapes=[pltpu.VMEM((D, IN_COLS_P), BF16), pltpu.VMEM((Q_LORA, MLA_HEADS * LANES), BF16),
                        pltpu.VMEM((KV_LORA, MLA_HEADS * LANES), BF16),
                        pltpu.VMEM((KV_LORA, MLA_HEADS * V_HEAD), BF16),
                        pltpu.VMEM((2, TM, IN_COLS_P), F32)],
        compiler_params=_cparams(("arbitrary",)),
        name="pre_l%d" % layer,
    )(*args)


def _ctx_kernel(ckv_ref, kr_ref, wkv_ref, kng_ref, kc_out, vc_out, wk_bf, wv_bf):
    @pl.when(pl.program_id(1) == 0)
    def _():
        _place_w_kv(wkv_ref, wk_bf, wv_bf)

    ckv = ckv_ref[0, 0].astype(BF16)
    kr = jnp.concatenate([jnp.zeros((PAST, QK_NOPE), F32), kr_ref[0, 0],
                          jnp.zeros((PAST, LANES - QK_HEAD), F32)], axis=1)
    k_nope = _dot(ckv, wk_bf[...])
    for hd in range(MLA_HEADS):
        kh = k_nope[:, hd * LANES:(hd + 1) * LANES] + kr
        kc_out[0, 0, hd] = (_rms(kh, QK_HEAD) * kng_ref[0]).astype(BF16)
    v = _dot(ckv, wv_bf[...])
    for p in range(MLA_HEADS // 2):
        vc_out[0, 0, p] = v[:, p * LANES:(p + 1) * LANES].astype(BF16)


def _ctx_kv(cache_ckv, cache_krope, prm):
    return pl.pallas_call(
        _ctx_kernel,
        grid=(DEPTH, N_SAMPLE),
        in_specs=[
            pl.BlockSpec((1, 1, PAST, KV_LORA), lambda l, b: (b, l, 0, 0)),
            pl.BlockSpec((1, 1, PAST, QK_ROPE), lambda l, b: (b, l, 0, 0)),
            pl.BlockSpec((1, KV_LORA, MLA_HEADS * (QK_NOPE + V_HEAD)), lambda l, b: (l, 0, 0)),
            pl.BlockSpec((1, 1, LANES), lambda l, b: (l, 0, 0)),
        ],
        out_specs=[
            pl.BlockSpec((1, 1, MLA_HEADS, PAST, LANES), lambda l, b: (l, b, 0, 0, 0)),
            pl.BlockSpec((1, 1, MLA_HEADS // 2, PAST, LANES), lambda l, b: (l, b, 0, 0, 0)),
        ],
        out_shape=[
            jax.ShapeDtypeStruct((DEPTH, N_SAMPLE, MLA_HEADS, PAST, LANES), BF16),
            jax.ShapeDtypeStruct((DEPTH, N_SAMPLE, MLA_HEADS // 2, PAST, LANES), BF16),
        ],
        scratch_shapes=[pltpu.VMEM((KV_LORA, MLA_HEADS * LANES), BF16),
                        pltpu.VMEM((KV_LORA, MLA_HEADS * V_HEAD), BF16)],
        compiler_params=_cparams(("arbitrary", "arbitrary")),
        name="ctx_kv",
    )(cache_ckv, cache_krope, prm["w_kv_up"], prm["k_norm_g"])


N_DR = 2 * NA_WIN_H - 1
N_DC = 2 * NA_WIN_W - 1


def _bias_kernel(rpb_ref, o_ref, blk_ref):
    cq = lax.broadcasted_iota(jnp.int32, (GRID_W, GRID_W), 0)
    ck = lax.broadcasted_iota(jnp.int32, (GRID_W, GRID_W), 1)
    dc = jnp.clip(ck - cq + (NA_WIN_W - 1), 0, N_DC - 1)
    c0 = jnp.clip(cq - NA_WIN_W // 2, 0, GRID_W - NA_WIN_W)
    in_win = (ck >= c0) & (ck < c0 + NA_WIN_W)
    for dr in range(N_DR):
        acc = jnp.zeros((GRID_W, GRID_W), F32)
        for j in range(N_DC):
            acc = jnp.where(dc == j, rpb_ref[pl.program_id(0), pl.program_id(1), dr, j], acc)
        blk_ref[dr] = jnp.where(in_win, acc, NEG_INF)
    blk_ref[N_DR] = jnp.full((GRID_W, GRID_W), NEG_INF, F32)
    idx = _na_block_index()
    for i in range(TILES_PER_SAMPLE):
        for rq in range(TM // GRID_W):
            for pp in range(NA_KEY_ROWS // 2):
                pair = jnp.concatenate([blk_ref[int(idx[i, rq, 2 * pp])], blk_ref[int(idx[i, rq, 2 * pp + 1])]],
                                       axis=1)
                o_ref[0, 0, i, rq * GRID_W:(rq + 1) * GRID_W, pp * LANES:(pp + 1) * LANES] = pair


def _na_bias(na_rpb):
    return pl.pallas_call(
        _bias_kernel,
        grid=(DEPTH, NA_HEADS),
        in_specs=[pl.BlockSpec(memory_space=pltpu.SMEM)],
        out_specs=pl.BlockSpec((1, 1, TILES_PER_SAMPLE, TM, NA_KEYS), lambda l, h: (l, h, 0, 0, 0)),
        out_shape=jax.ShapeDtypeStruct((DEPTH, NA_HEADS, TILES_PER_SAMPLE, TM, NA_KEYS), F32),
        scratch_shapes=[pltpu.VMEM((N_DR + 1, GRID_W, GRID_W), F32)],
        compiler_params=_cparams(("arbitrary", "arbitrary")),
        name="na_bias",
    )(na_rpb)


def _na_block_index():
    idx = np.full((TILES_PER_SAMPLE, TM // GRID_W, NA_KEY_ROWS), N_DR, np.int32)
    for i in range(TILES_PER_SAMPLE):
        ks = _na_key_start_row(i)
        for rq in range(TM // GRID_W):
            r = i * (TM // GRID_W) + rq
            r0 = min(max(r - NA_WIN_H // 2, 0), GRID_ROWS - NA_WIN_H)
            for rk in range(NA_KEY_ROWS):
                kr = ks + rk
                if r0 <= kr < r0 + NA_WIN_H:
                    idx[i, rq, rk] = kr - r + (NA_WIN_H - 1)
    return idx


def _na_key_start_row(i):
    return 0 if i < TILES_PER_SAMPLE // 2 else GRID_ROWS - NA_KEY_ROWS


def _prompt_attn_kernel(q_ref, k_ref, v_ref, nq_ref, nk_ref, nv_ref, om_ref, on_ref):
    low = _lane_is_low((PROMPT_LEN, LANES))
    for p in range(MLA_HEADS // 2):
        outs = []
        for hh in range(2):
            hd = 2 * p + hh
            s = _dot_t(q_ref[hd], k_ref[hd])
            outs.append(_softmax_pv([s], [v_ref[p]]))
        om_ref[p] = jnp.where(low, outs[0], outs[1]).astype(BF16)
    for p in range(NA_HEADS // 2):
        sl = slice(p * LANES, (p + 1) * LANES)
        qp = nq_ref[:, sl]
        kp = nk_ref[:, sl]
        vp = nv_ref[:, sl]
        outs = []
        for hh in range(2):
            qm = jnp.where(low if hh == 0 else ~low, qp, jnp.zeros_like(qp))
            outs.append(_softmax_pv([_dot_t(qm, kp)], [vp]))
        on_ref[:, sl] = jnp.where(low, outs[0], outs[1]).astype(BF16)


def _prompt_attn(q, k, v, nq, nk, nv):
    head3 = lambda n: pl.BlockSpec((n, PROMPT_LEN, LANES), lambda b: (0, b, 0))
    row = pl.BlockSpec((PROMPT_LEN, NA_WIDTH), lambda b: (b, 0))
    return pl.pallas_call(
        _prompt_attn_kernel,
        grid=(N_PROMPT,),
        in_specs=[head3(MLA_HEADS), head3(MLA_HEADS), head3(MLA_HEADS // 2), row, row, row],
        out_specs=[head3(MLA_HEADS // 2), row],
        out_shape=[jax.ShapeDtypeStruct((MLA_HEADS // 2, T_PROMPT, LANES), BF16),
                   jax.ShapeDtypeStruct((T_PROMPT, NA_WIDTH), BF16)],
        compiler_params=_cparams(("arbitrary",)),
        name="prompt_attn",
    )(q, k, v, nq, nk, nv)


def _sample_mla_kernel(q_ref, k_ref, v_ref, kc_ref, vc_ref, o_ref):
    low = _lane_is_low((TM, LANES))
    for t in range(TILES_PER_SAMPLE):
        rows = slice(t * TM, (t + 1) * TM)
        outs = []
        for hh in range(2):
            q = q_ref[hh, rows, :]
            s_lat = _dot_t(q, k_ref[hh])
            s_ctx = _dot_t(q, kc_ref[0, 0, hh])
            outs.append(_softmax_pv([s_lat, s_ctx], [v_ref[0], vc_ref[0, 0, 0]]))
        o_ref[0, rows, :] = jnp.where(low, outs[0], outs[1]).astype(BF16)


def _sample_mla(layer, q, k, v, kc, vc):
    seq_blk0 = T_PROMPT // SAMPLE_LEN
    return pl.pallas_call(
        _sample_mla_kernel,
        grid=(N_SAMPLE, MLA_HEADS // 2),
        in_specs=[
            pl.BlockSpec((2, SAMPLE_LEN, LANES), lambda b, p: (p, seq_blk0 + b, 0)),
            pl.BlockSpec((2, SAMPLE_LEN, LANES), lambda b, p: (p, seq_blk0 + b, 0)),
            pl.BlockSpec((1, SAMPLE_LEN, LANES), lambda b, p: (p, seq_blk0 + b, 0)),
            pl.BlockSpec((1, 1, 2, PAST, LANES), lambda b, p: (layer, b, p, 0, 0)),
            pl.BlockSpec((1, 1, 1, PAST, LANES), lambda b, p: (layer, b, p, 0, 0)),
        ],
        out_specs=pl.BlockSpec((1, SAMPLE_LEN, LANES), lambda b, p: (p, b, 0)),
        out_shape=jax.ShapeDtypeStruct((MLA_HEADS // 2, T_SAMPLE, LANES), BF16),
        compiler_params=_cparams(("arbitrary", "arbitrary")),
        name="sample_mla",
    )(q, k, v, kc, vc)


def _sample_na_kernel(q_ref, k_ref, v_ref, kc_ref, vc_ref, b_ref, o_ref):
    kc = kc_ref[0, 0].astype(BF16)
    vc = vc_ref[0, 0].astype(BF16)
    low = _lane_is_low((TM, LANES))
    for t in range(TILES_PER_SAMPLE):
        rows = slice(t * TM, (t + 1) * TM)
        start = _na_key_start_row(t) * GRID_W
        k_win = k_ref[start:start + NA_KEYS, :]
        v_win = v_ref[start:start + NA_KEYS, :]
        q = q_ref[rows, :]
        outs = []
        for hh in range(2):
            qm = jnp.where(low if hh == 0 else ~low, q, jnp.zeros_like(q))
            s_loc = _dot_t(qm, k_win) + b_ref[0, hh, t]
            s_ctx = _dot_t(qm, kc)
            outs.append(_softmax_pv([s_loc, s_ctx], [v_win, vc]))
        o_ref[rows, :] = jnp.where(low, outs[0], outs[1]).astype(BF16)


def _sample_na(layer, nq, nk, nv, cache_k, cache_v, bias):
    seq_blk0 = T_PROMPT // SAMPLE_LEN
    return pl.pallas_call(
        _sample_na_kernel,
        grid=(NA_HEADS // 2, N_SAMPLE),
        in_specs=[
            pl.BlockSpec((SAMPLE_LEN, LANES), lambda p, b: (seq_blk0 + b, p)),
            pl.BlockSpec((SAMPLE_LEN, LANES), lambda p, b: (seq_blk0 + b, p)),
            pl.BlockSpec((SAMPLE_LEN, LANES), lambda p, b: (seq_blk0 + b, p)),
            pl.BlockSpec((1, 1, PAST, LANES), lambda p, b: (b, layer, 0, p)),
            pl.BlockSpec((1, 1, PAST, LANES), lambda p, b: (b, layer, 0, p)),
            pl.BlockSpec((1, 2, TILES_PER_SAMPLE, TM, NA_KEYS), lambda p, b: (layer, p, 0, 0, 0)),
        ],
        out_specs=pl.BlockSpec((SAMPLE_LEN, LANES), lambda p, b: (b, p)),
        out_shape=jax.ShapeDtypeStruct((T_SAMPLE, NA_WIDTH), BF16),
        compiler_params=_cparams(("arbitrary", "arbitrary")),
        name="sample_na",
    )(nq, nk, nv, cache_k, cache_v, bias)


CONV_PAD = 16


def _argmax_first(cols):
    best = cols[0]
    idx = jnp.zeros(best.shape, jnp.int32)
    for j in range(1, len(cols)):
        upd = cols[j] > best
        idx = jnp.where(upd, j, idx)
        best = jnp.where(upd, cols[j], best)
    return idx


def _route(aff, bias):
    sel = aff + bias
    rows = [sel[e:e + 1, :] for e in range(N_EXPERTS)]
    group_scores = []
    for g in range(N_GROUPS):
        c = rows[g * EPG:(g + 1) * EPG]
        best = None
        for a in range(EPG):
            for b in range(a + 1, EPG):
                s = c[a] + c[b]
                best = s if best is None else jnp.maximum(best, s)
        group_scores.append(best)
    g_idx = _argmax_first(group_scores)
    in_group = []
    for j in range(EPG):
        v = rows[j]
        for g in range(1, N_GROUPS):
            v = jnp.where(g_idx == g, rows[g * EPG + j], v)
        in_group.append(v)
    i1 = _argmax_first(in_group)
    masked = [jnp.where(i1 == j, -jnp.inf, in_group[j]) for j in range(EPG)]
    i2 = _argmax_first(masked)
    e1 = g_idx * EPG + i1
    e2 = g_idx * EPG + i2
    sub = lax.broadcasted_iota(jnp.int32, aff.shape, 0)
    w1 = jnp.sum(jnp.where(sub == e1, aff, 0.0), axis=0, keepdims=True)
    w2 = jnp.sum(jnp.where(sub == e2, aff, 0.0), axis=0, keepdims=True)
    tot = w1 + w2
    return e1, e2, w1 / tot, w2 / tot


def _sort_tile(e1, e2):
    sub = lax.broadcasted_iota(jnp.int32, (N_EXPERTS, TM), 0)
    oh1 = sub == e1
    oh2 = sub == e2
    one = jnp.ones((N_EXPERTS, TM), F32)
    cnt1 = jnp.sum(jnp.where(oh1, one, 0.0), axis=1, keepdims=True) * one
    cnt2 = jnp.sum(jnp.where(oh2, one, 0.0), axis=1, keepdims=True) * one
    tot = (cnt1 + cnt2).astype(jnp.int32)
    padded = (tot + (RUN_ALIGN - 1)) & (-RUN_ALIGN)
    ee = lax.broadcasted_iota(jnp.int32, (N_EXPERTS, N_EXPERTS), 0)
    ef = lax.broadcasted_iota(jnp.int32, (N_EXPERTS, N_EXPERTS), 1)
    start = _dot((ef < ee).astype(BF16), padded.astype(BF16))
    ta = lax.broadcasted_iota(jnp.int32, (TM, TM), 0)
    tb = lax.broadcasted_iota(jnp.int32, (TM, TM), 1)
    before = (ta < tb).astype(BF16)
    rank1 = _dot(oh1.astype(BF16), before)
    rank2 = _dot(oh2.astype(BF16), before)
    pos1 = jnp.sum(jnp.where(oh1, start + rank1, 0.0), axis=0, keepdims=True)
    pos2 = jnp.sum(jnp.where(oh2, start + cnt1 + rank2, 0.0), axis=0, keepdims=True)
    return pos1.astype(jnp.int32), pos2.astype(jnp.int32), start.astype(jnp.int32), padded


def _post_kernel(x_ref, omp_ref, oms_ref, up_ref, uc_ref, un_ref, onp_ref, ons_ref, mod_ref, cw_ref, cb_ref,
                 lg_ref, lb_ref, wo_ref, n2g_ref, wr_ref, br_ref, x1_out, xs_out, pc_out, st_out, pd_out,
                 wo_bf, ext_ref, shift_ref, h2_buf, logit_buf):
    s = pl.program_id(0)
    i = jnp.minimum(s, N_TILES - 1)

    @pl.when(s == 0)
    def _():
        wo_bf[...] = wo_ref[0].astype(BF16)
        h2_buf[...] = jnp.zeros(h2_buf.shape, BF16)
        logit_buf[...] = jnp.zeros(logit_buf.shape, F32)

    pos_in_seq = (i - PROMPT_TILES) % TILES_PER_SAMPLE
    is_first = (i < PROMPT_TILES) | (pos_in_seq == 0)
    is_last = (i < PROMPT_TILES) | (pos_in_seq == TILES_PER_SAMPLE - 1)
    ext_ref[0:CONV_PAD, :] = jnp.where(is_first, 0.0, up_ref[TM - CONV_PAD:TM, :])
    ext_ref[CONV_PAD:CONV_PAD + TM, :] = uc_ref[...]
    ext_ref[CONV_PAD + TM:2 * CONV_PAD + TM, :] = jnp.where(is_last, 0.0, un_ref[0:CONV_PAD, :])
    acc = None
    first_off = CONV_PAD - CONV_K // 2
    for b in range(SUBLANES):
        taps = [j for j in range(CONV_K) if (first_off + j) % SUBLANES == b]
        if not taps:
            continue
        reach = max(first_off + j - b for j in taps)
        if b:
            shift_ref[b, 0:TM + reach, :] = ext_ref[b:b + TM + reach, :]
        for j in taps:
            a8 = first_off + j - b
            window = shift_ref[b, a8:a8 + TM, :] if b else ext_ref[a8:a8 + TM, :]
            term = window * cw_ref[0, j:j + 1, :]
            acc = term if acc is None else acc + term
    conv = acc + cb_ref[0]
    wide = lambda t: jnp.concatenate([t] * (CONV_CH // LANES), axis=1)
    cc = conv - wide(_lane_sums(conv, CONV_CH) * (1.0 / CONV_CH))
    var = _lane_sums(cc * cc, CONV_CH) * (1.0 / CONV_CH)
    o_conv = _silu(cc * wide(lax.rsqrt(var + EPS)) * lg_ref[0] + lb_ref[0])

    is_prompt = i < PROMPT_TILES
    mix = jnp.concatenate([jnp.where(is_prompt, omp_ref[p], oms_ref[p]) for p in range(MLA_HEADS // 2)]
                          + [o_conv.astype(BF16), jnp.where(is_prompt, onp_ref[...], ons_ref[...])],
                          axis=-1)
    att = _dot(mix, wo_bf[...])
    gate1 = mod_ref[0, :, 2 * D:3 * D]
    x1 = x_ref[...] + gate1 * att
    x1_out[...] = x1
    shift2 = mod_ref[0, :, 3 * D:4 * D]
    scale2 = mod_ref[0, :, 4 * D:5 * D]
    h2 = _rms(x1, D) * n2g_ref[0] * (1.0 + scale2) + shift2

    logits = lax.dot_general(wr_ref[...], h2, (((1,), (1,)), ((), ())), preferred_element_type=F32,
                             precision=lax.Precision.HIGHEST)
    h2_prev = h2_buf[(s + 1) % 2]
    logits_prev = logit_buf[(s + 1) % 2]
    h2_buf[s % 2] = h2.astype(BF16)
    logit_buf[s % 2] = logits

    e1, e2, w1, w2 = _route(jax.nn.sigmoid(logits_prev), br_ref[...])
    pos1, pos2, start, padded = _sort_tile(e1, e2)
    st_out[0] = start[:, 0:LANES]
    pd_out[0] = padded[:, 0:LANES]
    r = lax.broadcasted_iota(jnp.int32, (TILE_CAP, TM), 0)
    perm = ((r == pos1) | (r == pos2)).astype(BF16)
    xs_out[...] = _dot(perm, h2_prev).astype(BF16)
    sub = lax.broadcasted_iota(jnp.int32, (LANES, TM), 0)
    packed = jnp.where(sub == 0, pos1.astype(F32),
                       jnp.where(sub == 1, pos2.astype(F32),
                                 jnp.where(sub == 2, w1, jnp.where(sub == 3, w2, 0.0))))
    pc_out[...] = packed.T


def _post(layer, x, om_p, om_s, u, on_p, on_s, mods3, prm):
    cur = lambda s: jnp.minimum(s, N_TILES - 1)
    prev = lambda s: jnp.maximum(s - 1, 0)
    row = lambda s: (cur(s), 0)
    row2 = lambda s: (prev(s), 0)
    lay3 = lambda s: (layer, 0, 0)
    return pl.pallas_call(
        _post_kernel,
        grid=(N_TILES + 1,),
        in_specs=[
            pl.BlockSpec((TM, D), row),
            pl.BlockSpec((MLA_HEADS // 2, TM, LANES), lambda s: (0, jnp.minimum(s, PROMPT_TILES - 1), 0)),
            pl.BlockSpec((MLA_HEADS // 2, TM, LANES), lambda s: (0, jnp.maximum(cur(s) - PROMPT_TILES, 0), 0)),
            pl.BlockSpec((TM, CONV_CH), lambda s: (jnp.maximum(cur(s) - 1, 0), 0)),
            pl.BlockSpec((TM, CONV_CH), row),
            pl.BlockSpec((TM, CONV_CH), lambda s: (jnp.minimum(s + 1, N_TILES - 1), 0)),
            pl.BlockSpec((TM, NA_WIDTH), lambda s: (jnp.minimum(s, PROMPT_TILES - 1), 0)),
            pl.BlockSpec((TM, NA_WIDTH), lambda s: (jnp.maximum(cur(s) - PROMPT_TILES, 0), 0)),
            pl.BlockSpec((1, 1, MOD_W), lambda s: (layer * COND_ROWS + _mod_row(cur(s)), 0, 0)),
            pl.BlockSpec((1, CONV_K, CONV_CH), lay3),
            pl.BlockSpec((1, 1, CONV_CH), lay3),
            pl.BlockSpec((1, 1, CONV_CH), lay3),
            pl.BlockSpec((1, 1, CONV_CH), lay3),
            pl.BlockSpec((1, D, D), lay3),
            pl.BlockSpec((1, 1, D), lay3),
            pl.BlockSpec((N_EXPERTS, D), lambda s: (0, 0)),
            pl.BlockSpec((N_EXPERTS, TM), lambda s: (0, 0)),
        ],
        out_specs=[pl.BlockSpec((TM, D), row), pl.BlockSpec((TILE_CAP, D), row2),
                   pl.BlockSpec((TM, LANES), row2),
                   pl.BlockSpec((1, N_EXPERTS, LANES), lambda s: (prev(s), 0, 0)),
                   pl.BlockSpec((1, N_EXPERTS, LANES), lambda s: (prev(s), 0, 0))],
        out_shape=[jax.ShapeDtypeStruct((T_ALL, D), F32),
                   jax.ShapeDtypeStruct((N_TILES * TILE_CAP, D), BF16),
                   jax.ShapeDtypeStruct((T_ALL, LANES), F32),
                   jax.ShapeDtypeStruct((N_TILES, N_EXPERTS, LANES), jnp.int32),
                   jax.ShapeDtypeStruct((N_TILES, N_EXPERTS, LANES), jnp.int32)],
        scratch_shapes=[pltpu.VMEM((D, D), BF16), pltpu.VMEM((TM + 2 * CONV_PAD, CONV_CH), F32),
                        pltpu.VMEM((SUBLANES, TM + 2 * CONV_PAD, CONV_CH), F32),
                        pltpu.VMEM((2, TM, D), BF16), pltpu.VMEM((2, N_EXPERTS, TM), F32)],
        compiler_params=_cparams(("arbitrary",)),
        name="post_l%d" % layer,
    )(x, om_p, om_s, u, u, u, on_p, on_s, mods3, prm["conv_w"], prm["conv_b"], prm["conv_ln_g"], prm["conv_ln_b"],
      prm["w_out"], prm["norm2_g"], prm["w_router_t"], prm["b_router_t"])


def _for_each_piece(n, fn):
    def pieces(sizes):
        for sz in sizes:
            @pl.when((n & sz) != 0)
            def _():
                fn(n & (-2 * sz), sz)

    n_big = len(RUN_SIZES) - 2

    @pl.when(n >= RUN_SIZES[n_big - 1])
    def _():
        pieces(RUN_SIZES[:n_big])

    pieces(RUN_SIZES[n_big:])


def _expert_kernel(st_ref, pd_ref, xs_hbm, wg_ref, wu_ref, wd_ref, ys_hbm, xy, wg_bf, wu_bf, wd_bf, zeros_buf,
                   sems):
    e = pl.program_id(0)
    slot = e % 2
    other = 1 - slot
    sem_tail = sems.at[4]

    def start_all(ex, sl, inbound):
        def body(j, cur):
            start = st_ref[j * N_EXPERTS + ex]
            n = pd_ref[j * N_EXPERTS + ex]

            def piece(off, sz):
                rows_hbm = pl.ds(pl.multiple_of(j * TILE_CAP + start + off, RUN_ALIGN), sz)
                rows_buf = pl.ds(pl.multiple_of(cur + off, RUN_ALIGN), sz)
                if inbound:
                    pltpu.make_async_copy(xs_hbm.at[rows_hbm], xy.at[sl, rows_buf], sems.at[sl]).start()
                else:
                    pltpu.make_async_copy(xy.at[sl, rows_buf], ys_hbm.at[rows_hbm], sems.at[2 + sl]).start()

            _for_each_piece(n, piece)
            return cur + n
        lax.fori_loop(0, N_TILES, body, 0)

    def expert_rows(ex):
        return lax.fori_loop(0, N_TILES, lambda j, c: c + pd_ref[j * N_EXPERTS + ex], 0, unroll=True)

    def wait_all(ex, sl, inbound):
        total = expert_rows(ex)
        for sz in WAIT_SIZES:
            @pl.when((total & sz) != 0)
            def _():
                rows = pl.ds(0, sz)
                if inbound:
                    pltpu.make_async_copy(xs_hbm.at[rows], xy.at[sl, rows], sems.at[sl]).wait()
                else:
                    pltpu.make_async_copy(xy.at[sl, rows], ys_hbm.at[rows], sems.at[2 + sl]).wait()
        return total

    def compute_chunks(first, last):
        def chunk(c, carry):
            rows = pl.ds(pl.multiple_of(c * MOE_CHUNK, MOE_CHUNK), MOE_CHUNK)
            x = xy[slot, rows, :]
            hid = _silu(_dot(x, wg_bf[...])) * _dot(x, wu_bf[...])
            xy[slot, rows, :] = _dot(hid.astype(BF16), wd_bf[...]).astype(BF16)
            return carry
        lax.fori_loop(first, last, chunk, 0)

    @pl.when(e == 0)
    def _():
        xy[...] = jnp.zeros(xy.shape, BF16)
        zeros_buf[...] = jnp.zeros(zeros_buf.shape, BF16)
        start_all(e, slot, True)

    wg_bf[...] = wg_ref[0, 0].astype(BF16)
    wu_bf[...] = wu_ref[0, 0].astype(BF16)
    wd_bf[...] = wd_ref[0, 0].astype(BF16)
    n_chunks = (wait_all(e, slot, True) + (MOE_CHUNK - 1)) // MOE_CHUNK
    mid = (n_chunks + 1) // 2
    compute_chunks(0, mid)

    @pl.when(e > 0)
    def _():
        wait_all(e - 1, other, False)

    @pl.when(e < N_EXPERTS - 1)
    def _():
        start_all(e + 1, other, True)

    compute_chunks(mid, n_chunks)
    start_all(e, slot, False)

    @pl.when(e == N_EXPERTS - 1)
    def _():
        wait_all(e, slot, False)

        def tail_copy(j, off, sz):
            used = st_ref[j * N_EXPERTS + e] + pd_ref[j * N_EXPERTS + e]
            rows_hbm = pl.ds(pl.multiple_of(j * TILE_CAP + used + off, RUN_ALIGN), sz)
            return pltpu.make_async_copy(zeros_buf.at[pl.ds(0, sz)], ys_hbm.at[rows_hbm], sem_tail)

        def tail_len(j):
            return TILE_CAP - (st_ref[j * N_EXPERTS + e] + pd_ref[j * N_EXPERTS + e])

        def t_start(j, c):
            _for_each_piece(tail_len(j), lambda off, sz: tail_copy(j, off, sz).start())
            return c

        def t_wait(j, c):
            _for_each_piece(tail_len(j), lambda off, sz: tail_copy(j, off, sz).wait())
            return c

        lax.fori_loop(0, N_TILES, t_start, 0)
        lax.fori_loop(0, N_TILES, t_wait, 0)


def _experts(layer, starts, pads, xs, prm):
    wspec = lambda shape: pl.BlockSpec((1, 1) + shape, lambda e, st, pd: (layer, e, 0, 0))
    return pl.pallas_call(
        _expert_kernel,
        grid_spec=pltpu.PrefetchScalarGridSpec(
            num_scalar_prefetch=2,
            grid=(N_EXPERTS,),
            in_specs=[pl.BlockSpec(memory_space=pl.ANY), wspec((D, EXPERT_FF)), wspec((D, EXPERT_FF)),
                      wspec((EXPERT_FF, D))],
            out_specs=pl.BlockSpec(memory_space=pl.ANY),
            scratch_shapes=[pltpu.VMEM((2, XY_ROWS, D), BF16), pltpu.VMEM((D, EXPERT_FF), BF16),
                            pltpu.VMEM((D, EXPERT_FF), BF16), pltpu.VMEM((EXPERT_FF, D), BF16),
                            pltpu.VMEM((RUN_SIZES[0], D), BF16),
                            pltpu.SemaphoreType.DMA((5,))]),
        out_shape=jax.ShapeDtypeStruct((N_TILES * TILE_CAP, D), BF16),
        compiler_params=pltpu.CompilerParams(dimension_semantics=("arbitrary",),
                                             vmem_limit_bytes=EXPERT_VMEM_LIMIT),
        name="experts_l%d" % layer,
    )(starts, pads, xs, prm["w_e_gate"], prm["w_e_up"], prm["w_e_down"])


def _final_kernel(x1_ref, ys_ref, pc_ref, mod_ref, op_ref, os_ref):
    x = _moe_combine(x1_ref, ys_ref, pc_ref, mod_ref)
    i = pl.program_id(0)

    @pl.when(i < PROMPT_TILES)
    def _():
        op_ref[...] = x

    @pl.when(i >= PROMPT_TILES)
    def _():
        os_ref[...] = x


def _final(x1, ys, pcol, mods3):
    row = lambda i: (i, 0)
    return pl.pallas_call(
        _final_kernel,
        grid=(N_TILES,),
        in_specs=[pl.BlockSpec((TM, D), row), pl.BlockSpec((TILE_CAP, D), row), pl.BlockSpec((TM, LANES), row),
                  pl.BlockSpec((1, 1, MOD_W), lambda i: ((DEPTH - 1) * COND_ROWS + _mod_row(i), 0, 0))],
        out_specs=[pl.BlockSpec((TM, D), lambda i: (jnp.minimum(i, PROMPT_TILES - 1), 0)),
                   pl.BlockSpec((TM, D), lambda i: (jnp.maximum(i - PROMPT_TILES, 0), 0))],
        out_shape=[jax.ShapeDtypeStruct((T_PROMPT, D), F32), jax.ShapeDtypeStruct((T_SAMPLE, D), F32)],
        compiler_params=_cparams(("arbitrary",)),
        name="final_residual",
    )(x1, ys, pcol, mods3)


def _rope_tables():
    t = np.arange(SAMPLE_LEN)
    n_freq = QK_ROPE // 4
    inv_freq = ROPE_THETA ** (-np.arange(n_freq, dtype=np.float32) / n_freq)
    row = (t // GRID_W).astype(np.float32)
    col = (t % GRID_W).astype(np.float32)
    ang = jnp.concatenate([jnp.asarray(row)[:, None] * inv_freq, jnp.asarray(col)[:, None] * inv_freq],
                          axis=-1)
    cos_p = jnp.repeat(jnp.cos(ang), 2, axis=-1)
    sin_p = jnp.repeat(jnp.sin(ang), 2, axis=-1) * jnp.tile(jnp.asarray([-1.0, 1.0], F32), QK_ROPE // 2)
    pad_l = jnp.ones((SAMPLE_LEN, QK_NOPE), F32)
    pad_r = jnp.ones((SAMPLE_LEN, LANES - QK_HEAD), F32)
    cos = jnp.concatenate([pad_l, cos_p, pad_r], axis=-1)
    sin = jnp.concatenate([0 * pad_l, sin_p, 0 * pad_r], axis=-1)
    ident_c = jnp.ones((TM, LANES), F32)
    return jnp.concatenate([ident_c, cos], axis=0), jnp.concatenate([0 * ident_c, sin], axis=0)


def _pad_lanes(x, lo, total):
    pad = [(0, 0)] * (x.ndim - 1) + [(lo, total - lo - x.shape[-1])]
    return jnp.pad(x, pad)


def kernel(x_prompt, x_sample, cache_mla_ckv, cache_mla_krope, cache_na_k, cache_na_v, c, c_ctx, w_ada, b_ada, norm1_g, norm2_g, w_in, q_a_g, kv_a_g, w_q_up, w_kv_up, mla_q_norm_g, mla_k_norm_g, conv_dw_w, conv_dw_b, conv_ln_g, conv_ln_b, na_q_norm_g, na_k_norm_g, na_rpb, w_out, w_router, b_router, w_e_gate, w_e_up, w_e_down):
    cos, sin = _rope_tables()
    vec = lambda a: a.reshape(DEPTH, 1, a.shape[-1])
    prm = {
        "norm1_g": vec(norm1_g), "norm2_g": vec(norm2_g), "w_in": w_in, "q_a_g": vec(q_a_g),
        "kv_a_g": vec(kv_a_g), "w_q_up": w_q_up, "w_kv_up": w_kv_up,
        "q_norm_g": vec(_pad_lanes(mla_q_norm_g, 0, LANES)), "k_norm_g": vec(_pad_lanes(mla_k_norm_g, 0, LANES)),
        "na_q_g": vec(jnp.tile(na_q_norm_g, (1, 2))), "na_k_g": vec(jnp.tile(na_k_norm_g, (1, 2))),
        "cos": cos, "sin": sin,
        "conv_w": conv_dw_w, "conv_b": vec(conv_dw_b), "conv_ln_g": vec(conv_ln_g), "conv_ln_b": vec(conv_ln_b),
        "w_out": w_out, "w_router_t": w_router.T,
        "b_router_t": jnp.broadcast_to(b_router[:, None], (N_EXPERTS, TM)),
        "w_e_gate": w_e_gate, "w_e_up": w_e_up, "w_e_down": w_e_down,
    }

    cond = jnp.concatenate([c_ctx[None], c, jnp.zeros((COND_ROWS - N_COND, D), F32)], axis=0)
    mods3 = _modulation(cond.T, w_ada, b_ada).reshape(DEPTH * COND_ROWS, 1, MOD_W)

    kc, vc = _ctx_kv(cache_mla_ckv, cache_mla_krope, prm)
    bias = _na_bias(na_rpb)
    cache_k = cache_na_k.reshape(N_SAMPLE, DEPTH, PAST, NA_WIDTH)
    cache_v = cache_na_v.reshape(N_SAMPLE, DEPTH, PAST, NA_WIDTH)

    x = x1 = ys = pcol = None
    new_ckv, new_kr, new_nak, new_nav = [], [], [], []
    for layer in range(DEPTH):
        if layer == 0:
            outs = _pre(layer, False, (x_prompt.reshape(T_PROMPT, D), x_sample.reshape(T_SAMPLE, D)), mods3, prm)
        else:
            outs = _pre(layer, True, (x1, ys, pcol), mods3, prm)
        x, q, k, v, ckv_n, kr, u, nq, nk, nv, nkf, nvf = outs
        om_p, on_p = _prompt_attn(q, k, v, nq, nk, nv)
        om_s = _sample_mla(layer, q, k, v, kc, vc)
        on_s = _sample_na(layer, nq, nk, nv, cache_k, cache_v, bias)
        x1, xs, pcol, run_start, run_len = _post(layer, x, om_p, om_s, u, on_p, on_s, mods3, prm)
        ys = _experts(layer, run_start[:, :, 0].reshape(-1), run_len[:, :, 0].reshape(-1), xs, prm)
        new_ckv.append(ckv_n.reshape(N_PROMPT, PROMPT_LEN, KV_LORA))
        new_kr.append(kr)
        new_nak.append(nkf)
        new_nav.append(nvf)
    y_prompt, y_sample = _final(x1, ys, pcol, mods3)
    return (y_prompt.reshape(N_PROMPT, PROMPT_LEN, D), y_sample.reshape(N_SAMPLE, SAMPLE_LEN, D),
            jnp.stack(new_ckv, axis=1), jnp.stack(new_kr, axis=1).transpose(0, 1, 3, 2),
            jnp.stack(new_nak, axis=1).transpose(0, 1, 4, 2, 3), jnp.stack(new_nav, axis=1).transpose(0, 1, 4, 2, 3))
```

```python
import functools

import numpy as np
import jax
import jax.numpy as jnp
from jax import lax
from jax.experimental import pallas as pl
from jax.experimental.pallas import tpu as pltpu

F32 = jnp.float32
BF16 = jnp.bfloat16

D = 1024
N_PROMPT = 16
PROMPT_LEN = 256
N_SAMPLE = 4
SAMPLE_LEN = 1024
DEPTH = 4
PAST = 512
GRID_W = 64
GRID_ROWS = SAMPLE_LEN // GRID_W
MLA_HEADS = 8
QK_NOPE = 64
QK_ROPE = 32
QK_HEAD = QK_NOPE + QK_ROPE
V_HEAD = 64
Q_LORA = 256
KV_LORA = 128
CONV_CH = 256
CONV_K = 31
NA_HEADS = 4
NA_DIM = 64
NA_WIDTH = NA_HEADS * NA_DIM
NA_WIN_H = 8
NA_WIN_W = 16
N_EXPERTS = 16
N_GROUPS = 4
EPG = 4
EXPERT_FF = 512
ROPE_THETA = 10000.0
EPS = 1e-6
NEG_INF = -1e30
MLA_SCALE = QK_HEAD ** -0.5
NA_SCALE = NA_DIM ** -0.5

LANES = 128
SUBLANES = 8
TM = 256
T_PROMPT = N_PROMPT * PROMPT_LEN
T_SAMPLE = N_SAMPLE * SAMPLE_LEN
T_ALL = T_PROMPT + T_SAMPLE
N_TILES = T_ALL // TM
PROMPT_TILES = T_PROMPT // TM
TILES_PER_SAMPLE = SAMPLE_LEN // TM
N_COND = 1 + N_SAMPLE
COND_ROWS = 8
MOD_W = 6 * D

COL_Q = 0
COL_KV = Q_LORA
COL_KR = COL_KV + KV_LORA
COL_CONV = COL_KR + LANES
COL_NA = COL_CONV + 2 * CONV_CH
IN_COLS_P = COL_NA + 3 * NA_WIDTH
IN_COLS = IN_COLS_P - (LANES - QK_ROPE)

NA_KEY_ROWS = 12
NA_KEYS = NA_KEY_ROWS * GRID_W

VMEM_LIMIT = 56 * 1024 * 1024

RUN_ALIGN = 16
TILE_CAP = 768
RUN_SIZES = (256, 128, 64, 32, 16)
MOE_CHUNK = 256
XY_ROWS = N_TILES * TM
WAIT_SIZES = (8192, 4096, 2048, 1024, 512, 256, 128, 64, 32, 16)
EXPERT_VMEM_LIMIT = 60 * 1024 * 1024


def _cparams(sem):
    return pltpu.CompilerParams(dimension_semantics=sem, vmem_limit_bytes=VMEM_LIMIT)


def _mod_row(i):
    return jnp.where(i < PROMPT_TILES, 0, 1 + (i - PROMPT_TILES) // TILES_PER_SAMPLE)


def _rope_block(i):
    return jnp.where(i < PROMPT_TILES, 0, 1 + (i - PROMPT_TILES) % TILES_PER_SAMPLE)


def _dot(a, b):
    return jnp.dot(a, b, preferred_element_type=F32)


def _dot_t(a, b):
    return lax.dot_general(a, b, (((1,), (1,)), ((), ())), preferred_element_type=F32)


def _silu(x):
    return x * jax.nn.sigmoid(x)


def _lane_is_low(shape):
    return lax.broadcasted_iota(jnp.int32, shape, len(shape) - 1) < (LANES // 2)


MOD_BLK = 768


def _mod_kernel(ct_ref, w_ref, b_ref, o_ref, cb_ref):
    first = (pl.program_id(0) == 0) & (pl.program_id(1) == 0)

    @pl.when(first)
    def _():
        s = _silu(ct_ref[...])
        for m in range(N_COND):
            cb_ref[m] = jnp.broadcast_to(s[:, m:m + 1], (D, LANES))

    for j in range(MOD_BLK // LANES):
        w = w_ref[0, :, j * LANES:(j + 1) * LANES]
        rows = [jnp.sum(w * cb_ref[m], axis=0, keepdims=True) for m in range(N_COND)]
        rows.append(jnp.zeros((COND_ROWS - N_COND, LANES), F32))
        o_ref[0, :, j * LANES:(j + 1) * LANES] = (
            jnp.concatenate(rows, axis=0) + b_ref[0, :, j * LANES:(j + 1) * LANES])


def _modulation(cond_t, w_ada, b_ada):
    nb = MOD_W // MOD_BLK
    return pl.pallas_call(
        _mod_kernel,
        grid=(DEPTH, nb),
        in_specs=[
            pl.BlockSpec((D, COND_ROWS), lambda l, j: (0, 0)),
            pl.BlockSpec((1, D, MOD_BLK), lambda l, j: (l, 0, j)),
            pl.BlockSpec((1, 1, MOD_BLK), lambda l, j: (l, 0, j)),
        ],
        out_specs=pl.BlockSpec((1, COND_ROWS, MOD_BLK), lambda l, j: (l, 0, j)),
        out_shape=jax.ShapeDtypeStruct((DEPTH, COND_ROWS, MOD_W), F32),
        scratch_shapes=[pltpu.VMEM((N_COND, D, LANES), F32)],
        compiler_params=_cparams(("arbitrary", "arbitrary")),
        name="modulation",
    )(cond_t, w_ada, b_ada.reshape(DEPTH, 1, MOD_W))


def _lane_sums(sq, group):
    k = sq.shape[1]
    hi = sq.astype(BF16)
    lo = (sq - hi.astype(F32)).astype(BF16)
    if group == k:
        sel = jnp.ones((k, LANES), BF16)
    else:
        r = lax.broadcasted_iota(jnp.int32, (k, LANES), 0)
        c = lax.broadcasted_iota(jnp.int32, (k, LANES), 1)
        sel = ((r // group) == (c // group)).astype(BF16)
    return _dot(hi, sel) + _dot(lo, sel)


def _rms(x, width):
    rinv = lax.rsqrt(_lane_sums(x * x, x.shape[1]) * (1.0 / width) + EPS)
    reps = x.shape[1] // LANES
    return x * (rinv if reps == 1 else jnp.concatenate([rinv] * reps, axis=1))


def _rope(x, cos, sin):
    lane = lax.broadcasted_iota(jnp.int32, x.shape, 1)
    partner = jnp.where((lane & 1) == 0, pltpu.roll(x, LANES - 1, 1), pltpu.roll(x, 1, 1))
    return x * cos + partner * sin


def _pair_rms(x, gain):
    ms = _lane_sums(x * x, NA_DIM) * (1.0 / NA_DIM)
    return x * lax.rsqrt(ms + EPS) * gain


def _softmax_pv(scores, values):
    m = scores[0].max(axis=-1, keepdims=True)
    for s in scores[1:]:
        m = jnp.maximum(m, s.max(axis=-1, keepdims=True))
    acc = None
    den = None
    for s, v in zip(scores, values):
        p = jnp.exp(s - m)
        d = jnp.sum(p, axis=-1, keepdims=True)
        o = _dot(p.astype(BF16), v)
        acc = o if acc is None else acc + o
        den = d if den is None else den + d
    return acc * (1.0 / den)


def _moe_combine(x1_ref, ys_ref, pc_ref, modp_ref):
    pc = pc_ref[...]
    lane = lax.broadcasted_iota(jnp.int32, (TM, TILE_CAP), 1)
    sel = (jnp.where(lane == pc[:, 0:1].astype(jnp.int32), pc[:, 2:3], 0.0)
           + jnp.where(lane == pc[:, 1:2].astype(jnp.int32), pc[:, 3:4], 0.0))
    gate2 = modp_ref[0, :, 5 * D:6 * D]
    return x1_ref[...] + gate2 * _dot(sel.astype(BF16), ys_ref[...])


def _place_w_in(w_ref, w_bf):
    off_kv = Q_LORA + KV_LORA
    off_kr = off_kv + QK_ROPE
    w_bf[:, 0:off_kv] = w_ref[0, :, 0:off_kv].astype(BF16)
    w_bf[:, COL_KR:COL_KR + LANES] = jnp.zeros((D, LANES), BF16)
    w_bf[:, COL_KR + QK_NOPE:COL_KR + QK_HEAD] = w_ref[0, :, off_kv:off_kr].astype(BF16)
    w_bf[:, COL_CONV:IN_COLS_P] = w_ref[0, :, off_kr:off_kr + (IN_COLS_P - COL_CONV)].astype(BF16)


def _place_w_q(w_ref, w_bf):
    w_bf[...] = jnp.zeros(w_bf.shape, BF16)
    for hd in range(MLA_HEADS):
        w_bf[:, hd * LANES:hd * LANES + QK_HEAD] = w_ref[0, :, hd * QK_HEAD:(hd + 1) * QK_HEAD].astype(BF16)


def _place_w_kv(w_ref, wk_bf, wv_bf):
    wk_bf[...] = jnp.zeros(wk_bf.shape, BF16)
    for hd in range(MLA_HEADS):
        c0 = hd * (QK_NOPE + V_HEAD)
        wk_bf[:, hd * LANES:hd * LANES + QK_NOPE] = w_ref[0, :, c0:c0 + QK_NOPE].astype(BF16)
        wv_bf[:, hd * V_HEAD:(hd + 1) * V_HEAD] = w_ref[0, :, c0 + QK_NOPE:c0 + QK_NOPE + V_HEAD].astype(BF16)


def _pre_kernel(has_moe, *refs):
    if has_moe:
        x1_ref, ys_ref, pc_ref, modp_ref = refs[:4]
        refs = refs[4:]
    else:
        xp_ref, xs_ref = refs[:2]
        refs = refs[2:]
    (mod_ref, n1g_ref, win_ref, qag_ref, kvag_ref, wq_ref, wkv_ref, qng_ref, kng_ref,
     naqg_ref, nakg_ref, cos_ref, sin_ref) = refs[:13]
    refs = refs[13:]
    xo_ref = refs[0]
    refs = refs[1:]
    (q_out, k_out, v_out, ckv_out, kr_out, u_out, naq_out, nak_out, nav_out, nakf_out, navf_out,
     win_bf, wq_bf, wk_bf, wv_bf, proj_buf) = refs

    s = pl.program_id(0)
    i = jnp.minimum(s, N_TILES - 1)

    @pl.when(s == 0)
    def _():
        _place_w_in(win_ref, win_bf)
        _place_w_q(wq_ref, wq_bf)
        _place_w_kv(wkv_ref, wk_bf, wv_bf)
        proj_buf[...] = jnp.zeros(proj_buf.shape, F32)

    if has_moe:
        x = _moe_combine(x1_ref, ys_ref, pc_ref, modp_ref)
    else:
        x = jnp.where(i < PROMPT_TILES, xp_ref[...], xs_ref[...])
    xo_ref[...] = x
    shift = mod_ref[0, :, 0:D]
    scale = mod_ref[0, :, D:2 * D]
    h = _rms(x, D) * n1g_ref[0] * (1.0 + scale) + shift
    proj = proj_buf[(s + 1) % 2]
    proj_buf[s % 2] = _dot(h.astype(BF16), win_bf[...])

    cos = cos_ref[...]
    sin = sin_ref[...]

    q_lat = _rms(proj[:, COL_Q:COL_Q + Q_LORA], Q_LORA) * qag_ref[0]
    q = _dot(q_lat.astype(BF16), wq_bf[...])
    for hd in range(MLA_HEADS):
        qh = q[:, hd * LANES:(hd + 1) * LANES]
        qh = _rope(_rms(qh, QK_HEAD) * qng_ref[0], cos, sin)
        q_out[hd] = (qh * MLA_SCALE).astype(BF16)

    ckv_n = _rms(proj[:, COL_KV:COL_KV + KV_LORA], KV_LORA) * kvag_ref[0]
    kr = proj[:, COL_KR:COL_KR + LANES]
    ckv_b = ckv_n.astype(BF16)
    k_nope = _dot(ckv_b, wk_bf[...])
    kr_rot = _rope(kr * kng_ref[0], cos, sin)
    kr_sq = kr * kr
    for hd in range(MLA_HEADS):
        kn = k_nope[:, hd * LANES:(hd + 1) * LANES]
        rinv = lax.rsqrt(_lane_sums(kn * kn + kr_sq, LANES) * (1.0 / QK_HEAD) + EPS)
        k_out[hd] = ((kn * kng_ref[0] + kr_rot) * rinv).astype(BF16)
    v = _dot(ckv_b, wv_bf[...])
    for p in range(MLA_HEADS // 2):
        v_out[p] = v[:, p * LANES:(p + 1) * LANES].astype(BF16)

    a = proj[:, COL_CONV:COL_CONV + CONV_CH]
    g = proj[:, COL_CONV + CONV_CH:COL_CONV + 2 * CONV_CH]
    u_out[...] = a * jax.nn.sigmoid(g)

    k_slabs, v_slabs = [], []
    for p in range(NA_HEADS // 2):
        sl = slice(p * LANES, (p + 1) * LANES)
        qn = _pair_rms(proj[:, COL_NA + p * LANES:COL_NA + (p + 1) * LANES], naqg_ref[0])
        kn = _pair_rms(proj[:, COL_NA + NA_WIDTH + p * LANES:COL_NA + NA_WIDTH + (p + 1) * LANES],
                       nakg_ref[0])
        vn = proj[:, COL_NA + 2 * NA_WIDTH + p * LANES:COL_NA + 2 * NA_WIDTH + (p + 1) * LANES]
        naq_out[:, sl] = (qn * NA_SCALE).astype(BF16)
        nak_out[:, sl] = kn.astype(BF16)
        nav_out[:, sl] = vn.astype(BF16)
        k_slabs.append(kn)
        v_slabs.append(vn)

    kr_t = kr.T
    nak_t = jnp.concatenate(k_slabs, axis=1).T.reshape(NA_HEADS, NA_DIM, TM)
    nav_t = jnp.concatenate(v_slabs, axis=1).T.reshape(NA_HEADS, NA_DIM, TM)

    @pl.when(s <= PROMPT_TILES)
    def _():
        ckv_out[...] = ckv_n
        kr_out[0] = kr_t[QK_NOPE:QK_HEAD, :]
        nakf_out[0] = nak_t
        navf_out[0] = nav_t


def _pre(layer, has_moe, x_or_parts, mods3, prm):
    cur = lambda s: jnp.minimum(s, N_TILES - 1)
    prev = lambda s: jnp.maximum(s - 1, 0)
    row = lambda s: (cur(s), 0)
    row2 = lambda s: (prev(s), 0)
    lay3 = lambda s: (layer, 0, 0)
    in_specs = []
    args = []
    if has_moe:
        x1, ys, pcol = x_or_parts
        in_specs += [pl.BlockSpec((TM, D), row), pl.BlockSpec((TILE_CAP, D), row),
                     pl.BlockSpec((TM, LANES), row),
                     pl.BlockSpec((1, 1, MOD_W), lambda s: ((layer - 1) * COND_ROWS + _mod_row(cur(s)), 0, 0))]
        args += [x1, ys, pcol, mods3]
    else:
        x_prompt, x_sample = x_or_parts
        in_specs += [pl.BlockSpec((TM, D), lambda s: (jnp.minimum(s, PROMPT_TILES - 1), 0)),
                     pl.BlockSpec((TM, D), lambda s: (jnp.maximum(cur(s) - PROMPT_TILES, 0), 0))]
        args += [x_prompt, x_sample]
    in_specs += [
        pl.BlockSpec((1, 1, MOD_W), lambda s: (layer * COND_ROWS + _mod_row(cur(s)), 0, 0)),
        pl.BlockSpec((1, 1, D), lay3),
        pl.BlockSpec((1, D, IN_COLS), lay3),
        pl.BlockSpec((1, 1, Q_LORA), lay3),
        pl.BlockSpec((1, 1, KV_LORA), lay3),
        pl.BlockSpec((1, Q_LORA, MLA_HEADS * QK_HEAD), lay3),
        pl.BlockSpec((1, KV_LORA, MLA_HEADS * (QK_NOPE + V_HEAD)), lay3),
        pl.BlockSpec((1, 1, LANES), lay3),
        pl.BlockSpec((1, 1, LANES), lay3),
        pl.BlockSpec((1, 1, LANES), lay3),
        pl.BlockSpec((1, 1, LANES), lay3),
        pl.BlockSpec((TM, LANES), lambda s: (_rope_block(prev(s)), 0)),
        pl.BlockSpec((TM, LANES), lambda s: (_rope_block(prev(s)), 0)),
    ]
    args += [mods3, prm["norm1_g"], prm["w_in"], prm["q_a_g"], prm["kv_a_g"], prm["w_q_up"], prm["w_kv_up"],
             prm["q_norm_g"], prm["k_norm_g"], prm["na_q_g"], prm["na_k_g"], prm["cos"], prm["sin"]]
    out_specs = [pl.BlockSpec((TM, D), row)]
    out_shape = [jax.ShapeDtypeStruct((T_ALL, D), F32)]
    head3 = lambda n: pl.BlockSpec((n, TM, LANES), lambda s: (0, prev(s), 0))
    seq = lambda s: jnp.minimum(prev(s), PROMPT_TILES - 1)
    out_specs += [head3(MLA_HEADS), head3(MLA_HEADS), head3(MLA_HEADS // 2),
                  pl.BlockSpec((TM, KV_LORA), lambda s: (seq(s), 0)),
                  pl.BlockSpec((1, QK_ROPE, TM), lambda s: (seq(s), 0, 0)),
                  pl.BlockSpec((TM, CONV_CH), row2)] + [pl.BlockSpec((TM, NA_WIDTH), row2)] * 3
    out_specs += [pl.BlockSpec((1, NA_HEADS, NA_DIM, TM), lambda s: (seq(s), 0, 0, 0))] * 2
    out_shape += [
        jax.ShapeDtypeStruct((MLA_HEADS, T_ALL, LANES), BF16),
        jax.ShapeDtypeStruct((MLA_HEADS, T_ALL, LANES), BF16),
        jax.ShapeDtypeStruct((MLA_HEADS // 2, T_ALL, LANES), BF16),
        jax.ShapeDtypeStruct((T_PROMPT, KV_LORA), F32),
        jax.ShapeDtypeStruct((N_PROMPT, QK_ROPE, PROMPT_LEN), F32),
        jax.ShapeDtypeStruct((T_ALL, CONV_CH), F32),
        jax.ShapeDtypeStruct((T_ALL, NA_WIDTH), BF16),
        jax.ShapeDtypeStruct((T_ALL, NA_WIDTH), BF16),
        jax.ShapeDtypeStruct((T_ALL, NA_WIDTH), BF16),
        jax.ShapeDtypeStruct((N_PROMPT, NA_HEADS, NA_DIM, PROMPT_LEN), F32),
        jax.ShapeDtypeStruct((N_PROMPT, NA_HEADS, NA_DIM, PROMPT_LEN), F32),
    ]
    return pl.pallas_call(
        functools.partial(_pre_kernel, has_moe),
        grid=(N_TILES + 1,),
        in_specs=in_specs,
        out_specs=out_specs,
        out_shape=out_shape,
        scratch_shapes=[pltpu.VMEM((D, IN_COLS_P), BF16), pltpu.VMEM((Q_LORA, MLA_HEADS * LANES), BF16),
                        pltpu.VMEM((KV_LORA, MLA_HEADS * LANES), BF16),
                        pltpu.VMEM((KV_LORA, MLA_HEADS * V_HEAD), BF16),
                        pltpu.VMEM((2, TM, IN_COLS_P), F32)],
        compiler_params=_cparams(("arbitrary",)),
        name="pre_l%d" % layer,
    )(*args)


def _ctx_kernel(ckv_ref, kr_ref, wkv_ref, kng_ref, kc_out, vc_out, wk_bf, wv_bf):
    @pl.when(pl.program_id(1) == 0)
    def _():
        _place_w_kv(wkv_ref, wk_bf, wv_bf)

    ckv = ckv_ref[0, 0].astype(BF16)
    kr = jnp.concatenate([jnp.zeros((PAST, QK_NOPE), F32), kr_ref[0, 0],
                          jnp.zeros((PAST, LANES - QK_HEAD), F32)], axis=1)
    k_nope = _dot(ckv, wk_bf[...])
    for hd in range(MLA_HEADS):
        kh = k_nope[:, hd * LANES:(hd + 1) * LANES] + kr
        kc_out[0, 0, hd] = (_rms(kh, QK_HEAD) * kng_ref[0]).astype(BF16)
    v = _dot(ckv, wv_bf[...])
    for p in range(MLA_HEADS // 2):
        vc_out[0, 0, p] = v[:, p * LANES:(p + 1) * LANES].astype(BF16)


def _ctx_kv(cache_ckv, cache_krope, prm):
    return pl.pallas_call(
        _ctx_kernel,
        grid=(DEPTH, N_SAMPLE),
        in_specs=[
            pl.BlockSpec((1, 1, PAST, KV_LORA), lambda l, b: (b, l, 0, 0)),
            pl.BlockSpec((1, 1, PAST, QK_ROPE), lambda l, b: (b, l, 0, 0)),
            pl.BlockSpec((1, KV_LORA, MLA_HEADS * (QK_NOPE + V_HEAD)), lambda l, b: (l, 0, 0)),
            pl.BlockSpec((1, 1, LANES), lambda l, b: (l, 0, 0)),
        ],
        out_specs=[
            pl.BlockSpec((1, 1, MLA_HEADS, PAST, LANES), lambda l, b: (l, b, 0, 0, 0)),
            pl.BlockSpec((1, 1, MLA_HEADS // 2, PAST, LANES), lambda l, b: (l, b, 0, 0, 0)),
        ],
        out_shape=[
            jax.ShapeDtypeStruct((DEPTH, N_SAMPLE, MLA_HEADS, PAST, LANES), BF16),
            jax.ShapeDtypeStruct((DEPTH, N_SAMPLE, MLA_HEADS // 2, PAST, LANES), BF16),
        ],
        scratch_shapes=[pltpu.VMEM((KV_LORA, MLA_HEADS * LANES), BF16),
                        pltpu.VMEM((KV_LORA, MLA_HEADS * V_HEAD), BF16)],
        compiler_params=_cparams(("arbitrary", "arbitrary")),
        name="ctx_kv",
    )(cache_ckv, cache_krope, prm["w_kv_up"], prm["k_norm_g"])


N_DR = 2 * NA_WIN_H - 1
N_DC = 2 * NA_WIN_W - 1


def _bias_kernel(rpb_ref, o_ref, blk_ref):
    cq = lax.broadcasted_iota(jnp.int32, (GRID_W, GRID_W), 0)
    ck = lax.broadcasted_iota(jnp.int32, (GRID_W, GRID_W), 1)
    dc = jnp.clip(ck - cq + (NA_WIN_W - 1), 0, N_DC - 1)
    c0 = jnp.clip(cq - NA_WIN_W // 2, 0, GRID_W - NA_WIN_W)
    in_win = (ck >= c0) & (ck < c0 + NA_WIN_W)
    for dr in range(N_DR):
        acc = jnp.zeros((GRID_W, GRID_W), F32)
        for j in range(N_DC):
            acc = jnp.where(dc == j, rpb_ref[pl.program_id(0), pl.program_id(1), dr, j], acc)
        blk_ref[dr] = jnp.where(in_win, acc, NEG_INF)
    blk_ref[N_DR] = jnp.full((GRID_W, GRID_W), NEG_INF, F32)
    idx = _na_block_index()
    for i in range(TILES_PER_SAMPLE):
        for rq in range(TM // GRID_W):
            for pp in range(NA_KEY_ROWS // 2):
                pair = jnp.concatenate([blk_ref[int(idx[i, rq, 2 * pp])], blk_ref[int(idx[i, rq, 2 * pp + 1])]],
                                       axis=1)
                o_ref[0, 0, i, rq * GRID_W:(rq + 1) * GRID_W, pp * LANES:(pp + 1) * LANES] = pair


def _na_bias(na_rpb):
    return pl.pallas_call(
        _bias_kernel,
        grid=(DEPTH, NA_HEADS),
        in_specs=[pl.BlockSpec(memory_space=pltpu.SMEM)],
        out_specs=pl.BlockSpec((1, 1, TILES_PER_SAMPLE, TM, NA_KEYS), lambda l, h: (l, h, 0, 0, 0)),
        out_shape=jax.ShapeDtypeStruct((DEPTH, NA_HEADS, TILES_PER_SAMPLE, TM, NA_KEYS), F32),
        scratch_shapes=[pltpu.VMEM((N_DR + 1, GRID_W, GRID_W), F32)],
        compiler_params=_cparams(("arbitrary", "arbitrary")),
        name="na_bias",
    )(na_rpb)


def _na_block_index():
    idx = np.full((TILES_PER_SAMPLE, TM // GRID_W, NA_KEY_ROWS), N_DR, np.int32)
    for i in range(TILES_PER_SAMPLE):
        ks = _na_key_start_row(i)
        for rq in range(TM // GRID_W):
            r = i * (TM // GRID_W) + rq
            r0 = min(max(r - NA_WIN_H // 2, 0), GRID_ROWS - NA_WIN_H)
            for rk in range(NA_KEY_ROWS):
                kr = ks + rk
                if r0 <= kr < r0 + NA_WIN_H:
                    idx[i, rq, rk] = kr - r + (NA_WIN_H - 1)
    return idx


def _na_key_start_row(i):
    return 0 if i < TILES_PER_SAMPLE // 2 else GRID_ROWS - NA_KEY_ROWS


def _prompt_attn_kernel(q_ref, k_ref, v_ref, nq_ref, nk_ref, nv_ref, om_ref, on_ref):
    low = _lane_is_low((PROMPT_LEN, LANES))
    for p in range(MLA_HEADS // 2):
        outs = []
        for hh in range(2):
            hd = 2 * p + hh
            s = _dot_t(q_ref[hd], k_ref[hd])
            outs.append(_softmax_pv([s], [v_ref[p]]))
        om_ref[p] = jnp.where(low, outs[0], outs[1]).astype(BF16)
    for p in range(NA_HEADS // 2):
        sl = slice(p * LANES, (p + 1) * LANES)
        qp = nq_ref[:, sl]
        kp = nk_ref[:, sl]
        vp = nv_ref[:, sl]
        outs = []
        for hh in range(2):
            qm = jnp.where(low if hh == 0 else ~low, qp, jnp.zeros_like(qp))
            outs.append(_softmax_pv([_dot_t(qm, kp)], [vp]))
        on_ref[:, sl] = jnp.where(low, outs[0], outs[1]).astype(BF16)


def _prompt_attn(q, k, v, nq, nk, nv):
    head3 = lambda n: pl.BlockSpec((n, PROMPT_LEN, LANES), lambda b: (0, b, 0))
    row = pl.BlockSpec((PROMPT_LEN, NA_WIDTH), lambda b: (b, 0))
    return pl.pallas_call(
        _prompt_attn_kernel,
        grid=(N_PROMPT,),
        in_specs=[head3(MLA_HEADS), head3(MLA_HEADS), head3(MLA_HEADS // 2), row, row, row],
        out_specs=[head3(MLA_HEADS // 2), row],
        out_shape=[jax.ShapeDtypeStruct((MLA_HEADS // 2, T_PROMPT, LANES), BF16),
                   jax.ShapeDtypeStruct((T_PROMPT, NA_WIDTH), BF16)],
        compiler_params=_cparams(("arbitrary",)),
        name="prompt_attn",
    )(q, k, v, nq, nk, nv)


def _sample_mla_kernel(q_ref, k_ref, v_ref, kc_ref, vc_ref, o_ref):
    low = _lane_is_low((TM, LANES))
    for t in range(TILES_PER_SAMPLE):
        rows = slice(t * TM, (t + 1) * TM)
        outs = []
        for hh in range(2):
            q = q_ref[hh, rows, :]
            s_lat = _dot_t(q, k_ref[hh])
            s_ctx = _dot_t(q, kc_ref[0, 0, hh])
            outs.append(_softmax_pv([s_lat, s_ctx], [v_ref[0], vc_ref[0, 0, 0]]))
        o_ref[0, rows, :] = jnp.where(low, outs[0], outs[1]).astype(BF16)


def _sample_mla(layer, q, k, v, kc, vc):
    seq_blk0 = T_PROMPT // SAMPLE_LEN
    return pl.pallas_call(
        _sample_mla_kernel,
        grid=(N_SAMPLE, MLA_HEADS // 2),
        in_specs=[
            pl.BlockSpec((2, SAMPLE_LEN, LANES), lambda b, p: (p, seq_blk0 + b, 0)),
            pl.BlockSpec((2, SAMPLE_LEN, LANES), lambda b, p: (p, seq_blk0 + b, 0)),
            pl.BlockSpec((1, SAMPLE_LEN, LANES), lambda b, p: (p, seq_blk0 + b, 0)),
            pl.BlockSpec((1, 1, 2, PAST, LANES), lambda b, p: (layer, b, p, 0, 0)),
            pl.BlockSpec((1, 1, 1, PAST, LANES), lambda b, p: (layer, b, p, 0, 0)),
        ],
        out_specs=pl.BlockSpec((1, SAMPLE_LEN, LANES), lambda b, p: (p, b, 0)),
        out_shape=jax.ShapeDtypeStruct((MLA_HEADS // 2, T_SAMPLE, LANES), BF16),
        compiler_params=_cparams(("arbitrary", "arbitrary")),
        name="sample_mla",
    )(q, k, v, kc, vc)


def _sample_na_kernel(q_ref, k_ref, v_ref, kc_ref, vc_ref, b_ref, o_ref):
    kc = kc_ref[0, 0].astype(BF16)
    vc = vc_ref[0, 0].astype(BF16)
    low = _lane_is_low((TM, LANES))
    for t in range(TILES_PER_SAMPLE):
        rows = slice(t * TM, (t + 1) * TM)
        start = _na_key_start_row(t) * GRID_W
        k_win = k_ref[start:start + NA_KEYS, :]
        v_win = v_ref[start:start + NA_KEYS, :]
        q = q_ref[rows, :]
        outs = []
        for hh in range(2):
            qm = jnp.where(low if hh == 0 else ~low, q, jnp.zeros_like(q))
            s_loc = _dot_t(qm, k_win) + b_ref[0, hh, t]
            s_ctx = _dot_t(qm, kc)
            outs.append(_softmax_pv([s_loc, s_ctx], [v_win, vc]))
        o_ref[rows, :] = jnp.where(low, outs[0], outs[1]).astype(BF16)


def _sample_na(layer, nq, nk, nv, cache_k, cache_v, bias):
    seq_blk0 = T_PROMPT // SAMPLE_LEN
    return pl.pallas_call(
        _sample_na_kernel,
        grid=(NA_HEADS // 2, N_SAMPLE),
        in_specs=[
            pl.BlockSpec((SAMPLE_LEN, LANES), lambda p, b: (seq_blk0 + b, p)),
            pl.BlockSpec((SAMPLE_LEN, LANES), lambda p, b: (seq_blk0 + b, p)),
            pl.BlockSpec((SAMPLE_LEN, LANES), lambda p, b: (seq_blk0 + b, p)),
            pl.BlockSpec((1, 1, PAST, LANES), lambda p, b: (b, layer, 0, p)),
            pl.BlockSpec((1, 1, PAST, LANES), lambda p, b: (b, layer, 0, p)),
            pl.BlockSpec((1, 2, TILES_PER_SAMPLE, TM, NA_KEYS), lambda p, b: (layer, p, 0, 0, 0)),
        ],
        out_specs=pl.BlockSpec((SAMPLE_LEN, LANES), lambda p, b: (b, p)),
        out_shape=jax.ShapeDtypeStruct((T_SAMPLE, NA_WIDTH), BF16),
        compiler_params=_cparams(("arbitrary", "arbitrary")),
        name="sample_na",
    )(nq, nk, nv, cache_k, cache_v, bias)


CONV_PAD = 16


def _argmax_first(cols):
    best = cols[0]
    idx = jnp.zeros(best.shape, jnp.int32)
    for j in range(1, len(cols)):
        upd = cols[j] > best
        idx = jnp.where(upd, j, idx)
        best = jnp.where(upd, cols[j], best)
    return idx


def _route(aff, bias):
    sel = aff + bias
    rows = [sel[e:e + 1, :] for e in range(N_EXPERTS)]
    group_scores = []
    for g in range(N_GROUPS):
        c = rows[g * EPG:(g + 1) * EPG]
        best = None
        for a in range(EPG):
            for b in range(a + 1, EPG):
                s = c[a] + c[b]
                best = s if best is None else jnp.maximum(best, s)
        group_scores.append(best)
    g_idx = _argmax_first(group_scores)
    in_group = []
    for j in range(EPG):
        v = rows[j]
        for g in range(1, N_GROUPS):
            v = jnp.where(g_idx == g, rows[g * EPG + j], v)
        in_group.append(v)
    i1 = _argmax_first(in_group)
    masked = [jnp.where(i1 == j, -jnp.inf, in_group[j]) for j in range(EPG)]
    i2 = _argmax_first(masked)
    e1 = g_idx * EPG + i1
    e2 = g_idx * EPG + i2
    sub = lax.broadcasted_iota(jnp.int32, aff.shape, 0)
    w1 = jnp.sum(jnp.where(sub == e1, aff, 0.0), axis=0, keepdims=True)
    w2 = jnp.sum(jnp.where(sub == e2, aff, 0.0), axis=0, keepdims=True)
    tot = w1 + w2
    return e1, e2, w1 / tot, w2 / tot


def _sort_tile(e1, e2):
    sub = lax.broadcasted_iota(jnp.int32, (N_EXPERTS, TM), 0)
    oh1 = sub == e1
    oh2 = sub == e2
    one = jnp.ones((N_EXPERTS, TM), F32)
    cnt1 = jnp.sum(jnp.where(oh1, one, 0.0), axis=1, keepdims=True) * one
    cnt2 = jnp.sum(jnp.where(oh2, one, 0.0), axis=1, keepdims=True) * one
    tot = (cnt1 + cnt2).astype(jnp.int32)
    padded = (tot + (RUN_ALIGN - 1)) & (-RUN_ALIGN)
    ee = lax.broadcasted_iota(jnp.int32, (N_EXPERTS, N_EXPERTS), 0)
    ef = lax.broadcasted_iota(jnp.int32, (N_EXPERTS, N_EXPERTS), 1)
    start = _dot((ef < ee).astype(BF16), padded.astype(BF16))
    ta = lax.broadcasted_iota(jnp.int32, (TM, TM), 0)
    tb = lax.broadcasted_iota(jnp.int32, (TM, TM), 1)
    before = (ta < tb).astype(BF16)
    rank1 = _dot(oh1.astype(BF16), before)
    rank2 = _dot(oh2.astype(BF16), before)
    pos1 = jnp.sum(jnp.where(oh1, start + rank1, 0.0), axis=0, keepdims=True)
    pos2 = jnp.sum(jnp.where(oh2, start + cnt1 + rank2, 0.0), axis=0, keepdims=True)
    return pos1.astype(jnp.int32), pos2.astype(jnp.int32), start.astype(jnp.int32), padded


def _post_kernel(x_ref, omp_ref, oms_ref, up_ref, uc_ref, un_ref, onp_ref, ons_ref, mod_ref, cw_ref, cb_ref,
                 lg_ref, lb_ref, wo_ref, n2g_ref, wr_ref, br_ref, x1_out, xs_out, pc_out, st_out, pd_out,
                 wo_bf, ext_ref, shift_ref, h2_buf, logit_buf):
    s = pl.program_id(0)
    i = jnp.minimum(s, N_TILES - 1)

    @pl.when(s == 0)
    def _():
        wo_bf[...] = wo_ref[0].astype(BF16)
        h2_buf[...] = jnp.zeros(h2_buf.shape, BF16)
        logit_buf[...] = jnp.zeros(logit_buf.shape, F32)

    pos_in_seq = (i - PROMPT_TILES) % TILES_PER_SAMPLE
    is_first = (i < PROMPT_TILES) | (pos_in_seq == 0)
    is_last = (i < PROMPT_TILES) | (pos_in_seq == TILES_PER_SAMPLE - 1)
    ext_ref[0:CONV_PAD, :] = jnp.where(is_first, 0.0, up_ref[TM - CONV_PAD:TM, :])
    ext_ref[CONV_PAD:CONV_PAD + TM, :] = uc_ref[...]
    ext_ref[CONV_PAD + TM:2 * CONV_PAD + TM, :] = jnp.where(is_last, 0.0, un_ref[0:CONV_PAD, :])
    acc = None
    first_off = CONV_PAD - CONV_K // 2
    for b in range(SUBLANES):
        taps = [j for j in range(CONV_K) if (first_off + j) % SUBLANES == b]
        if not taps:
            continue
        reach = max(first_off + j - b for j in taps)
        if b:
            shift_ref[b, 0:TM + reach, :] = ext_ref[b:b + TM + reach, :]
        for j in taps:
            a8 = first_off + j - b
            window = shift_ref[b, a8:a8 + TM, :] if b else ext_ref[a8:a8 + TM, :]
            term = window * cw_ref[0, j:j + 1, :]
            acc = term if acc is None else acc + term
    conv = acc + cb_ref[0]
    wide = lambda t: jnp.concatenate([t] * (CONV_CH // LANES), axis=1)
    cc = conv - wide(_lane_sums(conv, CONV_CH) * (1.0 / CONV_CH))
    var = _lane_sums(cc * cc, CONV_CH) * (1.0 / CONV_CH)
    o_conv = _silu(cc * wide(lax.rsqrt(var + EPS)) * lg_ref[0] + lb_ref[0])

    is_prompt = i < PROMPT_TILES
    mix = jnp.concatenate([jnp.where(is_prompt, omp_ref[p], oms_ref[p]) for p in range(MLA_HEADS // 2)]
                          + [o_conv.astype(BF16), jnp.where(is_prompt, onp_ref[...], ons_ref[...])],
                          axis=-1)
    att = _dot(mix, wo_bf[...])
    gate1 = mod_ref[0, :, 2 * D:3 * D]
    x1 = x_ref[...] + gate1 * att
    x1_out[...] = x1
    shift2 = mod_ref[0, :, 3 * D:4 * D]
    scale2 = mod_ref[0, :, 4 * D:5 * D]
    h2 = _rms(x1, D) * n2g_ref[0] * (1.0 + scale2) + shift2

    logits = lax.dot_general(wr_ref[...], h2, (((1,), (1,)), ((), ())), preferred_element_type=F32,
                             precision=lax.Precision.HIGHEST)
    h2_prev = h2_buf[(s + 1) % 2]
    logits_prev = logit_buf[(s + 1) % 2]
    h2_buf[s % 2] = h2.astype(BF16)
    logit_buf[s % 2] = logits

    e1, e2, w1, w2 = _route(jax.nn.sigmoid(logits_prev), br_ref[...])
    pos1, pos2, start, padded = _sort_tile(e1, e2)
    st_out[0] = start[:, 0:LANES]
    pd_out[0] = padded[:, 0:LANES]
    r = lax.broadcasted_iota(jnp.int32, (TILE_CAP, TM), 0)
    perm = ((r == pos1) | (r == pos2)).astype(BF16)
    xs_out[...] = _dot(perm, h2_prev).astype(BF16)
    sub = lax.broadcasted_iota(jnp.int32, (LANES, TM), 0)
    packed = jnp.where(sub == 0, pos1.astype(F32),
                       jnp.where(sub == 1, pos2.astype(F32),
                                 jnp.where(sub == 2, w1, jnp.where(sub == 3, w2, 0.0))))
    pc_out[...] = packed.T


def _post(layer, x, om_p, om_s, u, on_p, on_s, mods3, prm):
    cur = lambda s: jnp.minimum(s, N_TILES - 1)
    prev = lambda s: jnp.maximum(s - 1, 0)
    row = lambda s: (cur(s), 0)
    row2 = lambda s: (prev(s), 0)
    lay3 = lambda s: (layer, 0, 0)
    return pl.pallas_call(
        _post_kernel,
        grid=(N_TILES + 1,),
        in_specs=[
            pl.BlockSpec((TM, D), row),
            pl.BlockSpec((MLA_HEADS // 2, TM, LANES), lambda s: (0, jnp.minimum(s, PROMPT_TILES - 1), 0)),
            pl.BlockSpec((MLA_HEADS // 2, TM, LANES), lambda s: (0, jnp.maximum(cur(s) - PROMPT_TILES, 0), 0)),
            pl.BlockSpec((TM, CONV_CH), lambda s: (jnp.maximum(cur(s) - 1, 0), 0)),
            pl.BlockSpec((TM, CONV_CH), row),
            pl.BlockSpec((TM, CONV_CH), lambda s: (jnp.minimum(s + 1, N_TILES - 1), 0)),
            pl.BlockSpec((TM, NA_WIDTH), lambda s: (jnp.minimum(s, PROMPT_TILES - 1), 0)),
            pl.BlockSpec((TM, NA_WIDTH), lambda s: (jnp.maximum(cur(s) - PROMPT_TILES, 0), 0)),
            pl.BlockSpec((1, 1, MOD_W), lambda s: (layer * COND_ROWS + _mod_row(cur(s)), 0, 0)),
            pl.BlockSpec((1, CONV_K, CONV_CH), lay3),
            pl.BlockSpec((1, 1, CONV_CH), lay3),
            pl.BlockSpec((1, 1, CONV_CH), lay3),
            pl.BlockSpec((1, 1, CONV_CH), lay3),
            pl.BlockSpec((1, D, D), lay3),
            pl.BlockSpec((1, 1, D), lay3),
            pl.BlockSpec((N_EXPERTS, D), lambda s: (0, 0)),
            pl.BlockSpec((N_EXPERTS, TM), lambda s: (0, 0)),
        ],
        out_specs=[pl.BlockSpec((TM, D), row), pl.BlockSpec((TILE_CAP, D), row2),
                   pl.BlockSpec((TM, LANES), row2),
                   pl.BlockSpec((1, N_EXPERTS, LANES), lambda s: (prev(s), 0, 0)),
                   pl.BlockSpec((1, N_EXPERTS, LANES), lambda s: (prev(s), 0, 0))],
        out_shape=[jax.ShapeDtypeStruct((T_ALL, D), F32),
                   jax.ShapeDtypeStruct((N_TILES * TILE_CAP, D), BF16),
                   jax.ShapeDtypeStruct((T_ALL, LANES), F32),
                   jax.ShapeDtypeStruct((N_TILES, N_EXPERTS, LANES), jnp.int32),
                   jax.ShapeDtypeStruct((N_TILES, N_EXPERTS, LANES), jnp.int32)],
        scratch_shapes=[pltpu.VMEM((D, D), BF16), pltpu.VMEM((TM + 2 * CONV_PAD, CONV_CH), F32),
                        pltpu.VMEM((SUBLANES, TM + 2 * CONV_PAD, CONV_CH), F32),
                        pltpu.VMEM((2, TM, D), BF16), pltpu.VMEM((2, N_EXPERTS, TM), F32)],
        compiler_params=_cparams(("arbitrary",)),
        name="post_l%d" % layer,
    )(x, om_p, om_s, u, u, u, on_p, on_s, mods3, prm["conv_w"], prm["conv_b"], prm["conv_ln_g"], prm["conv_ln_b"],
      prm["w_out"], prm["norm2_g"], prm["w_router_t"], prm["b_router_t"])


def _for_each_piece(n, fn):
    def pieces(sizes):
        for sz in sizes:
            @pl.when((n & sz) != 0)
            def _():
                fn(n & (-2 * sz), sz)

    n_big = len(RUN_SIZES) - 2

    @pl.when(n >= RUN_SIZES[n_big - 1])
    def _():
        pieces(RUN_SIZES[:n_big])

    pieces(RUN_SIZES[n_big:])


def _expert_kernel(st_ref, pd_ref, xs_hbm, wg_ref, wu_ref, wd_ref, ys_hbm, xy, wg_bf, wu_bf, wd_bf, zeros_buf,
                   sems):
    e = pl.program_id(0)
    slot = e % 2
    other = 1 - slot
    sem_tail = sems.at[4]

    def start_all(ex, sl, inbound):
        def body(j, cur):
            start = st_ref[j * N_EXPERTS + ex]
            n = pd_ref[j * N_EXPERTS + ex]

            def piece(off, sz):
                rows_hbm = pl.ds(pl.multiple_of(j * TILE_CAP + start + off, RUN_ALIGN), sz)
                rows_buf = pl.ds(pl.multiple_of(cur + off, RUN_ALIGN), sz)
                if inbound:
                    pltpu.make_async_copy(xs_hbm.at[rows_hbm], xy.at[sl, rows_buf], sems.at[sl]).start()
                else:
                    pltpu.make_async_copy(xy.at[sl, rows_buf], ys_hbm.at[rows_hbm], sems.at[2 + sl]).start()

            _for_each_piece(n, piece)
            return cur + n
        lax.fori_loop(0, N_TILES, body, 0)

    def expert_rows(ex):
        return lax.fori_loop(0, N_TILES, lambda j, c: c + pd_ref[j * N_EXPERTS + ex], 0, unroll=True)

    def wait_all(ex, sl, inbound):
        total = expert_rows(ex)
        for sz in WAIT_SIZES:
            @pl.when((total & sz) != 0)
            def _():
                rows = pl.ds(0, sz)
                if inbound:
                    pltpu.make_async_copy(xs_hbm.at[rows], xy.at[sl, rows], sems.at[sl]).wait()
                else:
                    pltpu.make_async_copy(xy.at[sl, rows], ys_hbm.at[rows], sems.at[2 + sl]).wait()
        return total

    def compute_chunks(first, last):
        def chunk(c, carry):
            rows = pl.ds(pl.multiple_of(c * MOE_CHUNK, MOE_CHUNK), MOE_CHUNK)
            x = xy[slot, rows, :]
            hid = _silu(_dot(x, wg_bf[...])) * _dot(x, wu_bf[...])
            xy[slot, rows, :] = _dot(hid.astype(BF16), wd_bf[...]).astype(BF16)
            return carry
        lax.fori_loop(first, last, chunk, 0)

    @pl.when(e == 0)
    def _():
        xy[...] = jnp.zeros(xy.shape, BF16)
        zeros_buf[...] = jnp.zeros(zeros_buf.shape, BF16)
        start_all(e, slot, True)

    wg_bf[...] = wg_ref[0, 0].astype(BF16)
    wu_bf[...] = wu_ref[0, 0].astype(BF16)
    wd_bf[...] = wd_ref[0, 0].astype(BF16)
    n_chunks = (wait_all(e, slot, True) + (MOE_CHUNK - 1)) // MOE_CHUNK

    mid = (n_chunks + 1) // 2
    compute_chunks(0, mid)

    @pl.when(e > 0)
    def _():
        wait_all(e - 1, other, False)

    @pl.when(e < N_EXPERTS - 1)
    def _():
        start_all(e + 1, other, True)

    compute_chunks(mid, n_chunks)
    start_all(e, slot, False)

    @pl.when(e == N_EXPERTS - 1)
    def _():
        wait_all(e, slot, False)

        def tail_copy(j, off, sz):
            used = st_ref[j * N_EXPERTS + e] + pd_ref[j * N_EXPERTS + e]
            rows_hbm = pl.ds(pl.multiple_of(j * TILE_CAP + used + off, RUN_ALIGN), sz)
            return pltpu.make_async_copy(zeros_buf.at[pl.ds(0, sz)], ys_hbm.at[rows_hbm], sem_tail)

        def tail_len(j):
            return TILE_CAP - (st_ref[j * N_EXPERTS + e] + pd_ref[j * N_EXPERTS + e])

        def t_start(j, c):
            _for_each_piece(tail_len(j), lambda off, sz: tail_copy(j, off, sz).start())
            return c

        def t_wait(j, c):
            _for_each_piece(tail_len(j), lambda off, sz: tail_copy(j, off, sz).wait())
            return c

        lax.fori_loop(0, N_TILES, t_start, 0)
        lax.fori_loop(0, N_TILES, t_wait, 0)


def _experts(layer, starts, pads, xs, prm):
    wspec = lambda shape: pl.BlockSpec((1, 1) + shape, lambda e, st, pd: (layer, e, 0, 0))
    return pl.pallas_call(
        _expert_kernel,
        grid_spec=pltpu.PrefetchScalarGridSpec(
            num_scalar_prefetch=2,
            grid=(N_EXPERTS,),
            in_specs=[pl.BlockSpec(memory_space=pl.ANY), wspec((D, EXPERT_FF)), wspec((D, EXPERT_FF)),
                      wspec((EXPERT_FF, D))],
            out_specs=pl.BlockSpec(memory_space=pl.ANY),
            scratch_shapes=[pltpu.VMEM((2, XY_ROWS, D), BF16), pltpu.VMEM((D, EXPERT_FF), BF16),
                            pltpu.VMEM((D, EXPERT_FF), BF16), pltpu.VMEM((EXPERT_FF, D), BF16),
                            pltpu.VMEM((RUN_SIZES[0], D), BF16),
                            pltpu.SemaphoreType.DMA((5,))]),
        out_shape=jax.ShapeDtypeStruct((N_TILES * TILE_CAP, D), BF16),
        compiler_params=pltpu.CompilerParams(dimension_semantics=("arbitrary",),
                                             vmem_limit_bytes=EXPERT_VMEM_LIMIT),
        name="experts_l%d" % layer,
    )(starts, pads, xs, prm["w_e_gate"], prm["w_e_up"], prm["w_e_down"])


def _final_kernel(x1_ref, ys_ref, pc_ref, mod_ref, op_ref, os_ref):
    x = _moe_combine(x1_ref, ys_ref, pc_ref, mod_ref)
    i = pl.program_id(0)

    @pl.when(i < PROMPT_TILES)
    def _():
        op_ref[...] = x

    @pl.when(i >= PROMPT_TILES)
    def _():
        os_ref[...] = x


def _final(x1, ys, pcol, mods3):
    row = lambda i: (i, 0)
    return pl.pallas_call(
        _final_kernel,
        grid=(N_TILES,),
        in_specs=[pl.BlockSpec((TM, D), row), pl.BlockSpec((TILE_CAP, D), row), pl.BlockSpec((TM, LANES), row),
                  pl.BlockSpec((1, 1, MOD_W), lambda i: ((DEPTH - 1) * COND_ROWS + _mod_row(i), 0, 0))],
        out_specs=[pl.BlockSpec((TM, D), lambda i: (jnp.minimum(i, PROMPT_TILES - 1), 0)),
                   pl.BlockSpec((TM, D), lambda i: (jnp.maximum(i - PROMPT_TILES, 0), 0))],
        out_shape=[jax.ShapeDtypeStruct((T_PROMPT, D), F32), jax.ShapeDtypeStruct((T_SAMPLE, D), F32)],
        compiler_params=_cparams(("arbitrary",)),
        name="final_residual",
    )(x1, ys, pcol, mods3)


def _rope_tables():
    t = np.arange(SAMPLE_LEN)
    n_freq = QK_ROPE // 4
    inv_freq = ROPE_THETA ** (-np.arange(n_freq, dtype=np.float32) / n_freq)
    row = (t // GRID_W).astype(np.float32)
    col = (t % GRID_W).astype(np.float32)
    ang = jnp.concatenate([jnp.asarray(row)[:, None] * inv_freq, jnp.asarray(col)[:, None] * inv_freq],
                          axis=-1)
    cos_p = jnp.repeat(jnp.cos(ang), 2, axis=-1)
    sin_p = jnp.repeat(jnp.sin(ang), 2, axis=-1) * jnp.tile(jnp.asarray([-1.0, 1.0], F32), QK_ROPE // 2)
    pad_l = jnp.ones((SAMPLE_LEN, QK_NOPE), F32)
    pad_r = jnp.ones((SAMPLE_LEN, LANES - QK_HEAD), F32)
    cos = jnp.concatenate([pad_l, cos_p, pad_r], axis=-1)
    sin = jnp.concatenate([0 * pad_l, sin_p, 0 * pad_r], axis=-1)
    ident_c = jnp.ones((TM, LANES), F32)
    return jnp.concatenate([ident_c, cos], axis=0), jnp.concatenate([0 * ident_c, sin], axis=0)


def _pad_lanes(x, lo, total):
    pad = [(0, 0)] * (x.ndim - 1) + [(lo, total - lo - x.shape[-1])]
    return jnp.pad(x, pad)


def kernel(x_prompt, x_sample, cache_mla_ckv, cache_mla_krope, cache_na_k, cache_na_v, c, c_ctx, w_ada, b_ada, norm1_g, norm2_g, w_in, q_a_g, kv_a_g, w_q_up, w_kv_up, mla_q_norm_g, mla_k_norm_g, conv_dw_w, conv_dw_b, conv_ln_g, conv_ln_b, na_q_norm_g, na_k_norm_g, na_rpb, w_out, w_router, b_router, w_e_gate, w_e_up, w_e_down):
    cos, sin = _rope_tables()
    vec = lambda a: a.reshape(DEPTH, 1, a.shape[-1])
    prm = {
        "norm1_g": vec(norm1_g), "norm2_g": vec(norm2_g), "w_in": w_in, "q_a_g": vec(q_a_g),
        "kv_a_g": vec(kv_a_g), "w_q_up": w_q_up, "w_kv_up": w_kv_up,
        "q_norm_g": vec(_pad_lanes(mla_q_norm_g, 0, LANES)), "k_norm_g": vec(_pad_lanes(mla_k_norm_g, 0, LANES)),
        "na_q_g": vec(jnp.tile(na_q_norm_g, (1, 2))), "na_k_g": vec(jnp.tile(na_k_norm_g, (1, 2))),
        "cos": cos, "sin": sin,
        "conv_w": conv_dw_w, "conv_b": vec(conv_dw_b), "conv_ln_g": vec(conv_ln_g), "conv_ln_b": vec(conv_ln_b),
        "w_out": w_out, "w_router_t": w_router.T,
        "b_router_t": jnp.broadcast_to(b_router[:, None], (N_EXPERTS, TM)),
        "w_e_gate": w_e_gate, "w_e_up": w_e_up, "w_e_down": w_e_down,
    }

    cond = jnp.concatenate([c_ctx[None], c, jnp.zeros((COND_ROWS - N_COND, D), F32)], axis=0)
    mods3 = _modulation(cond.T, w_ada, b_ada).reshape(DEPTH * COND_ROWS, 1, MOD_W)

    kc, vc = _ctx_kv(cache_mla_ckv, cache_mla_krope, prm)
    bias = _na_bias(na_rpb)
    cache_k = cache_na_k.reshape(N_SAMPLE, DEPTH, PAST, NA_WIDTH)
    cache_v = cache_na_v.reshape(N_SAMPLE, DEPTH, PAST, NA_WIDTH)

    x = x1 = ys = pcol = None
    new_ckv, new_kr, new_nak, new_nav = [], [], [], []
    for layer in range(DEPTH):
        if layer == 0:
            outs = _pre(layer, False, (x_prompt.reshape(T_PROMPT, D), x_sample.reshape(T_SAMPLE, D)), mods3, prm)
        else:
            outs = _pre(layer, True, (x1, ys, pcol), mods3, prm)
        x, q, k, v, ckv_n, kr, u, nq, nk, nv, nkf, nvf = outs
        om_p, on_p = _prompt_attn(q, k, v, nq, nk, nv)
        om_s = _sample_mla(layer, q, k, v, kc, vc)
        on_s = _sample_na(layer, nq, nk, nv, cache_k, cache_v, bias)
        x1, xs, pcol, run_start, run_len = _post(layer, x, om_p, om_s, u, on_p, on_s, mods3, prm)
        ys = _experts(layer, run_start[:, :, 0].reshape(-1), run_len[:, :, 0].reshape(-1), xs, prm)
        new_ckv.append(ckv_n.reshape(N_PROMPT, PROMPT_LEN, KV_LORA))
        new_kr.append(kr)
        new_nak.append(nkf)
        new_nav.append(nvf)
    y_prompt, y_sample = _final(x1, ys, pcol, mods3)
    return (y_prompt.reshape(N_PROMPT, PROMPT_LEN, D), y_sample.reshape(N_SAMPLE, SAMPLE_LEN, D),
            jnp.stack(new_ckv, axis=1), jnp.stack(new_kr, axis=1).transpose(0, 1, 3, 2),
            jnp.stack(new_nak, axis=1).transpose(0, 1, 4, 2, 3), jnp.stack(new_nav, axis=1).transpose(0, 1, 4, 2, 3))
```

```python
import functools

import numpy as np
import jax
import jax.numpy as jnp
from jax import lax
from jax.experimental import pallas as pl
from jax.experimental.pallas import tpu as pltpu

F32 = jnp.float32
BF16 = jnp.bfloat16

D = 1024
N_PROMPT = 16
PROMPT_LEN = 256
N_SAMPLE = 4
SAMPLE_LEN = 1024
DEPTH = 4
PAST = 512
GRID_W = 64
GRID_ROWS = SAMPLE_LEN // GRID_W
MLA_HEADS = 8
QK_NOPE = 64
QK_ROPE = 32
QK_HEAD = QK_NOPE + QK_ROPE
V_HEAD = 64
Q_LORA = 256
KV_LORA = 128
CONV_CH = 256
CONV_K = 31
NA_HEADS = 4
NA_DIM = 64
NA_WIDTH = NA_HEADS * NA_DIM
NA_WIN_H = 8
NA_WIN_W = 16
N_EXPERTS = 16
N_GROUPS = 4
EPG = 4
EXPERT_FF = 512
ROPE_THETA = 10000.0
EPS = 1e-6
NEG_INF = -1e30
MLA_SCALE = QK_HEAD ** -0.5
NA_SCALE = NA_DIM ** -0.5

LANES = 128
SUBLANES = 8
TM = 256
T_PROMPT = N_PROMPT * PROMPT_LEN
T_SAMPLE = N_SAMPLE * SAMPLE_LEN
T_ALL = T_PROMPT + T_SAMPLE
N_TILES = T_ALL // TM
PROMPT_TILES = T_PROMPT // TM
TILES_PER_SAMPLE = SAMPLE_LEN // TM
N_COND = 1 + N_SAMPLE
COND_ROWS = 8
MOD_W = 6 * D

COL_Q = 0
COL_KV = Q_LORA
COL_KR = COL_KV + KV_LORA
COL_CONV = COL_KR + LANES
COL_NA = COL_CONV + 2 * CONV_CH
IN_COLS_P = COL_NA + 3 * NA_WIDTH
IN_COLS = IN_COLS_P - (LANES - QK_ROPE)

NA_KEY_ROWS = 12
NA_KEYS = NA_KEY_ROWS * GRID_W

VMEM_LIMIT = 56 * 1024 * 1024

RUN_ALIGN = 16
TILE_CAP = 768
RUN_SIZES = (256, 128, 64, 32, 16)
MOE_CHUNK = 256
XY_ROWS = N_TILES * TM
WAIT_SIZES = (8192, 4096, 2048, 1024, 512, 256, 128, 64, 32, 16)
EXPERT_VMEM_LIMIT = 60 * 1024 * 1024


def _cparams(sem):
    return pltpu.CompilerParams(dimension_semantics=sem, vmem_limit_bytes=VMEM_LIMIT)


def _mod_row(i):
    return jnp.where(i < PROMPT_TILES, 0, 1 + (i - PROMPT_TILES) // TILES_PER_SAMPLE)


def _rope_block(i):
    return jnp.where(i < PROMPT_TILES, 0, 1 + (i - PROMPT_TILES) % TILES_PER_SAMPLE)


def _dot(a, b):
    return jnp.dot(a, b, preferred_element_type=F32)


def _dot_t(a, b):
    return lax.dot_general(a, b, (((1,), (1,)), ((), ())), preferred_element_type=F32)


def _silu(x):
    return x * jax.nn.sigmoid(x)


def _lane_is_low(shape):
    return lax.broadcasted_iota(jnp.int32, shape, len(shape) - 1) < (LANES // 2)


MOD_BLK = 768


def _mod_kernel(ct_ref, w_ref, b_ref, o_ref, cb_ref):
    first = (pl.program_id(0) == 0) & (pl.program_id(1) == 0)

    @pl.when(first)
    def _():
        s = _silu(ct_ref[...])
        for m in range(N_COND):
            cb_ref[m] = jnp.broadcast_to(s[:, m:m + 1], (D, LANES))

    for j in range(MOD_BLK // LANES):
        w = w_ref[0, :, j * LANES:(j + 1) * LANES]
        rows = [jnp.sum(w * cb_ref[m], axis=0, keepdims=True) for m in range(N_COND)]
        rows.append(jnp.zeros((COND_ROWS - N_COND, LANES), F32))
        o_ref[0, :, j * LANES:(j + 1) * LANES] = (
            jnp.concatenate(rows, axis=0) + b_ref[0, :, j * LANES:(j + 1) * LANES])


def _modulation(cond_t, w_ada, b_ada):
    nb = MOD_W // MOD_BLK
    return pl.pallas_call(
        _mod_kernel,
        grid=(DEPTH, nb),
        in_specs=[
            pl.BlockSpec((D, COND_ROWS), lambda l, j: (0, 0)),
            pl.BlockSpec((1, D, MOD_BLK), lambda l, j: (l, 0, j)),
            pl.BlockSpec((1, 1, MOD_BLK), lambda l, j: (l, 0, j)),
        ],
        out_specs=pl.BlockSpec((1, COND_ROWS, MOD_BLK), lambda l, j: (l, 0, j)),
        out_shape=jax.ShapeDtypeStruct((DEPTH, COND_ROWS, MOD_W), F32),
        scratch_shapes=[pltpu.VMEM((N_COND, D, LANES), F32)],
        compiler_params=_cparams(("arbitrary", "arbitrary")),
        name="modulation",
    )(cond_t, w_ada, b_ada.reshape(DEPTH, 1, MOD_W))


def _lane_sums(sq, group):
    k = sq.shape[1]
    hi = sq.astype(BF16)
    lo = (sq - hi.astype(F32)).astype(BF16)
    if group == k:
        sel = jnp.ones((k, LANES), BF16)
    else:
        r = lax.broadcasted_iota(jnp.int32, (k, LANES), 0)
        c = lax.broadcasted_iota(jnp.int32, (k, LANES), 1)
        sel = ((r // group) == (c // group)).astype(BF16)
    return _dot(hi, sel) + _dot(lo, sel)


def _rms(x, width):
    rinv = lax.rsqrt(_lane_sums(x * x, x.shape[1]) * (1.0 / width) + EPS)
    reps = x.shape[1] // LANES
    return x * (rinv if reps == 1 else jnp.concatenate([rinv] * reps, axis=1))


def _rope(x, cos, sin):
    lane = lax.broadcasted_iota(jnp.int32, x.shape, 1)
    partner = jnp.where((lane & 1) == 0, pltpu.roll(x, LANES - 1, 1), pltpu.roll(x, 1, 1))
    return x * cos + partner * sin


def _pair_rms(x, gain):
    ms = _lane_sums(x * x, NA_DIM) * (1.0 / NA_DIM)
    return x * lax.rsqrt(ms + EPS) * gain


def _softmax_pv(scores, values, feature_major=()):
    m = scores[0].max(axis=-1, keepdims=True)
    for s in scores[1:]:
        m = jnp.maximum(m, s.max(axis=-1, keepdims=True))
    acc = None
    den = None
    for idx, (s, v) in enumerate(zip(scores, values)):
        p = jnp.exp(s - m)
        d = jnp.sum(p, axis=-1, keepdims=True)
        o = _dot_t(p.astype(BF16), v) if idx in feature_major else _dot(p.astype(BF16), v)
        acc = o if acc is None else acc + o
        den = d if den is None else den + d
    return acc * (1.0 / den)


def _moe_combine(x1_ref, ys_ref, pc_ref, modp_ref):
    pc = pc_ref[...]
    lane = lax.broadcasted_iota(jnp.int32, (TM, TILE_CAP), 1)
    sel = (jnp.where(lane == pc[:, 0:1].astype(jnp.int32), pc[:, 2:3], 0.0)
           + jnp.where(lane == pc[:, 1:2].astype(jnp.int32), pc[:, 3:4], 0.0))
    gate2 = modp_ref[0, :, 5 * D:6 * D]
    return x1_ref[...] + gate2 * _dot(sel.astype(BF16), ys_ref[...])


def _place_w_in(w_ref, w_bf):
    off_kv = Q_LORA + KV_LORA
    off_kr = off_kv + QK_ROPE
    w_bf[0:off_kv, :] = w_ref[0, 0:off_kv, :].astype(BF16)
    w_bf[COL_KR:COL_KR + LANES, :] = jnp.zeros((LANES, D), BF16)
    w_bf[COL_KR + QK_NOPE:COL_KR + QK_HEAD, :] = w_ref[0, off_kv:off_kr, :].astype(BF16)
    w_bf[COL_CONV:IN_COLS_P, :] = w_ref[0, off_kr:off_kr + (IN_COLS_P - COL_CONV), :].astype(BF16)


def _place_w_q(w_ref, w_bf):
    w_bf[...] = jnp.zeros(w_bf.shape, BF16)
    for hd in range(MLA_HEADS):
        w_bf[:, hd * LANES:hd * LANES + QK_HEAD] = w_ref[0, :, hd * QK_HEAD:(hd + 1) * QK_HEAD].astype(BF16)


def _place_w_kv(w_ref, wk_bf, wv_bf):
    wk_bf[...] = jnp.zeros(wk_bf.shape, BF16)
    for hd in range(MLA_HEADS):
        c0 = hd * (QK_NOPE + V_HEAD)
        wk_bf[:, hd * LANES:hd * LANES + QK_NOPE] = w_ref[0, :, c0:c0 + QK_NOPE].astype(BF16)
        wv_bf[:, hd * V_HEAD:(hd + 1) * V_HEAD] = w_ref[0, :, c0 + QK_NOPE:c0 + QK_NOPE + V_HEAD].astype(BF16)


def _pre_kernel(has_moe, *refs):
    if has_moe:
        x1_ref, ys_ref, pc_ref, modp_ref = refs[:4]
        refs = refs[4:]
    else:
        xp_ref, xs_ref = refs[:2]
        refs = refs[2:]
    (mod_ref, n1g_ref, win_ref, qag_ref, kvag_ref, wq_ref, wkv_ref, qng_ref, kng_ref,
     naqg_ref, nakg_ref, cos_ref, sin_ref) = refs[:13]
    refs = refs[13:]
    xo_ref = refs[0]
    refs = refs[1:]
    (q_out, k_out, v_out, ckv_out, kr_out, u_out, naq_out, nak_out, nav_out, nakf_out, navf_out,
     win_bf, wq_bf, wk_bf, wv_bf, proj_buf) = refs

    s = pl.program_id(0)
    i = jnp.minimum(s, N_TILES - 1)

    @pl.when(s == 0)
    def _():
        _place_w_in(win_ref, win_bf)
        _place_w_q(wq_ref, wq_bf)
        _place_w_kv(wkv_ref, wk_bf, wv_bf)
        proj_buf[...] = jnp.zeros(proj_buf.shape, F32)

    if has_moe:
        x = _moe_combine(x1_ref, ys_ref, pc_ref, modp_ref)
    else:
        x = jnp.where(i < PROMPT_TILES, xp_ref[...], xs_ref[...])
    xo_ref[...] = x
    shift = mod_ref[0, :, 0:D]
    scale = mod_ref[0, :, D:2 * D]
    h = _rms(x, D) * n1g_ref[0] * (1.0 + scale) + shift
    proj = proj_buf[(s + 1) % 2]
    proj_buf[s % 2] = _dot_t(h.astype(BF16), win_bf[...])

    cos = cos_ref[...]
    sin = sin_ref[...]

    q_lat = _rms(proj[:, COL_Q:COL_Q + Q_LORA], Q_LORA) * qag_ref[0]
    q = _dot(q_lat.astype(BF16), wq_bf[...])
    for hd in range(MLA_HEADS):
        qh = q[:, hd * LANES:(hd + 1) * LANES]
        qh = _rope(_rms(qh, QK_HEAD) * qng_ref[0], cos, sin)
        q_out[hd] = (qh * MLA_SCALE).astype(BF16)

    ckv_n = _rms(proj[:, COL_KV:COL_KV + KV_LORA], KV_LORA) * kvag_ref[0]
    kr = proj[:, COL_KR:COL_KR + LANES]
    ckv_b = ckv_n.astype(BF16)
    k_nope = _dot(ckv_b, wk_bf[...])
    kr_rot = _rope(kr * kng_ref[0], cos, sin)
    kr_sq = kr * kr
    for hd in range(MLA_HEADS):
        kn = k_nope[:, hd * LANES:(hd + 1) * LANES]
        rinv = lax.rsqrt(_lane_sums(kn * kn + kr_sq, LANES) * (1.0 / QK_HEAD) + EPS)
        k_out[hd] = ((kn * kng_ref[0] + kr_rot) * rinv).astype(BF16)
    v = _dot(ckv_b, wv_bf[...])
    for p in range(MLA_HEADS // 2):
        v_out[p] = v[:, p * LANES:(p + 1) * LANES].astype(BF16)

    a = proj[:, COL_CONV:COL_CONV + CONV_CH]
    g = proj[:, COL_CONV + CONV_CH:COL_CONV + 2 * CONV_CH]
    u_out[...] = a * jax.nn.sigmoid(g)

    k_slabs, v_slabs = [], []
    for p in range(NA_HEADS // 2):
        sl = slice(p * LANES, (p + 1) * LANES)
        qn = _pair_rms(proj[:, COL_NA + p * LANES:COL_NA + (p + 1) * LANES], naqg_ref[0])
        kn = _pair_rms(proj[:, COL_NA + NA_WIDTH + p * LANES:COL_NA + NA_WIDTH + (p + 1) * LANES],
                       nakg_ref[0])
        vn = proj[:, COL_NA + 2 * NA_WIDTH + p * LANES:COL_NA + 2 * NA_WIDTH + (p + 1) * LANES]
        naq_out[:, sl] = (qn * NA_SCALE).astype(BF16)
        nak_out[:, sl] = kn.astype(BF16)
        nav_out[:, sl] = vn.astype(BF16)
        k_slabs.append(kn)
        v_slabs.append(vn)

    kr_t = kr.T
    nak_t = jnp.concatenate(k_slabs, axis=1).T.reshape(NA_HEADS, NA_DIM, TM)
    nav_t = jnp.concatenate(v_slabs, axis=1).T.reshape(NA_HEADS, NA_DIM, TM)

    @pl.when(s <= PROMPT_TILES)
    def _():
        ckv_out[...] = ckv_n
        kr_out[0] = kr_t[QK_NOPE:QK_HEAD, :]
        nakf_out[0] = nak_t
        navf_out[0] = nav_t


def _pre(layer, has_moe, x_or_parts, mods3, prm):
    cur = lambda s: jnp.minimum(s, N_TILES - 1)
    prev = lambda s: jnp.maximum(s - 1, 0)
    row = lambda s: (cur(s), 0)
    row2 = lambda s: (prev(s), 0)
    lay3 = lambda s: (layer, 0, 0)
    in_specs = []
    args = []
    if has_moe:
        x1, ys, pcol = x_or_parts
        in_specs += [pl.BlockSpec((TM, D), row), pl.BlockSpec((TILE_CAP, D), row),
                     pl.BlockSpec((TM, LANES), row),
                     pl.BlockSpec((1, 1, MOD_W), lambda s: ((layer - 1) * COND_ROWS + _mod_row(cur(s)), 0, 0))]
        args += [x1, ys, pcol, mods3]
    else:
        x_prompt, x_sample = x_or_parts
        in_specs += [pl.BlockSpec((TM, D), lambda s: (jnp.minimum(s, PROMPT_TILES - 1), 0)),
                     pl.BlockSpec((TM, D), lambda s: (jnp.maximum(cur(s) - PROMPT_TILES, 0), 0))]
        args += [x_prompt, x_sample]
    in_specs += [
        pl.BlockSpec((1, 1, MOD_W), lambda s: (layer * COND_ROWS + _mod_row(cur(s)), 0, 0)),
        pl.BlockSpec((1, 1, D), lay3),
        pl.BlockSpec((1, IN_COLS, D), lay3),
        pl.BlockSpec((1, 1, Q_LORA), lay3),
        pl.BlockSpec((1, 1, KV_LORA), lay3),
        pl.BlockSpec((1, Q_LORA, MLA_HEADS * QK_HEAD), lay3),
        pl.BlockSpec((1, KV_LORA, MLA_HEADS * (QK_NOPE + V_HEAD)), lay3),
        pl.BlockSpec((1, 1, LANES), lay3),
        pl.BlockSpec((1, 1, LANES), lay3),
        pl.BlockSpec((1, 1, LANES), lay3),
        pl.BlockSpec((1, 1, LANES), lay3),
        pl.BlockSpec((TM, LANES), lambda s: (_rope_block(prev(s)), 0)),
        pl.BlockSpec((TM, LANES), lambda s: (_rope_block(prev(s)), 0)),
    ]
    args += [mods3, prm["norm1_g"], prm["w_in_t"], prm["q_a_g"], prm["kv_a_g"], prm["w_q_up"], prm["w_kv_up"],
             prm["q_norm_g"], prm["k_norm_g"], prm["na_q_g"], prm["na_k_g"], prm["cos"], prm["sin"]]
    out_specs = [pl.BlockSpec((TM, D), row)]
    out_shape = [jax.ShapeDtypeStruct((T_ALL, D), F32)]
    head3 = lambda n: pl.BlockSpec((n, TM, LANES), lambda s: (0, prev(s), 0))
    seq = lambda s: jnp.minimum(prev(s), PROMPT_TILES - 1)
    out_specs += [head3(MLA_HEADS), head3(MLA_HEADS), head3(MLA_HEADS // 2),
                  pl.BlockSpec((TM, KV_LORA), lambda s: (seq(s), 0)),
                  pl.BlockSpec((1, QK_ROPE, TM), lambda s: (seq(s), 0, 0)),
                  pl.BlockSpec((TM, CONV_CH), row2)] + [pl.BlockSpec((TM, NA_WIDTH), row2)] * 3
    out_specs += [pl.BlockSpec((1, NA_HEADS, NA_DIM, TM), lambda s: (seq(s), 0, 0, 0))] * 2
    out_shape += [
        jax.ShapeDtypeStruct((MLA_HEADS, T_ALL, LANES), BF16),
        jax.ShapeDtypeStruct((MLA_HEADS, T_ALL, LANES), BF16),
        jax.ShapeDtypeStruct((MLA_HEADS // 2, T_ALL, LANES), BF16),
        jax.ShapeDtypeStruct((T_PROMPT, KV_LORA), F32),
        jax.ShapeDtypeStruct((N_PROMPT, QK_ROPE, PROMPT_LEN), F32),
        jax.ShapeDtypeStruct((T_ALL, CONV_CH), F32),
        jax.ShapeDtypeStruct((T_ALL, NA_WIDTH), BF16),
        jax.ShapeDtypeStruct((T_ALL, NA_WIDTH), BF16),
        jax.ShapeDtypeStruct((T_ALL, NA_WIDTH), BF16),
        jax.ShapeDtypeStruct((N_PROMPT, NA_HEADS, NA_DIM, PROMPT_LEN), F32),
        jax.ShapeDtypeStruct((N_PROMPT, NA_HEADS, NA_DIM, PROMPT_LEN), F32),
    ]
    return pl.pallas_call(
        functools.partial(_pre_kernel, has_moe),
        grid=(N_TILES + 1,),
        in_specs=in_specs,
        out_specs=out_specs,
        out_shape=out_shape,
        scratch_shapes=[pltpu.VMEM((IN_COLS_P, D), BF16), pltpu.VMEM((Q_LORA, MLA_HEADS * LANES), BF16),
                        pltpu.VMEM((KV_LORA, MLA_HEADS * LANES), BF16),
                        pltpu.VMEM((KV_LORA, MLA_HEADS * V_HEAD), BF16),
                        pltpu.VMEM((2, TM, IN_COLS_P), F32)],
        compiler_params=_cparams(("arbitrary",)),
        name="pre_l%d" % layer,
    )(*args)


def _ctx_kernel(ckv_ref, kr_ref, wkv_ref, kng_ref, kc_out, vc_out, wk_bf, wv_bf):
    @pl.when(pl.program_id(1) == 0)
    def _():
        _place_w_kv(wkv_ref, wk_bf, wv_bf)

    ckv = ckv_ref[0, 0].astype(BF16)
    kr = jnp.concatenate([jnp.zeros((PAST, QK_NOPE), F32), kr_ref[0, 0].T,
                          jnp.zeros((PAST, LANES - QK_HEAD), F32)], axis=1)
    k_nope = _dot(ckv, wk_bf[...])
    for hd in range(MLA_HEADS):
        kh = k_nope[:, hd * LANES:(hd + 1) * LANES] + kr
        kc_out[0, 0, hd] = (_rms(kh, QK_HEAD) * kng_ref[0]).astype(BF16)
    v = _dot(ckv, wv_bf[...])
    for p in range(MLA_HEADS // 2):
        vc_out[0, 0, p] = v[:, p * LANES:(p + 1) * LANES].astype(BF16)


def _ctx_kv(cache_ckv, cache_krope, prm):
    return pl.pallas_call(
        _ctx_kernel,
        grid=(DEPTH, N_SAMPLE),
        in_specs=[
            pl.BlockSpec((1, 1, PAST, KV_LORA), lambda l, b: (b, l, 0, 0)),
            pl.BlockSpec((1, 1, QK_ROPE, PAST), lambda l, b: (b, l, 0, 0)),
            pl.BlockSpec((1, KV_LORA, MLA_HEADS * (QK_NOPE + V_HEAD)), lambda l, b: (l, 0, 0)),
            pl.BlockSpec((1, 1, LANES), lambda l, b: (l, 0, 0)),
        ],
        out_specs=[
            pl.BlockSpec((1, 1, MLA_HEADS, PAST, LANES), lambda l, b: (l, b, 0, 0, 0)),
            pl.BlockSpec((1, 1, MLA_HEADS // 2, PAST, LANES), lambda l, b: (l, b, 0, 0, 0)),
        ],
        out_shape=[
            jax.ShapeDtypeStruct((DEPTH, N_SAMPLE, MLA_HEADS, PAST, LANES), BF16),
            jax.ShapeDtypeStruct((DEPTH, N_SAMPLE, MLA_HEADS // 2, PAST, LANES), BF16),
        ],
        scratch_shapes=[pltpu.VMEM((KV_LORA, MLA_HEADS * LANES), BF16),
                        pltpu.VMEM((KV_LORA, MLA_HEADS * V_HEAD), BF16)],
        compiler_params=_cparams(("arbitrary", "arbitrary")),
        name="ctx_kv",
    )(cache_ckv, cache_krope, prm["w_kv_up"], prm["k_norm_g"])


N_DR = 2 * NA_WIN_H - 1
N_DC = 2 * NA_WIN_W - 1


def _bias_kernel(rpb_ref, o_ref, blk_ref):
    cq = lax.broadcasted_iota(jnp.int32, (GRID_W, GRID_W), 0)
    ck = lax.broadcasted_iota(jnp.int32, (GRID_W, GRID_W), 1)
    dc = jnp.clip(ck - cq + (NA_WIN_W - 1), 0, N_DC - 1)
    c0 = jnp.clip(cq - NA_WIN_W // 2, 0, GRID_W - NA_WIN_W)
    in_win = (ck >= c0) & (ck < c0 + NA_WIN_W)
    for dr in range(N_DR):
        acc = jnp.zeros((GRID_W, GRID_W), F32)
        for j in range(N_DC):
            acc = jnp.where(dc == j, rpb_ref[pl.program_id(0), pl.program_id(1), dr, j], acc)
        blk_ref[dr] = jnp.where(in_win, acc, NEG_INF)
    blk_ref[N_DR] = jnp.full((GRID_W, GRID_W), NEG_INF, F32)
    idx = _na_block_index()
    for i in range(TILES_PER_SAMPLE):
        for rq in range(TM // GRID_W):
            for pp in range(NA_KEY_ROWS // 2):
                pair = jnp.concatenate([blk_ref[int(idx[i, rq, 2 * pp])], blk_ref[int(idx[i, rq, 2 * pp + 1])]],
                                       axis=1)
                o_ref[0, 0, i, rq * GRID_W:(rq + 1) * GRID_W, pp * LANES:(pp + 1) * LANES] = pair


def _na_bias(na_rpb):
    return pl.pallas_call(
        _bias_kernel,
        grid=(DEPTH, NA_HEADS),
        in_specs=[pl.BlockSpec(memory_space=pltpu.SMEM)],
        out_specs=pl.BlockSpec((1, 1, TILES_PER_SAMPLE, TM, NA_KEYS), lambda l, h: (l, h, 0, 0, 0)),
        out_shape=jax.ShapeDtypeStruct((DEPTH, NA_HEADS, TILES_PER_SAMPLE, TM, NA_KEYS), F32),
        scratch_shapes=[pltpu.VMEM((N_DR + 1, GRID_W, GRID_W), F32)],
        compiler_params=_cparams(("arbitrary", "arbitrary")),
        name="na_bias",
    )(na_rpb)


def _na_block_index():
    idx = np.full((TILES_PER_SAMPLE, TM // GRID_W, NA_KEY_ROWS), N_DR, np.int32)
    for i in range(TILES_PER_SAMPLE):
        ks = _na_key_start_row(i)
        for rq in range(TM // GRID_W):
            r = i * (TM // GRID_W) + rq
            r0 = min(max(r - NA_WIN_H // 2, 0), GRID_ROWS - NA_WIN_H)
            for rk in range(NA_KEY_ROWS):
                kr = ks + rk
                if r0 <= kr < r0 + NA_WIN_H:
                    idx[i, rq, rk] = kr - r + (NA_WIN_H - 1)
    return idx


def _na_key_start_row(i):
    return 0 if i < TILES_PER_SAMPLE // 2 else GRID_ROWS - NA_KEY_ROWS


def _prompt_attn_kernel(q_ref, k_ref, v_ref, nq_ref, nk_ref, nv_ref, om_ref, on_ref):
    low = _lane_is_low((PROMPT_LEN, LANES))
    for p in range(MLA_HEADS // 2):
        outs = []
        for hh in range(2):
            hd = 2 * p + hh
            s = _dot_t(q_ref[hd], k_ref[hd])
            outs.append(_softmax_pv([s], [v_ref[p]]))
        om_ref[p] = jnp.where(low, outs[0], outs[1]).astype(BF16)
    for p in range(NA_HEADS // 2):
        sl = slice(p * LANES, (p + 1) * LANES)
        qp = nq_ref[:, sl]
        kp = nk_ref[:, sl]
        vp = nv_ref[:, sl]
        outs = []
        for hh in range(2):
            qm = jnp.where(low if hh == 0 else ~low, qp, jnp.zeros_like(qp))
            outs.append(_softmax_pv([_dot_t(qm, kp)], [vp]))
        on_ref[:, sl] = jnp.where(low, outs[0], outs[1]).astype(BF16)


def _prompt_attn(q, k, v, nq, nk, nv):
    head3 = lambda n: pl.BlockSpec((n, PROMPT_LEN, LANES), lambda b: (0, b, 0))
    row = pl.BlockSpec((PROMPT_LEN, NA_WIDTH), lambda b: (b, 0))
    return pl.pallas_call(
        _prompt_attn_kernel,
        grid=(N_PROMPT,),
        in_specs=[head3(MLA_HEADS), head3(MLA_HEADS), head3(MLA_HEADS // 2), row, row, row],
        out_specs=[head3(MLA_HEADS // 2), row],
        out_shape=[jax.ShapeDtypeStruct((MLA_HEADS // 2, T_PROMPT, LANES), BF16),
                   jax.ShapeDtypeStruct((T_PROMPT, NA_WIDTH), BF16)],
        compiler_params=_cparams(("arbitrary",)),
        name="prompt_attn",
    )(q, k, v, nq, nk, nv)


def _sample_mla_kernel(q_ref, k_ref, v_ref, kc_ref, vc_ref, o_ref):
    low = _lane_is_low((TM, LANES))
    for t in range(TILES_PER_SAMPLE):
        rows = slice(t * TM, (t + 1) * TM)
        outs = []
        for hh in range(2):
            q = q_ref[hh, rows, :]
            s_lat = _dot_t(q, k_ref[hh])
            s_ctx = _dot_t(q, kc_ref[0, 0, hh])
            outs.append(_softmax_pv([s_lat, s_ctx], [v_ref[0], vc_ref[0, 0, 0]]))
        o_ref[0, rows, :] = jnp.where(low, outs[0], outs[1]).astype(BF16)


def _sample_mla(layer, q, k, v, kc, vc):
    seq_blk0 = T_PROMPT // SAMPLE_LEN
    return pl.pallas_call(
        _sample_mla_kernel,
        grid=(N_SAMPLE, MLA_HEADS // 2),
        in_specs=[
            pl.BlockSpec((2, SAMPLE_LEN, LANES), lambda b, p: (p, seq_blk0 + b, 0)),
            pl.BlockSpec((2, SAMPLE_LEN, LANES), lambda b, p: (p, seq_blk0 + b, 0)),
            pl.BlockSpec((1, SAMPLE_LEN, LANES), lambda b, p: (p, seq_blk0 + b, 0)),
            pl.BlockSpec((1, 1, 2, PAST, LANES), lambda b, p: (layer, b, p, 0, 0)),
            pl.BlockSpec((1, 1, 1, PAST, LANES), lambda b, p: (layer, b, p, 0, 0)),
        ],
        out_specs=pl.BlockSpec((1, SAMPLE_LEN, LANES), lambda b, p: (p, b, 0)),
        out_shape=jax.ShapeDtypeStruct((MLA_HEADS // 2, T_SAMPLE, LANES), BF16),
        compiler_params=_cparams(("arbitrary", "arbitrary")),
        name="sample_mla",
    )(q, k, v, kc, vc)


def _sample_na_kernel(q_ref, k_ref, v_ref, kc_ref, vc_ref, b_ref, o_ref):
    kc_t = kc_ref[0, 0].astype(BF16)
    vc_t = vc_ref[0, 0].astype(BF16)
    low = _lane_is_low((TM, LANES))
    for t in range(TILES_PER_SAMPLE):
        rows = slice(t * TM, (t + 1) * TM)
        start = _na_key_start_row(t) * GRID_W
        k_win = k_ref[start:start + NA_KEYS, :]
        v_win = v_ref[start:start + NA_KEYS, :]
        q = q_ref[rows, :]
        outs = []
        for hh in range(2):
            qm = jnp.where(low if hh == 0 else ~low, q, jnp.zeros_like(q))
            s_loc = _dot_t(qm, k_win) + b_ref[0, hh, t]
            s_ctx = _dot(qm, kc_t)
            outs.append(_softmax_pv([s_loc, s_ctx], [v_win, vc_t], feature_major=(1,)))
        o_ref[rows, :] = jnp.where(low, outs[0], outs[1]).astype(BF16)


def _sample_na(layer, nq, nk, nv, cache_k, cache_v, bias):
    seq_blk0 = T_PROMPT // SAMPLE_LEN
    return pl.pallas_call(
        _sample_na_kernel,
        grid=(NA_HEADS // 2, N_SAMPLE),
        in_specs=[
            pl.BlockSpec((SAMPLE_LEN, LANES), lambda p, b: (seq_blk0 + b, p)),
            pl.BlockSpec((SAMPLE_LEN, LANES), lambda p, b: (seq_blk0 + b, p)),
            pl.BlockSpec((SAMPLE_LEN, LANES), lambda p, b: (seq_blk0 + b, p)),
            pl.BlockSpec((1, 1, LANES, PAST), lambda p, b: (b, layer, p, 0)),
            pl.BlockSpec((1, 1, LANES, PAST), lambda p, b: (b, layer, p, 0)),
            pl.BlockSpec((1, 2, TILES_PER_SAMPLE, TM, NA_KEYS), lambda p, b: (layer, p, 0, 0, 0)),
        ],
        out_specs=pl.BlockSpec((SAMPLE_LEN, LANES), lambda p, b: (b, p)),
        out_shape=jax.ShapeDtypeStruct((T_SAMPLE, NA_WIDTH), BF16),
        compiler_params=_cparams(("arbitrary", "arbitrary")),
        name="sample_na",
    )(nq, nk, nv, cache_k, cache_v, bias)


CONV_PAD = 16


def _argmax_first(cols):
    best = cols[0]
    idx = jnp.zeros(best.shape, jnp.int32)
    for j in range(1, len(cols)):
        upd = cols[j] > best
        idx = jnp.where(upd, j, idx)
        best = jnp.where(upd, cols[j], best)
    return idx


def _route(aff, bias):
    sel = aff + bias
    rows = [sel[e:e + 1, :] for e in range(N_EXPERTS)]
    group_scores = []
    for g in range(N_GROUPS):
        c = rows[g * EPG:(g + 1) * EPG]
        best = None
        for a in range(EPG):
            for b in range(a + 1, EPG):
                s = c[a] + c[b]
                best = s if best is None else jnp.maximum(best, s)
        group_scores.append(best)
    g_idx = _argmax_first(group_scores)
    in_group = []
    for j in range(EPG):
        v = rows[j]
        for g in range(1, N_GROUPS):
            v = jnp.where(g_idx == g, rows[g * EPG + j], v)
        in_group.append(v)
    i1 = _argmax_first(in_group)
    masked = [jnp.where(i1 == j, -jnp.inf, in_group[j]) for j in range(EPG)]
    i2 = _argmax_first(masked)
    e1 = g_idx * EPG + i1
    e2 = g_idx * EPG + i2
    sub = lax.broadcasted_iota(jnp.int32, aff.shape, 0)
    w1 = jnp.sum(jnp.where(sub == e1, aff, 0.0), axis=0, keepdims=True)
    w2 = jnp.sum(jnp.where(sub == e2, aff, 0.0), axis=0, keepdims=True)
    tot = w1 + w2
    return e1, e2, w1 / tot, w2 / tot


def _sort_tile(e1, e2):
    sub = lax.broadcasted_iota(jnp.int32, (N_EXPERTS, TM), 0)
    oh1 = sub == e1
    oh2 = sub == e2
    one = jnp.ones((N_EXPERTS, TM), F32)
    cnt1 = jnp.sum(jnp.where(oh1, one, 0.0), axis=1, keepdims=True) * one
    cnt2 = jnp.sum(jnp.where(oh2, one, 0.0), axis=1, keepdims=True) * one
    tot = (cnt1 + cnt2).astype(jnp.int32)
    padded = (tot + (RUN_ALIGN - 1)) & (-RUN_ALIGN)
    ee = lax.broadcasted_iota(jnp.int32, (N_EXPERTS, N_EXPERTS), 0)
    ef = lax.broadcasted_iota(jnp.int32, (N_EXPERTS, N_EXPERTS), 1)
    start = _dot((ef < ee).astype(BF16), padded.astype(BF16))
    ta = lax.broadcasted_iota(jnp.int32, (TM, TM), 0)
    tb = lax.broadcasted_iota(jnp.int32, (TM, TM), 1)
    before = (ta < tb).astype(BF16)
    rank1 = _dot(oh1.astype(BF16), before)
    rank2 = _dot(oh2.astype(BF16), before)
    pos1 = jnp.sum(jnp.where(oh1, start + rank1, 0.0), axis=0, keepdims=True)
    pos2 = jnp.sum(jnp.where(oh2, start + cnt1 + rank2, 0.0), axis=0, keepdims=True)
    return pos1.astype(jnp.int32), pos2.astype(jnp.int32), start.astype(jnp.int32), padded


def _post_kernel(x_ref, omp_ref, oms_ref, up_ref, uc_ref, un_ref, onp_ref, ons_ref, mod_ref, cw_ref, cb_ref,
                 lg_ref, lb_ref, wo_ref, n2g_ref, wr_ref, br_ref, x1_out, xs_out, pc_out, st_out, pd_out,
                 wo_bf, ext_ref, shift_ref, h2_buf, logit_buf):
    s = pl.program_id(0)
    i = jnp.minimum(s, N_TILES - 1)

    @pl.when(s == 0)
    def _():
        wo_bf[...] = wo_ref[0].astype(BF16)
        h2_buf[...] = jnp.zeros(h2_buf.shape, BF16)
        logit_buf[...] = jnp.zeros(logit_buf.shape, F32)

    pos_in_seq = (i - PROMPT_TILES) % TILES_PER_SAMPLE
    is_first = (i < PROMPT_TILES) | (pos_in_seq == 0)
    is_last = (i < PROMPT_TILES) | (pos_in_seq == TILES_PER_SAMPLE - 1)
    ext_ref[0:CONV_PAD, :] = jnp.where(is_first, 0.0, up_ref[TM - CONV_PAD:TM, :])
    ext_ref[CONV_PAD:CONV_PAD + TM, :] = uc_ref[...]
    ext_ref[CONV_PAD + TM:2 * CONV_PAD + TM, :] = jnp.where(is_last, 0.0, un_ref[0:CONV_PAD, :])
    acc = None
    first_off = CONV_PAD - CONV_K // 2
    for b in range(SUBLANES):
        taps = [j for j in range(CONV_K) if (first_off + j) % SUBLANES == b]
        if not taps:
            continue
        reach = max(first_off + j - b for j in taps)
        if b:
            shift_ref[b, 0:TM + reach, :] = ext_ref[b:b + TM + reach, :]
        for j in taps:
            a8 = first_off + j - b
            window = shift_ref[b, a8:a8 + TM, :] if b else ext_ref[a8:a8 + TM, :]
            term = window * cw_ref[0, j:j + 1, :]
            acc = term if acc is None else acc + term
    conv = acc + cb_ref[0]
    wide = lambda t: jnp.concatenate([t] * (CONV_CH // LANES), axis=1)
    cc = conv - wide(_lane_sums(conv, CONV_CH) * (1.0 / CONV_CH))
    var = _lane_sums(cc * cc, CONV_CH) * (1.0 / CONV_CH)
    o_conv = _silu(cc * wide(lax.rsqrt(var + EPS)) * lg_ref[0] + lb_ref[0])

    is_prompt = i < PROMPT_TILES
    mix = jnp.concatenate([jnp.where(is_prompt, omp_ref[p], oms_ref[p]) for p in range(MLA_HEADS // 2)]
                          + [o_conv.astype(BF16), jnp.where(is_prompt, onp_ref[...], ons_ref[...])],
                          axis=-1)
    att = _dot(mix, wo_bf[...])
    gate1 = mod_ref[0, :, 2 * D:3 * D]
    x1 = x_ref[...] + gate1 * att
    x1_out[...] = x1
    shift2 = mod_ref[0, :, 3 * D:4 * D]
    scale2 = mod_ref[0, :, 4 * D:5 * D]
    h2 = _rms(x1, D) * n2g_ref[0] * (1.0 + scale2) + shift2

    logits = lax.dot_general(wr_ref[...], h2, (((1,), (1,)), ((), ())), preferred_element_type=F32,
                             precision=lax.Precision.HIGHEST)
    h2_prev = h2_buf[(s + 1) % 2]
    logits_prev = logit_buf[(s + 1) % 2]
    h2_buf[s % 2] = h2.astype(BF16)
    logit_buf[s % 2] = logits

    e1, e2, w1, w2 = _route(jax.nn.sigmoid(logits_prev), br_ref[...])
    pos1, pos2, start, padded = _sort_tile(e1, e2)
    st_out[0] = start[:, 0:LANES]
    pd_out[0] = padded[:, 0:LANES]
    r = lax.broadcasted_iota(jnp.int32, (TILE_CAP, TM), 0)
    perm = ((r == pos1) | (r == pos2)).astype(BF16)
    xs_out[...] = _dot(perm, h2_prev).astype(BF16)
    sub = lax.broadcasted_iota(jnp.int32, (LANES, TM), 0)
    packed = jnp.where(sub == 0, pos1.astype(F32),
                       jnp.where(sub == 1, pos2.astype(F32),
                                 jnp.where(sub == 2, w1, jnp.where(sub == 3, w2, 0.0))))
    pc_out[...] = packed.T


def _post(layer, x, om_p, om_s, u, on_p, on_s, mods3, prm):
    cur = lambda s: jnp.minimum(s, N_TILES - 1)
    prev = lambda s: jnp.maximum(s - 1, 0)
    row = lambda s: (cur(s), 0)
    row2 = lambda s: (prev(s), 0)
    lay3 = lambda s: (layer, 0, 0)
    return pl.pallas_call(
        _post_kernel,
        grid=(N_TILES + 1,),
        in_specs=[
            pl.BlockSpec((TM, D), row),
            pl.BlockSpec((MLA_HEADS // 2, TM, LANES), lambda s: (0, jnp.minimum(s, PROMPT_TILES - 1), 0)),
            pl.BlockSpec((MLA_HEADS // 2, TM, LANES), lambda s: (0, jnp.maximum(cur(s) - PROMPT_TILES, 0), 0)),
            pl.BlockSpec((TM, CONV_CH), lambda s: (jnp.maximum(cur(s) - 1, 0), 0)),
            pl.BlockSpec((TM, CONV_CH), row),
            pl.BlockSpec((TM, CONV_CH), lambda s: (jnp.minimum(s + 1, N_TILES - 1), 0)),
            pl.BlockSpec((TM, NA_WIDTH), lambda s: (jnp.minimum(s, PROMPT_TILES - 1), 0)),
            pl.BlockSpec((TM, NA_WIDTH), lambda s: (jnp.maximum(cur(s) - PROMPT_TILES, 0), 0)),
            pl.BlockSpec((1, 1, MOD_W), lambda s: (layer * COND_ROWS + _mod_row(cur(s)), 0, 0)),
            pl.BlockSpec((1, CONV_K, CONV_CH), lay3),
            pl.BlockSpec((1, 1, CONV_CH), lay3),
            pl.BlockSpec((1, 1, CONV_CH), lay3),
            pl.BlockSpec((1, 1, CONV_CH), lay3),
            pl.BlockSpec((1, D, D), lay3),
            pl.BlockSpec((1, 1, D), lay3),
            pl.BlockSpec((N_EXPERTS, D), lambda s: (0, 0)),
            pl.BlockSpec((N_EXPERTS, TM), lambda s: (0, 0)),
        ],
        out_specs=[pl.BlockSpec((TM, D), row), pl.BlockSpec((TILE_CAP, D), row2),
                   pl.BlockSpec((TM, LANES), row2),
                   pl.BlockSpec((1, N_EXPERTS, LANES), lambda s: (prev(s), 0, 0)),
                   pl.BlockSpec((1, N_EXPERTS, LANES), lambda s: (prev(s), 0, 0))],
        out_shape=[jax.ShapeDtypeStruct((T_ALL, D), F32),
                   jax.ShapeDtypeStruct((N_TILES * TILE_CAP, D), BF16),
                   jax.ShapeDtypeStruct((T_ALL, LANES), F32),
                   jax.ShapeDtypeStruct((N_TILES, N_EXPERTS, LANES), jnp.int32),
                   jax.ShapeDtypeStruct((N_TILES, N_EXPERTS, LANES), jnp.int32)],
        scratch_shapes=[pltpu.VMEM((D, D), BF16), pltpu.VMEM((TM + 2 * CONV_PAD, CONV_CH), F32),
                        pltpu.VMEM((SUBLANES, TM + 2 * CONV_PAD, CONV_CH), F32),
                        pltpu.VMEM((2, TM, D), BF16), pltpu.VMEM((2, N_EXPERTS, TM), F32)],
        compiler_params=_cparams(("arbitrary",)),
        name="post_l%d" % layer,
    )(x, om_p, om_s, u, u, u, on_p, on_s, mods3, prm["conv_w"], prm["conv_b"], prm["conv_ln_g"], prm["conv_ln_b"],
      prm["w_out"], prm["norm2_g"], prm["w_router_t"], prm["b_router_t"])


def _for_each_piece(n, fn):
    def pieces(sizes):
        for sz in sizes:
            @pl.when((n & sz) != 0)
            def _():
                fn(n & (-2 * sz), sz)

    n_big = len(RUN_SIZES) - 2

    @pl.when(n >= RUN_SIZES[n_big - 1])
    def _():
        pieces(RUN_SIZES[:n_big])

    pieces(RUN_SIZES[n_big:])


def _expert_kernel(st_ref, pd_ref, xs_hbm, wg_ref, wu_ref, wd_ref, ys_hbm, xy, wg_bf, wu_bf, wd_bf, zeros_buf,
                   sems):
    e = pl.program_id(0)
    slot = e % 2
    other = 1 - slot
    sem_tail = sems.at[4]

    def start_all(ex, sl, inbound):
        def body(j, cur):
            start = st_ref[j * N_EXPERTS + ex]
            n = pd_ref[j * N_EXPERTS + ex]

            def piece(off, sz):
                rows_hbm = pl.ds(pl.multiple_of(j * TILE_CAP + start + off, RUN_ALIGN), sz)
                rows_buf = pl.ds(pl.multiple_of(cur + off, RUN_ALIGN), sz)
                if inbound:
                    pltpu.make_async_copy(xs_hbm.at[rows_hbm], xy.at[sl, rows_buf], sems.at[sl]).start()
                else:
                    pltpu.make_async_copy(xy.at[sl, rows_buf], ys_hbm.at[rows_hbm], sems.at[2 + sl]).start()

            _for_each_piece(n, piece)
            return cur + n
        lax.fori_loop(0, N_TILES, body, 0)

    def expert_rows(ex):
        return lax.fori_loop(0, N_TILES, lambda j, c: c + pd_ref[j * N_EXPERTS + ex], 0, unroll=True)

    def wait_all(ex, sl, inbound):
        total = expert_rows(ex)
        for sz in WAIT_SIZES:
            @pl.when((total & sz) != 0)
            def _():
                rows = pl.ds(0, sz)
                if inbound:
                    pltpu.make_async_copy(xs_hbm.at[rows], xy.at[sl, rows], sems.at[sl]).wait()
                else:
                    pltpu.make_async_copy(xy.at[sl, rows], ys_hbm.at[rows], sems.at[2 + sl]).wait()
        return total

    def compute_chunks(first, last):
        def chunk(c, carry):
            rows = pl.ds(pl.multiple_of(c * MOE_CHUNK, MOE_CHUNK), MOE_CHUNK)
            x = xy[slot, rows, :]
            hid = _silu(_dot(x, wg_bf[...])) * _dot(x, wu_bf[...])
            xy[slot, rows, :] = _dot(hid.astype(BF16), wd_bf[...]).astype(BF16)
            return carry
        lax.fori_loop(first, last, chunk, 0)

    @pl.when(e == 0)
    def _():
        xy[...] = jnp.zeros(xy.shape, BF16)
        zeros_buf[...] = jnp.zeros(zeros_buf.shape, BF16)
        start_all(e, slot, True)

    wg_bf[...] = wg_ref[0, 0].astype(BF16)
    wu_bf[...] = wu_ref[0, 0].astype(BF16)
    wd_bf[...] = wd_ref[0, 0].astype(BF16)
    n_chunks = (wait_all(e, slot, True) + (MOE_CHUNK - 1)) // MOE_CHUNK

    mid = (n_chunks + 1) // 2
    compute_chunks(0, mid)

    @pl.when(e > 0)
    def _():
        wait_all(e - 1, other, False)

    @pl.when(e < N_EXPERTS - 1)
    def _():
        start_all(e + 1, other, True)

    compute_chunks(mid, n_chunks)
    start_all(e, slot, False)

    @pl.when(e == N_EXPERTS - 1)
    def _():
        wait_all(e, slot, False)

        def tail_copy(j, off, sz):
            used = st_ref[j * N_EXPERTS + e] + pd_ref[j * N_EXPERTS + e]
            rows_hbm = pl.ds(pl.multiple_of(j * TILE_CAP + used + off, RUN_ALIGN), sz)
            return pltpu.make_async_copy(zeros_buf.at[pl.ds(0, sz)], ys_hbm.at[rows_hbm], sem_tail)

        def tail_len(j):
            return TILE_CAP - (st_ref[j * N_EXPERTS + e] + pd_ref[j * N_EXPERTS + e])

        def t_start(j, c):
            _for_each_piece(tail_len(j), lambda off, sz: tail_copy(j, off, sz).start())
            return c

        def t_wait(j, c):
            _for_each_piece(tail_len(j), lambda off, sz: tail_copy(j, off, sz).wait())
            return c

        lax.fori_loop(0, N_TILES, t_start, 0)
        lax.fori_loop(0, N_TILES, t_wait, 0)


def _experts(layer, starts, pads, xs, prm):
    wspec = lambda shape: pl.BlockSpec((1, 1) + shape, lambda e, st, pd: (layer, e, 0, 0))
    return pl.pallas_call(
        _expert_kernel,
        grid_spec=pltpu.PrefetchScalarGridSpec(
            num_scalar_prefetch=2,
            grid=(N_EXPERTS,),
            in_specs=[pl.BlockSpec(memory_space=pl.ANY), wspec((D, EXPERT_FF)), wspec((D, EXPERT_FF)),
                      wspec((EXPERT_FF, D))],
            out_specs=pl.BlockSpec(memory_space=pl.ANY),
            scratch_shapes=[pltpu.VMEM((2, XY_ROWS, D), BF16), pltpu.VMEM((D, EXPERT_FF), BF16),
                            pltpu.VMEM((D, EXPERT_FF), BF16), pltpu.VMEM((EXPERT_FF, D), BF16),
                            pltpu.VMEM((RUN_SIZES[0], D), BF16),
                            pltpu.SemaphoreType.DMA((5,))]),
        out_shape=jax.ShapeDtypeStruct((N_TILES * TILE_CAP, D), BF16),
        compiler_params=pltpu.CompilerParams(dimension_semantics=("arbitrary",),
                                             vmem_limit_bytes=EXPERT_VMEM_LIMIT),
        name="experts_l%d" % layer,
    )(starts, pads, xs, prm["w_e_gate"], prm["w_e_up"], prm["w_e_down"])


def _final_kernel(x1_ref, ys_ref, pc_ref, mod_ref, op_ref, os_ref):
    x = _moe_combine(x1_ref, ys_ref, pc_ref, mod_ref)
    i = pl.program_id(0)

    @pl.when(i < PROMPT_TILES)
    def _():
        op_ref[...] = x

    @pl.when(i >= PROMPT_TILES)
    def _():
        os_ref[...] = x


def _final(x1, ys, pcol, mods3):
    row = lambda i: (i, 0)
    return pl.pallas_call(
        _final_kernel,
        grid=(N_TILES,),
        in_specs=[pl.BlockSpec((TM, D), row), pl.BlockSpec((TILE_CAP, D), row), pl.BlockSpec((TM, LANES), row),
                  pl.BlockSpec((1, 1, MOD_W), lambda i: ((DEPTH - 1) * COND_ROWS + _mod_row(i), 0, 0))],
        out_specs=[pl.BlockSpec((TM, D), lambda i: (jnp.minimum(i, PROMPT_TILES - 1), 0)),
                   pl.BlockSpec((TM, D), lambda i: (jnp.maximum(i - PROMPT_TILES, 0), 0))],
        out_shape=[jax.ShapeDtypeStruct((T_PROMPT, D), F32), jax.ShapeDtypeStruct((T_SAMPLE, D), F32)],
        compiler_params=_cparams(("arbitrary",)),
        name="final_residual",
    )(x1, ys, pcol, mods3)


def _rope_tables():
    t = np.arange(SAMPLE_LEN)
    n_freq = QK_ROPE // 4
    inv_freq = ROPE_THETA ** (-np.arange(n_freq, dtype=np.float32) / n_freq)
    row = (t // GRID_W).astype(np.float32)
    col = (t % GRID_W).astype(np.float32)
    ang = jnp.concatenate([jnp.asarray(row)[:, None] * inv_freq, jnp.asarray(col)[:, None] * inv_freq],
                          axis=-1)
    cos_p = jnp.repeat(jnp.cos(ang), 2, axis=-1)
    sin_p = jnp.repeat(jnp.sin(ang), 2, axis=-1) * jnp.tile(jnp.asarray([-1.0, 1.0], F32), QK_ROPE // 2)
    pad_l = jnp.ones((SAMPLE_LEN, QK_NOPE), F32)
    pad_r = jnp.ones((SAMPLE_LEN, LANES - QK_HEAD), F32)
    cos = jnp.concatenate([pad_l, cos_p, pad_r], axis=-1)
    sin = jnp.concatenate([0 * pad_l, sin_p, 0 * pad_r], axis=-1)
    ident_c = jnp.ones((TM, LANES), F32)
    return jnp.concatenate([ident_c, cos], axis=0), jnp.concatenate([0 * ident_c, sin], axis=0)


def _pad_lanes(x, lo, total):
    pad = [(0, 0)] * (x.ndim - 1) + [(lo, total - lo - x.shape[-1])]
    return jnp.pad(x, pad)


def kernel(x_prompt, x_sample, cache_mla_ckv, cache_mla_krope, cache_na_k, cache_na_v, c, c_ctx, w_ada, b_ada, norm1_g, norm2_g, w_in, q_a_g, kv_a_g, w_q_up, w_kv_up, mla_q_norm_g, mla_k_norm_g, conv_dw_w, conv_dw_b, conv_ln_g, conv_ln_b, na_q_norm_g, na_k_norm_g, na_rpb, w_out, w_router, b_router, w_e_gate, w_e_up, w_e_down):
    cos, sin = _rope_tables()
    vec = lambda a: a.reshape(DEPTH, 1, a.shape[-1])
    prm = {
        "norm1_g": vec(norm1_g), "norm2_g": vec(norm2_g), "w_in_t": jnp.swapaxes(w_in, 1, 2), "q_a_g": vec(q_a_g),
        "kv_a_g": vec(kv_a_g), "w_q_up": w_q_up, "w_kv_up": w_kv_up,
        "q_norm_g": vec(_pad_lanes(mla_q_norm_g, 0, LANES)), "k_norm_g": vec(_pad_lanes(mla_k_norm_g, 0, LANES)),
        "na_q_g": vec(jnp.tile(na_q_norm_g, (1, 2))), "na_k_g": vec(jnp.tile(na_k_norm_g, (1, 2))),
        "cos": cos, "sin": sin,
        "conv_w": conv_dw_w, "conv_b": vec(conv_dw_b), "conv_ln_g": vec(conv_ln_g), "conv_ln_b": vec(conv_ln_b),
        "w_out": w_out, "w_router_t": w_router.T,
        "b_router_t": jnp.broadcast_to(b_router[:, None], (N_EXPERTS, TM)),
        "w_e_gate": w_e_gate, "w_e_up": w_e_up, "w_e_down": w_e_down,
    }

    cond = jnp.concatenate([c_ctx[None], c, jnp.zeros((COND_ROWS - N_COND, D), F32)], axis=0)
    mods3 = _modulation(cond.T, w_ada, b_ada).reshape(DEPTH * COND_ROWS, 1, MOD_W)

    kc, vc = _ctx_kv(cache_mla_ckv, jnp.swapaxes(cache_mla_krope, 2, 3), prm)
    bias = _na_bias(na_rpb)
    cache_k = cache_na_k.transpose(0, 1, 3, 4, 2).reshape(N_SAMPLE, DEPTH, NA_WIDTH, PAST)
    cache_v = cache_na_v.transpose(0, 1, 3, 4, 2).reshape(N_SAMPLE, DEPTH, NA_WIDTH, PAST)

    x = x1 = ys = pcol = None
    new_ckv, new_kr, new_nak, new_nav = [], [], [], []
    for layer in range(DEPTH):
        if layer == 0:
            outs = _pre(layer, False, (x_prompt.reshape(T_PROMPT, D), x_sample.reshape(T_SAMPLE, D)), mods3, prm)
        else:
            outs = _pre(layer, True, (x1, ys, pcol), mods3, prm)
        x, q, k, v, ckv_n, kr, u, nq, nk, nv, nkf, nvf = outs
        om_p, on_p = _prompt_attn(q, k, v, nq, nk, nv)
        om_s = _sample_mla(layer, q, k, v, kc, vc)
        on_s = _sample_na(layer, nq, nk, nv, cache_k, cache_v, bias)
        x1, xs, pcol, run_start, run_len = _post(layer, x, om_p, om_s, u, on_p, on_s, mods3, prm)
        ys = _experts(layer, run_start[:, :, 0].reshape(-1), run_len[:, :, 0].reshape(-1), xs, prm)
        new_ckv.append(ckv_n.reshape(N_PROMPT, PROMPT_LEN, KV_LORA))
        new_kr.append(kr)
        new_nak.append(nkf)
        new_nav.append(nvf)
    y_prompt, y_sample = _final(x1, ys, pcol, mods3)
    return (y_prompt.reshape(N_PROMPT, PROMPT_LEN, D), y_sample.reshape(N_SAMPLE, SAMPLE_LEN, D),
            jnp.stack(new_ckv, axis=1), jnp.stack(new_kr, axis=1).transpose(0, 1, 3, 2),
            jnp.stack(new_nak, axis=1).transpose(0, 1, 4, 2, 3), jnp.stack(new_nav, axis=1).transpose(0, 1, 4, 2, 3))
```

```python
import functools

import numpy as np
import jax
import jax.numpy as jnp
from jax import lax
from jax.experimental import pallas as pl
from jax.experimental.pallas import tpu as pltpu

F32 = jnp.float32
BF16 = jnp.bfloat16

D = 1024
N_PROMPT = 16
PROMPT_LEN = 256
N_SAMPLE = 4
SAMPLE_LEN = 1024
DEPTH = 4
PAST = 512
GRID_W = 64
GRID_ROWS = SAMPLE_LEN // GRID_W
MLA_HEADS = 8
QK_NOPE = 64
QK_ROPE = 32
QK_HEAD = QK_NOPE + QK_ROPE
V_HEAD = 64
Q_LORA = 256
KV_LORA = 128
CONV_CH = 256
CONV_K = 31
NA_HEADS = 4
NA_DIM = 64
NA_WIDTH = NA_HEADS * NA_DIM
NA_WIN_H = 8
NA_WIN_W = 16
N_EXPERTS = 16
N_GROUPS = 4
EPG = 4
EXPERT_FF = 512
ROPE_THETA = 10000.0
EPS = 1e-6
NEG_INF = -1e30
MLA_SCALE = QK_HEAD ** -0.5
NA_SCALE = NA_DIM ** -0.5

LANES = 128
SUBLANES = 8
TM = 256
T_PROMPT = N_PROMPT * PROMPT_LEN
T_SAMPLE = N_SAMPLE * SAMPLE_LEN
T_ALL = T_PROMPT + T_SAMPLE
N_TILES = T_ALL // TM
PROMPT_TILES = T_PROMPT // TM
TILES_PER_SAMPLE = SAMPLE_LEN // TM
N_COND = 1 + N_SAMPLE
COND_ROWS = 8
MOD_W = 6 * D

COL_Q = 0
COL_KV = Q_LORA
COL_KR = COL_KV + KV_LORA
COL_CONV = COL_KR + LANES
COL_NA = COL_CONV + 2 * CONV_CH
IN_COLS_P = COL_NA + 3 * NA_WIDTH
IN_COLS = IN_COLS_P - (LANES - QK_ROPE)

NA_KEY_ROWS = 12
NA_KEYS = NA_KEY_ROWS * GRID_W

VMEM_LIMIT = 56 * 1024 * 1024

RUN_ALIGN = 16
TILE_CAP = 768
RUN_SIZES = (256, 128, 64, 32, 16)
MOE_CHUNK = 256
XY_ROWS = N_TILES * TM
WAIT_SIZES = (8192, 4096, 2048, 1024, 512, 256, 128, 64, 32, 16)
EXPERT_VMEM_LIMIT = 60 * 1024 * 1024


def _cparams(sem):
    return pltpu.CompilerParams(dimension_semantics=sem, vmem_limit_bytes=VMEM_LIMIT)


def _mod_row(i):
    return jnp.where(i < PROMPT_TILES, 0, 1 + (i - PROMPT_TILES) // TILES_PER_SAMPLE)


def _rope_block(i):
    return jnp.where(i < PROMPT_TILES, 0, 1 + (i - PROMPT_TILES) % TILES_PER_SAMPLE)


def _dot(a, b):
    return jnp.dot(a, b, preferred_element_type=F32)


def _dot_t(a, b):
    return lax.dot_general(a, b, (((1,), (1,)), ((), ())), preferred_element_type=F32)


def _silu(x):
    return x * jax.nn.sigmoid(x)


def _lane_is_low(shape):
    return lax.broadcasted_iota(jnp.int32, shape, len(shape) - 1) < (LANES // 2)


MOD_BLK = 768


def _mod_kernel(ct_ref, w_ref, b_ref, o_ref, cb_ref):
    first = (pl.program_id(0) == 0) & (pl.program_id(1) == 0)

    @pl.when(first)
    def _():
        s = _silu(ct_ref[...])
        for m in range(N_COND):
            cb_ref[m] = jnp.broadcast_to(s[:, m:m + 1], (D, LANES))

    for j in range(MOD_BLK // LANES):
        w = w_ref[0, :, j * LANES:(j + 1) * LANES]
        rows = [jnp.sum(w * cb_ref[m], axis=0, keepdims=True) for m in range(N_COND)]
        rows.append(jnp.zeros((COND_ROWS - N_COND, LANES), F32))
        o_ref[0, :, j * LANES:(j + 1) * LANES] = (
            jnp.concatenate(rows, axis=0) + b_ref[0, :, j * LANES:(j + 1) * LANES])


def _modulation(cond_t, w_ada, b_ada):
    nb = MOD_W // MOD_BLK
    return pl.pallas_call(
        _mod_kernel,
        grid=(DEPTH, nb),
        in_specs=[
            pl.BlockSpec((D, COND_ROWS), lambda l, j: (0, 0)),
            pl.BlockSpec((1, D, MOD_BLK), lambda l, j: (l, 0, j)),
            pl.BlockSpec((1, 1, MOD_BLK), lambda l, j: (l, 0, j)),
        ],
        out_specs=pl.BlockSpec((1, COND_ROWS, MOD_BLK), lambda l, j: (l, 0, j)),
        out_shape=jax.ShapeDtypeStruct((DEPTH, COND_ROWS, MOD_W), F32),
        scratch_shapes=[pltpu.VMEM((N_COND, D, LANES), F32)],
        compiler_params=_cparams(("arbitrary", "arbitrary")),
        name="modulation",
    )(cond_t, w_ada, b_ada.reshape(DEPTH, 1, MOD_W))


def _lane_sums(sq, group):
    k = sq.shape[1]
    hi = sq.astype(BF16)
    lo = (sq - hi.astype(F32)).astype(BF16)
    if group == k:
        sel = jnp.ones((k, LANES), BF16)
    else:
        r = lax.broadcasted_iota(jnp.int32, (k, LANES), 0)
        c = lax.broadcasted_iota(jnp.int32, (k, LANES), 1)
        sel = ((r // group) == (c // group)).astype(BF16)
    return _dot(hi, sel) + _dot(lo, sel)


def _rms(x, width):
    rinv = lax.rsqrt(_lane_sums(x * x, x.shape[1]) * (1.0 / width) + EPS)
    reps = x.shape[1] // LANES
    return x * (rinv if reps == 1 else jnp.concatenate([rinv] * reps, axis=1))


def _rope(x, cos, sin):
    lane = lax.broadcasted_iota(jnp.int32, x.shape, 1)
    partner = jnp.where((lane & 1) == 0, pltpu.roll(x, LANES - 1, 1), pltpu.roll(x, 1, 1))
    return x * cos + partner * sin


def _pair_rms(x, gain):
    ms = _lane_sums(x * x, NA_DIM) * (1.0 / NA_DIM)
    return x * lax.rsqrt(ms + EPS) * gain


def _softmax_pv(scores, values, feature_major=()):
    m = scores[0].max(axis=-1, keepdims=True)
    for s in scores[1:]:
        m = jnp.maximum(m, s.max(axis=-1, keepdims=True))
    acc = None
    den = None
    for idx, (s, v) in enumerate(zip(scores, values)):
        p = jnp.exp(s - m)
        d = jnp.sum(p, axis=-1, keepdims=True)
        o = _dot_t(p.astype(BF16), v) if idx in feature_major else _dot(p.astype(BF16), v)
        acc = o if acc is None else acc + o
        den = d if den is None else den + d
    return acc * (1.0 / den)


def _moe_combine(x1_ref, ys_ref, pc_ref, modp_ref):
    pc = pc_ref[...]
    lane = lax.broadcasted_iota(jnp.int32, (TM, TILE_CAP), 1)
    sel = (jnp.where(lane == pc[:, 0:1].astype(jnp.int32), pc[:, 2:3], 0.0)
           + jnp.where(lane == pc[:, 1:2].astype(jnp.int32), pc[:, 3:4], 0.0))
    gate2 = modp_ref[0, :, 5 * D:6 * D]
    return x1_ref[...] + gate2 * _dot(sel.astype(BF16), ys_ref[...])


def _place_w_in(w_ref, w_bf):
    off_kv = Q_LORA + KV_LORA
    off_kr = off_kv + QK_ROPE
    w_bf[0:off_kv, :] = w_ref[0, 0:off_kv, :].astype(BF16)
    w_bf[COL_KR:COL_KR + LANES, :] = jnp.zeros((LANES, D), BF16)
    w_bf[COL_KR + QK_NOPE:COL_KR + QK_HEAD, :] = w_ref[0, off_kv:off_kr, :].astype(BF16)
    w_bf[COL_CONV:IN_COLS_P, :] = w_ref[0, off_kr:off_kr + (IN_COLS_P - COL_CONV), :].astype(BF16)


def _place_w_q(w_ref, w_bf):
    w_bf[...] = jnp.zeros(w_bf.shape, BF16)
    for hd in range(MLA_HEADS):
        w_bf[:, hd * LANES:hd * LANES + QK_HEAD] = w_ref[0, :, hd * QK_HEAD:(hd + 1) * QK_HEAD].astype(BF16)


def _place_w_kv(w_ref, wk_bf, wv_bf):
    wk_bf[...] = jnp.zeros(wk_bf.shape, BF16)
    for hd in range(MLA_HEADS):
        c0 = hd * (QK_NOPE + V_HEAD)
        wk_bf[:, hd * LANES:hd * LANES + QK_NOPE] = w_ref[0, :, c0:c0 + QK_NOPE].astype(BF16)
        wv_bf[:, hd * V_HEAD:(hd + 1) * V_HEAD] = w_ref[0, :, c0 + QK_NOPE:c0 + QK_NOPE + V_HEAD].astype(BF16)


def _pre_kernel(has_moe, *refs):
    if has_moe:
        x1_ref, ys_ref, pc_ref, modp_ref = refs[:4]
        refs = refs[4:]
    else:
        xp_ref, xs_ref = refs[:2]
        refs = refs[2:]
    (mod_ref, n1g_ref, win_ref, qag_ref, kvag_ref, wq_ref, wkv_ref, qng_ref, kng_ref,
     naqg_ref, nakg_ref, cos_ref, sin_ref) = refs[:13]
    refs = refs[13:]
    xo_ref = refs[0]
    refs = refs[1:]
    (q_out, k_out, v_out, ckv_out, kr_out, u_out, naq_out, nak_out, nav_out, nakf_out, navf_out,
     win_bf, wq_bf, wk_bf, wv_bf, proj_buf) = refs

    s = pl.program_id(0)
    i = jnp.minimum(s, N_TILES - 1)

    @pl.when(s == 0)
    def _():
        _place_w_in(win_ref, win_bf)
        _place_w_q(wq_ref, wq_bf)
        _place_w_kv(wkv_ref, wk_bf, wv_bf)
        proj_buf[...] = jnp.zeros(proj_buf.shape, F32)

    if has_moe:
        x = _moe_combine(x1_ref, ys_ref, pc_ref, modp_ref)
    else:
        x = jnp.where(i < PROMPT_TILES, xp_ref[...], xs_ref[...])
    xo_ref[...] = x
    shift = mod_ref[0, :, 0:D]
    scale = mod_ref[0, :, D:2 * D]
    h = _rms(x, D) * n1g_ref[0] * (1.0 + scale) + shift
    proj = proj_buf[(s + 1) % 2]
    proj_buf[s % 2] = _dot_t(h.astype(BF16), win_bf[...])

    cos = cos_ref[...]
    sin = sin_ref[...]

    q_lat = _rms(proj[:, COL_Q:COL_Q + Q_LORA], Q_LORA) * qag_ref[0]
    q = _dot(q_lat.astype(BF16), wq_bf[...])
    for hd in range(MLA_HEADS):
        qh = q[:, hd * LANES:(hd + 1) * LANES]
        qh = _rope(_rms(qh, QK_HEAD) * qng_ref[0], cos, sin)
        q_out[hd] = (qh * MLA_SCALE).astype(BF16)

    ckv_n = _rms(proj[:, COL_KV:COL_KV + KV_LORA], KV_LORA) * kvag_ref[0]
    kr = proj[:, COL_KR:COL_KR + LANES]
    ckv_b = ckv_n.astype(BF16)
    k_nope = _dot(ckv_b, wk_bf[...])
    kr_rot = _rope(kr * kng_ref[0], cos, sin)
    kr_sq = kr * kr
    for hd in range(MLA_HEADS):
        kn = k_nope[:, hd * LANES:(hd + 1) * LANES]
        rinv = lax.rsqrt(_lane_sums(kn * kn + kr_sq, LANES) * (1.0 / QK_HEAD) + EPS)
        k_out[hd] = ((kn * kng_ref[0] + kr_rot) * rinv).astype(BF16)
    v = _dot(ckv_b, wv_bf[...])
    for p in range(MLA_HEADS // 2):
        v_out[p] = v[:, p * LANES:(p + 1) * LANES].astype(BF16)

    a = proj[:, COL_CONV:COL_CONV + CONV_CH]
    g = proj[:, COL_CONV + CONV_CH:COL_CONV + 2 * CONV_CH]
    u_out[...] = a * jax.nn.sigmoid(g)

    k_slabs, v_slabs = [], []
    for p in range(NA_HEADS // 2):
        sl = slice(p * LANES, (p + 1) * LANES)
        qn = _pair_rms(proj[:, COL_NA + p * LANES:COL_NA + (p + 1) * LANES], naqg_ref[0])
        kn = _pair_rms(proj[:, COL_NA + NA_WIDTH + p * LANES:COL_NA + NA_WIDTH + (p + 1) * LANES],
                       nakg_ref[0])
        vn = proj[:, COL_NA + 2 * NA_WIDTH + p * LANES:COL_NA + 2 * NA_WIDTH + (p + 1) * LANES]
        naq_out[:, sl] = (qn * NA_SCALE).astype(BF16)
        nak_out[:, sl] = kn.astype(BF16)
        nav_out[:, sl] = vn.astype(BF16)
        k_slabs.append(kn)
        v_slabs.append(vn)

    kr_t = kr.T
    nak_t = jnp.concatenate(k_slabs, axis=1).T.reshape(NA_HEADS, NA_DIM, TM)
    nav_t = jnp.concatenate(v_slabs, axis=1).T.reshape(NA_HEADS, NA_DIM, TM)

    @pl.when(s <= PROMPT_TILES)
    def _():
        ckv_out[...] = ckv_n
        kr_out[0] = kr_t[QK_NOPE:QK_HEAD, :]
        nakf_out[0] = nak_t
        navf_out[0] = nav_t


def _pre(layer, has_moe, x_or_parts, mods3, prm):
    cur = lambda s: jnp.minimum(s, N_TILES - 1)
    prev = lambda s: jnp.maximum(s - 1, 0)
    row = lambda s: (cur(s), 0)
    row2 = lambda s: (prev(s), 0)
    lay3 = lambda s: (layer, 0, 0)
    in_specs = []
    args = []
    if has_moe:
        x1, ys, pcol = x_or_parts
        in_specs += [pl.BlockSpec((TM, D), row), pl.BlockSpec((TILE_CAP, D), row),
                     pl.BlockSpec((TM, LANES), row),
                     pl.BlockSpec((1, 1, MOD_W), lambda s: ((layer - 1) * COND_ROWS + _mod_row(cur(s)), 0, 0))]
        args += [x1, ys, pcol, mods3]
    else:
        x_prompt, x_sample = x_or_parts
        in_specs += [pl.BlockSpec((TM, D), lambda s: (jnp.minimum(s, PROMPT_TILES - 1), 0)),
                     pl.BlockSpec((TM, D), lambda s: (jnp.maximum(cur(s) - PROMPT_TILES, 0), 0))]
        args += [x_prompt, x_sample]
    in_specs += [
        pl.BlockSpec((1, 1, MOD_W), lambda s: (layer * COND_ROWS + _mod_row(cur(s)), 0, 0)),
        pl.BlockSpec((1, 1, D), lay3),
        pl.BlockSpec((1, IN_COLS, D), lay3),
        pl.BlockSpec((1, 1, Q_LORA), lay3),
        pl.BlockSpec((1, 1, KV_LORA), lay3),
        pl.BlockSpec((1, Q_LORA, MLA_HEADS * QK_HEAD), lay3),
        pl.BlockSpec((1, KV_LORA, MLA_HEADS * (QK_NOPE + V_HEAD)), lay3),
        pl.BlockSpec((1, 1, LANES), lay3),
        pl.BlockSpec((1, 1, LANES), lay3),
        pl.BlockSpec((1, 1, LANES), lay3),
        pl.BlockSpec((1, 1, LANES), lay3),
        pl.BlockSpec((TM, LANES), lambda s: (_rope_block(prev(s)), 0)),
        pl.BlockSpec((TM, LANES), lambda s: (_rope_block(prev(s)), 0)),
    ]
    args += [mods3, prm["norm1_g"], prm["w_in_t"], prm["q_a_g"], prm["kv_a_g"], prm["w_q_up"], prm["w_kv_up"],
             prm["q_norm_g"], prm["k_norm_g"], prm["na_q_g"], prm["na_k_g"], prm["cos"], prm["sin"]]
    out_specs = [pl.BlockSpec((TM, D), row)]
    out_shape = [jax.ShapeDtypeStruct((T_ALL, D), F32)]
    head3 = lambda n: pl.BlockSpec((n, TM, LANES), lambda s: (0, prev(s), 0))
    seq = lambda s: jnp.minimum(prev(s), PROMPT_TILES - 1)
    out_specs += [head3(MLA_HEADS), head3(MLA_HEADS), head3(MLA_HEADS // 2),
                  pl.BlockSpec((TM, KV_LORA), lambda s: (seq(s), 0)),
                  pl.BlockSpec((1, QK_ROPE, TM), lambda s: (seq(s), 0, 0)),
                  pl.BlockSpec((TM, CONV_CH), row2)] + [pl.BlockSpec((TM, NA_WIDTH), row2)] * 3
    out_specs += [pl.BlockSpec((1, NA_HEADS, NA_DIM, TM), lambda s: (seq(s), 0, 0, 0))] * 2
    out_shape += [
        jax.ShapeDtypeStruct((MLA_HEADS, T_ALL, LANES), BF16),
        jax.ShapeDtypeStruct((MLA_HEADS, T_ALL, LANES), BF16),
        jax.ShapeDtypeStruct((MLA_HEADS // 2, T_ALL, LANES), BF16),
        jax.ShapeDtypeStruct((T_PROMPT, KV_LORA), F32),
        jax.ShapeDtypeStruct((N_PROMPT, QK_ROPE, PROMPT_LEN), F32),
        jax.ShapeDtypeStruct((T_ALL, CONV_CH), F32),
        jax.ShapeDtypeStruct((T_ALL, NA_WIDTH), BF16),
        jax.ShapeDtypeStruct((T_ALL, NA_WIDTH), BF16),
        jax.ShapeDtypeStruct((T_ALL, NA_WIDTH), BF16),
        jax.ShapeDtypeStruct((N_PROMPT, NA_HEADS, NA_DIM, PROMPT_LEN), F32),
        jax.ShapeDtypeStruct((N_PROMPT, NA_HEADS, NA_DIM, PROMPT_LEN), F32),
    ]
    return pl.pallas_call(
        functools.partial(_pre_kernel, has_moe),
        grid=(N_TILES + 1,),
        in_specs=in_specs,
        out_specs=out_specs,
        out_shape=out_shape,
        scratch_shapes=[pltpu.VMEM((IN_COLS_P, D), BF16), pltpu.VMEM((Q_LORA, MLA_HEADS * LANES), BF16),
                        pltpu.VMEM((KV_LORA, MLA_HEADS * LANES), BF16),
                        pltpu.VMEM((KV_LORA, MLA_HEADS * V_HEAD), BF16),
                        pltpu.VMEM((2, TM, IN_COLS_P), F32)],
        compiler_params=_cparams(("arbitrary",)),
        name="pre_l%d" % layer,
    )(*args)


def _ctx_kernel(ckv_ref, kr_ref, wkv_ref, kng_ref, kc_out, vc_out, wk_bf, wv_bf):
    @pl.when(pl.program_id(1) == 0)
    def _():
        _place_w_kv(wkv_ref, wk_bf, wv_bf)

    ckv = ckv_ref[0, 0].astype(BF16)
    kr = jnp.concatenate([jnp.zeros((PAST, QK_NOPE), F32), kr_ref[0, 0].T,
                          jnp.zeros((PAST, LANES - QK_HEAD), F32)], axis=1)
    k_nope = _dot(ckv, wk_bf[...])
    for hd in range(MLA_HEADS):
        kh = k_nope[:, hd * LANES:(hd + 1) * LANES] + kr
        kc_out[0, 0, hd] = (_rms(kh, QK_HEAD) * kng_ref[0]).astype(BF16)
    v = _dot(ckv, wv_bf[...])
    for p in range(MLA_HEADS // 2):
        vc_out[0, 0, p] = v[:, p * LANES:(p + 1) * LANES].astype(BF16)


def _ctx_kv(cache_ckv, cache_krope, prm):
    return pl.pallas_call(
        _ctx_kernel,
        grid=(DEPTH, N_SAMPLE),
        in_specs=[
            pl.BlockSpec((1, 1, PAST, KV_LORA), lambda l, b: (b, l, 0, 0)),
            pl.BlockSpec((1, 1, QK_ROPE, PAST), lambda l, b: (b, l, 0, 0)),
            pl.BlockSpec((1, KV_LORA, MLA_HEADS * (QK_NOPE + V_HEAD)), lambda l, b: (l, 0, 0)),
            pl.BlockSpec((1, 1, LANES), lambda l, b: (l, 0, 0)),
        ],
        out_specs=[
            pl.BlockSpec((1, 1, MLA_HEADS, PAST, LANES), lambda l, b: (l, b, 0, 0, 0)),
            pl.BlockSpec((1, 1, MLA_HEADS // 2, PAST, LANES), lambda l, b: (l, b, 0, 0, 0)),
        ],
        out_shape=[
            jax.ShapeDtypeStruct((DEPTH, N_SAMPLE, MLA_HEADS, PAST, LANES), BF16),
            jax.ShapeDtypeStruct((DEPTH, N_SAMPLE, MLA_HEADS // 2, PAST, LANES), BF16),
        ],
        scratch_shapes=[pltpu.VMEM((KV_LORA, MLA_HEADS * LANES), BF16),
                        pltpu.VMEM((KV_LORA, MLA_HEADS * V_HEAD), BF16)],
        compiler_params=_cparams(("arbitrary", "arbitrary")),
        name="ctx_kv",
    )(cache_ckv, cache_krope, prm["w_kv_up"], prm["k_norm_g"])


N_DR = 2 * NA_WIN_H - 1
N_DC = 2 * NA_WIN_W - 1


def _bias_kernel(rpb_ref, o_ref, blk_ref):
    cq = lax.broadcasted_iota(jnp.int32, (GRID_W, GRID_W), 0)
    ck = lax.broadcasted_iota(jnp.int32, (GRID_W, GRID_W), 1)
    dc = jnp.clip(ck - cq + (NA_WIN_W - 1), 0, N_DC - 1)
    c0 = jnp.clip(cq - NA_WIN_W // 2, 0, GRID_W - NA_WIN_W)
    in_win = (ck >= c0) & (ck < c0 + NA_WIN_W)
    for dr in range(N_DR):
        acc = jnp.zeros((GRID_W, GRID_W), F32)
        for j in range(N_DC):
            acc = jnp.where(dc == j, rpb_ref[pl.program_id(0), pl.program_id(1), dr, j], acc)
        blk_ref[dr] = jnp.where(in_win, acc, NEG_INF)
    blk_ref[N_DR] = jnp.full((GRID_W, GRID_W), NEG_INF, F32)
    idx = _na_block_index()
    for i in range(TILES_PER_SAMPLE):
        for rq in range(TM // GRID_W):
            for pp in range(NA_KEY_ROWS // 2):
                pair = jnp.concatenate([blk_ref[int(idx[i, rq, 2 * pp])], blk_ref[int(idx[i, rq, 2 * pp + 1])]],
                                       axis=1)
                o_ref[0, 0, i, rq * GRID_W:(rq + 1) * GRID_W, pp * LANES:(pp + 1) * LANES] = pair


def _na_bias(na_rpb):
    return pl.pallas_call(
        _bias_kernel,
        grid=(DEPTH, NA_HEADS),
        in_specs=[pl.BlockSpec(memory_space=pltpu.SMEM)],
        out_specs=pl.BlockSpec((1, 1, TILES_PER_SAMPLE, TM, NA_KEYS), lambda l, h: (l, h, 0, 0, 0)),
        out_shape=jax.ShapeDtypeStruct((DEPTH, NA_HEADS, TILES_PER_SAMPLE, TM, NA_KEYS), F32),
        scratch_shapes=[pltpu.VMEM((N_DR + 1, GRID_W, GRID_W), F32)],
        compiler_params=_cparams(("arbitrary", "arbitrary")),
        name="na_bias",
    )(na_rpb)


def _na_block_index():
    idx = np.full((TILES_PER_SAMPLE, TM // GRID_W, NA_KEY_ROWS), N_DR, np.int32)
    for i in range(TILES_PER_SAMPLE):
        ks = _na_key_start_row(i)
        for rq in range(TM // GRID_W):
            r = i * (TM // GRID_W) + rq
            r0 = min(max(r - NA_WIN_H // 2, 0), GRID_ROWS - NA_WIN_H)
            for rk in range(NA_KEY_ROWS):
                kr = ks + rk
                if r0 <= kr < r0 + NA_WIN_H:
                    idx[i, rq, rk] = kr - r + (NA_WIN_H - 1)
    return idx


def _na_key_start_row(i):
    return 0 if i < TILES_PER_SAMPLE // 2 else GRID_ROWS - NA_KEY_ROWS


def _prompt_attn_kernel(q_ref, k_ref, v_ref, nq_ref, nk_ref, nv_ref, om_ref, on_ref):
    low = _lane_is_low((PROMPT_LEN, LANES))
    for p in range(MLA_HEADS // 2):
        outs = []
        for hh in range(2):
            hd = 2 * p + hh
            s = _dot_t(q_ref[hd], k_ref[hd])
            outs.append(_softmax_pv([s], [v_ref[p]]))
        om_ref[p] = jnp.where(low, outs[0], outs[1]).astype(BF16)
    for p in range(NA_HEADS // 2):
        sl = slice(p * LANES, (p + 1) * LANES)
        qp = nq_ref[:, sl]
        kp = nk_ref[:, sl]
        vp = nv_ref[:, sl]
        outs = []
        for hh in range(2):
            qm = jnp.where(low if hh == 0 else ~low, qp, jnp.zeros_like(qp))
            outs.append(_softmax_pv([_dot_t(qm, kp)], [vp]))
        on_ref[:, sl] = jnp.where(low, outs[0], outs[1]).astype(BF16)


def _prompt_attn(q, k, v, nq, nk, nv):
    head3 = lambda n: pl.BlockSpec((n, PROMPT_LEN, LANES), lambda b: (0, b, 0))
    row = pl.BlockSpec((PROMPT_LEN, NA_WIDTH), lambda b: (b, 0))
    return pl.pallas_call(
        _prompt_attn_kernel,
        grid=(N_PROMPT,),
        in_specs=[head3(MLA_HEADS), head3(MLA_HEADS), head3(MLA_HEADS // 2), row, row, row],
        out_specs=[head3(MLA_HEADS // 2), row],
        out_shape=[jax.ShapeDtypeStruct((MLA_HEADS // 2, T_PROMPT, LANES), BF16),
                   jax.ShapeDtypeStruct((T_PROMPT, NA_WIDTH), BF16)],
        compiler_params=_cparams(("arbitrary",)),
        name="prompt_attn",
    )(q, k, v, nq, nk, nv)


def _sample_mla_kernel(q_ref, k_ref, v_ref, kc_ref, vc_ref, o_ref):
    low = _lane_is_low((TM, LANES))
    for t in range(TILES_PER_SAMPLE):
        rows = slice(t * TM, (t + 1) * TM)
        outs = []
        for hh in range(2):
            q = q_ref[hh, rows, :]
            s_lat = _dot_t(q, k_ref[hh])
            s_ctx = _dot_t(q, kc_ref[0, 0, hh])
            outs.append(_softmax_pv([s_lat, s_ctx], [v_ref[0], vc_ref[0, 0, 0]]))
        o_ref[0, rows, :] = jnp.where(low, outs[0], outs[1]).astype(BF16)


def _sample_mla(layer, q, k, v, kc, vc):
    seq_blk0 = T_PROMPT // SAMPLE_LEN
    return pl.pallas_call(
        _sample_mla_kernel,
        grid=(N_SAMPLE, MLA_HEADS // 2),
        in_specs=[
            pl.BlockSpec((2, SAMPLE_LEN, LANES), lambda b, p: (p, seq_blk0 + b, 0)),
            pl.BlockSpec((2, SAMPLE_LEN, LANES), lambda b, p: (p, seq_blk0 + b, 0)),
            pl.BlockSpec((1, SAMPLE_LEN, LANES), lambda b, p: (p, seq_blk0 + b, 0)),
            pl.BlockSpec((1, 1, 2, PAST, LANES), lambda b, p: (layer, b, p, 0, 0)),
            pl.BlockSpec((1, 1, 1, PAST, LANES), lambda b, p: (layer, b, p, 0, 0)),
        ],
        out_specs=pl.BlockSpec((1, SAMPLE_LEN, LANES), lambda b, p: (p, b, 0)),
        out_shape=jax.ShapeDtypeStruct((MLA_HEADS // 2, T_SAMPLE, LANES), BF16),
        compiler_params=_cparams(("arbitrary", "arbitrary")),
        name="sample_mla",
    )(q, k, v, kc, vc)


def _sample_na_kernel(q_ref, k_ref, v_ref, kc_ref, vc_ref, b_ref, o_ref):
    kc_t = kc_ref[0, 0].astype(BF16)
    vc_t = vc_ref[0, 0].astype(BF16)
    low = _lane_is_low((TM, LANES))
    for t in range(TILES_PER_SAMPLE):
        rows = slice(t * TM, (t + 1) * TM)
        start = _na_key_start_row(t) * GRID_W
        k_win = k_ref[start:start + NA_KEYS, :]
        v_win = v_ref[start:start + NA_KEYS, :]
        q = q_ref[rows, :]
        outs = []
        for hh in range(2):
            qm = jnp.where(low if hh == 0 else ~low, q, jnp.zeros_like(q))
            s_loc = _dot_t(qm, k_win) + b_ref[0, hh, t]
            s_ctx = _dot(qm, kc_t)
            outs.append(_softmax_pv([s_loc, s_ctx], [v_win, vc_t], feature_major=(1,)))
        o_ref[rows, :] = jnp.where(low, outs[0], outs[1]).astype(BF16)


def _sample_na(layer, nq, nk, nv, cache_k, cache_v, bias):
    seq_blk0 = T_PROMPT // SAMPLE_LEN
    return pl.pallas_call(
        _sample_na_kernel,
        grid=(NA_HEADS // 2, N_SAMPLE),
        in_specs=[
            pl.BlockSpec((SAMPLE_LEN, LANES), lambda p, b: (seq_blk0 + b, p)),
            pl.BlockSpec((SAMPLE_LEN, LANES), lambda p, b: (seq_blk0 + b, p)),
            pl.BlockSpec((SAMPLE_LEN, LANES), lambda p, b: (seq_blk0 + b, p)),
            pl.BlockSpec((1, 1, LANES, PAST), lambda p, b: (b, layer, p, 0)),
            pl.BlockSpec((1, 1, LANES, PAST), lambda p, b: (b, layer, p, 0)),
            pl.BlockSpec((1, 2, TILES_PER_SAMPLE, TM, NA_KEYS), lambda p, b: (layer, p, 0, 0, 0)),
        ],
        out_specs=pl.BlockSpec((SAMPLE_LEN, LANES), lambda p, b: (b, p)),
        out_shape=jax.ShapeDtypeStruct((T_SAMPLE, NA_WIDTH), BF16),
        compiler_params=_cparams(("arbitrary", "arbitrary")),
        name="sample_na",
    )(nq, nk, nv, cache_k, cache_v, bias)


CONV_PAD = 16


def _argmax_first(cols):
    best = cols[0]
    idx = jnp.zeros(best.shape, jnp.int32)
    for j in range(1, len(cols)):
        upd = cols[j] > best
        idx = jnp.where(upd, j, idx)
        best = jnp.where(upd, cols[j], best)
    return idx


def _route(aff, bias):
    sel = aff + bias
    rows = [sel[e:e + 1, :] for e in range(N_EXPERTS)]
    group_scores = []
    for g in range(N_GROUPS):
        c = rows[g * EPG:(g + 1) * EPG]
        best = None
        for a in range(EPG):
            for b in range(a + 1, EPG):
                s = c[a] + c[b]
                best = s if best is None else jnp.maximum(best, s)
        group_scores.append(best)
    g_idx = _argmax_first(group_scores)
    in_group = []
    for j in range(EPG):
        v = rows[j]
        for g in range(1, N_GROUPS):
            v = jnp.where(g_idx == g, rows[g * EPG + j], v)
        in_group.append(v)
    i1 = _argmax_first(in_group)
    masked = [jnp.where(i1 == j, -jnp.inf, in_group[j]) for j in range(EPG)]
    i2 = _argmax_first(masked)
    e1 = g_idx * EPG + i1
    e2 = g_idx * EPG + i2
    sub = lax.broadcasted_iota(jnp.int32, aff.shape, 0)
    w1 = jnp.sum(jnp.where(sub == e1, aff, 0.0), axis=0, keepdims=True)
    w2 = jnp.sum(jnp.where(sub == e2, aff, 0.0), axis=0, keepdims=True)
    tot = w1 + w2
    return e1, e2, w1 / tot, w2 / tot


def _sort_tile(e1, e2):
    sub = lax.broadcasted_iota(jnp.int32, (N_EXPERTS, TM), 0)
    oh1 = sub == e1
    oh2 = sub == e2
    one = jnp.ones((N_EXPERTS, TM), F32)
    cnt1 = jnp.sum(jnp.where(oh1, one, 0.0), axis=1, keepdims=True) * one
    cnt2 = jnp.sum(jnp.where(oh2, one, 0.0), axis=1, keepdims=True) * one
    tot = (cnt1 + cnt2).astype(jnp.int32)
    padded = (tot + (RUN_ALIGN - 1)) & (-RUN_ALIGN)
    ee = lax.broadcasted_iota(jnp.int32, (N_EXPERTS, N_EXPERTS), 0)
    ef = lax.broadcasted_iota(jnp.int32, (N_EXPERTS, N_EXPERTS), 1)
    start = _dot((ef < ee).astype(BF16), padded.astype(BF16))
    ta = lax.broadcasted_iota(jnp.int32, (TM, TM), 0)
    tb = lax.broadcasted_iota(jnp.int32, (TM, TM), 1)
    before = (ta < tb).astype(BF16)
    rank1 = _dot(oh1.astype(BF16), before)
    rank2 = _dot(oh2.astype(BF16), before)
    pos1 = jnp.sum(jnp.where(oh1, start + rank1, 0.0), axis=0, keepdims=True)
    pos2 = jnp.sum(jnp.where(oh2, start + cnt1 + rank2, 0.0), axis=0, keepdims=True)
    return pos1.astype(jnp.int32), pos2.astype(jnp.int32), start.astype(jnp.int32), padded


def _post_kernel(x_ref, omp_ref, oms_ref, up_ref, uc_ref, un_ref, onp_ref, ons_ref, mod_ref, cw_ref, cb_ref,
                 lg_ref, lb_ref, wo_ref, n2g_ref, wr_ref, br_ref, x1_out, xs_out, pc_out, st_out, pd_out,
                 wo_bf, ext_ref, shift_ref, h2_buf, logit_buf):
    s = pl.program_id(0)
    i = jnp.minimum(s, N_TILES - 1)

    @pl.when(s == 0)
    def _():
        wo_bf[...] = wo_ref[0].astype(BF16)
        h2_buf[...] = jnp.zeros(h2_buf.shape, BF16)
        logit_buf[...] = jnp.zeros(logit_buf.shape, F32)

    pos_in_seq = (i - PROMPT_TILES) % TILES_PER_SAMPLE
    is_first = (i < PROMPT_TILES) | (pos_in_seq == 0)
    is_last = (i < PROMPT_TILES) | (pos_in_seq == TILES_PER_SAMPLE - 1)
    ext_ref[0:CONV_PAD, :] = jnp.where(is_first, 0.0, up_ref[TM - CONV_PAD:TM, :])
    ext_ref[CONV_PAD:CONV_PAD + TM, :] = uc_ref[...]
    ext_ref[CONV_PAD + TM:2 * CONV_PAD + TM, :] = jnp.where(is_last, 0.0, un_ref[0:CONV_PAD, :])
    acc = None
    first_off = CONV_PAD - CONV_K // 2
    for b in range(SUBLANES):
        taps = [j for j in range(CONV_K) if (first_off + j) % SUBLANES == b]
        if not taps:
            continue
        reach = max(first_off + j - b for j in taps)
        if b:
            shift_ref[b, 0:TM + reach, :] = ext_ref[b:b + TM + reach, :]
        for j in taps:
            a8 = first_off + j - b
            window = shift_ref[b, a8:a8 + TM, :] if b else ext_ref[a8:a8 + TM, :]
            term = window * cw_ref[0, j:j + 1, :]
            acc = term if acc is None else acc + term
    conv = acc + cb_ref[0]
    wide = lambda t: jnp.concatenate([t] * (CONV_CH // LANES), axis=1)
    cc = conv - wide(_lane_sums(conv, CONV_CH) * (1.0 / CONV_CH))
    var = _lane_sums(cc * cc, CONV_CH) * (1.0 / CONV_CH)
    o_conv = _silu(cc * wide(lax.rsqrt(var + EPS)) * lg_ref[0] + lb_ref[0])

    is_prompt = i < PROMPT_TILES
    mix = jnp.concatenate([jnp.where(is_prompt, omp_ref[p], oms_ref[p]) for p in range(MLA_HEADS // 2)]
                          + [o_conv.astype(BF16), jnp.where(is_prompt, onp_ref[...], ons_ref[...])],
                          axis=-1)
    att = _dot(mix, wo_bf[...])
    gate1 = mod_ref[0, :, 2 * D:3 * D]
    x1 = x_ref[...] + gate1 * att
    x1_out[...] = x1
    shift2 = mod_ref[0, :, 3 * D:4 * D]
    scale2 = mod_ref[0, :, 4 * D:5 * D]
    h2 = _rms(x1, D) * n2g_ref[0] * (1.0 + scale2) + shift2

    h2_hi = h2.astype(BF16)
    h2_lo = (h2 - h2_hi.astype(F32)).astype(BF16)
    wr = wr_ref[...]
    wr_hi = wr.astype(BF16)
    wr_lo = (wr - wr_hi.astype(F32)).astype(BF16)
    logits = _dot_t(wr_hi, h2_hi) + _dot_t(wr_hi, h2_lo) + _dot_t(wr_lo, h2_hi)
    h2_prev = h2_buf[(s + 1) % 2]
    logits_prev = logit_buf[(s + 1) % 2]
    h2_buf[s % 2] = h2_hi
    logit_buf[s % 2] = logits

    e1, e2, w1, w2 = _route(jax.nn.sigmoid(logits_prev), br_ref[...])
    pos1, pos2, start, padded = _sort_tile(e1, e2)
    st_out[0] = start[:, 0:LANES]
    pd_out[0] = padded[:, 0:LANES]
    r = lax.broadcasted_iota(jnp.int32, (TILE_CAP, TM), 0)
    perm = ((r == pos1) | (r == pos2)).astype(BF16)
    xs_out[...] = _dot(perm, h2_prev).astype(BF16)
    sub = lax.broadcasted_iota(jnp.int32, (LANES, TM), 0)
    packed = jnp.where(sub == 0, pos1.astype(F32),
                       jnp.where(sub == 1, pos2.astype(F32),
                                 jnp.where(sub == 2, w1, jnp.where(sub == 3, w2, 0.0))))
    pc_out[...] = packed.T


def _post(layer, x, om_p, om_s, u, on_p, on_s, mods3, prm):
    cur = lambda s: jnp.minimum(s, N_TILES - 1)
    prev = lambda s: jnp.maximum(s - 1, 0)
    row = lambda s: (cur(s), 0)
    row2 = lambda s: (prev(s), 0)
    lay3 = lambda s: (layer, 0, 0)
    return pl.pallas_call(
        _post_kernel,
        grid=(N_TILES + 1,),
        in_specs=[
            pl.BlockSpec((TM, D), row),
            pl.BlockSpec((MLA_HEADS // 2, TM, LANES), lambda s: (0, jnp.minimum(s, PROMPT_TILES - 1), 0)),
            pl.BlockSpec((MLA_HEADS // 2, TM, LANES), lambda s: (0, jnp.maximum(cur(s) - PROMPT_TILES, 0), 0)),
            pl.BlockSpec((TM, CONV_CH), lambda s: (jnp.maximum(cur(s) - 1, 0), 0)),
            pl.BlockSpec((TM, CONV_CH), row),
            pl.BlockSpec((TM, CONV_CH), lambda s: (jnp.minimum(s + 1, N_TILES - 1), 0)),
            pl.BlockSpec((TM, NA_WIDTH), lambda s: (jnp.minimum(s, PROMPT_TILES - 1), 0)),
            pl.BlockSpec((TM, NA_WIDTH), lambda s: (jnp.maximum(cur(s) - PROMPT_TILES, 0), 0)),
            pl.BlockSpec((1, 1, MOD_W), lambda s: (layer * COND_ROWS + _mod_row(cur(s)), 0, 0)),
            pl.BlockSpec((1, CONV_K, CONV_CH), lay3),
            pl.BlockSpec((1, 1, CONV_CH), lay3),
            pl.BlockSpec((1, 1, CONV_CH), lay3),
            pl.BlockSpec((1, 1, CONV_CH), lay3),
            pl.BlockSpec((1, D, D), lay3),
            pl.BlockSpec((1, 1, D), lay3),
            pl.BlockSpec((N_EXPERTS, D), lambda s: (0, 0)),
            pl.BlockSpec((N_EXPERTS, TM), lambda s: (0, 0)),
        ],
        out_specs=[pl.BlockSpec((TM, D), row), pl.BlockSpec((TILE_CAP, D), row2),
                   pl.BlockSpec((TM, LANES), row2),
                   pl.BlockSpec((1, N_EXPERTS, LANES), lambda s: (prev(s), 0, 0)),
                   pl.BlockSpec((1, N_EXPERTS, LANES), lambda s: (prev(s), 0, 0))],
        out_shape=[jax.ShapeDtypeStruct((T_ALL, D), F32),
                   jax.ShapeDtypeStruct((N_TILES * TILE_CAP, D), BF16),
                   jax.ShapeDtypeStruct((T_ALL, LANES), F32),
                   jax.ShapeDtypeStruct((N_TILES, N_EXPERTS, LANES), jnp.int32),
                   jax.ShapeDtypeStruct((N_TILES, N_EXPERTS, LANES), jnp.int32)],
        scratch_shapes=[pltpu.VMEM((D, D), BF16), pltpu.VMEM((TM + 2 * CONV_PAD, CONV_CH), F32),
                        pltpu.VMEM((SUBLANES, TM + 2 * CONV_PAD, CONV_CH), F32),
                        pltpu.VMEM((2, TM, D), BF16), pltpu.VMEM((2, N_EXPERTS, TM), F32)],
        compiler_params=_cparams(("arbitrary",)),
        name="post_l%d" % layer,
    )(x, om_p, om_s, u, u, u, on_p, on_s, mods3, prm["conv_w"], prm["conv_b"], prm["conv_ln_g"], prm["conv_ln_b"],
      prm["w_out"], prm["norm2_g"], prm["w_router_t"], prm["b_router_t"])


def _for_each_piece(n, fn):
    def pieces(sizes):
        for sz in sizes:
            @pl.when((n & sz) != 0)
            def _():
                fn(n & (-2 * sz), sz)

    n_big = len(RUN_SIZES) - 2

    @pl.when(n >= RUN_SIZES[n_big - 1])
    def _():
        pieces(RUN_SIZES[:n_big])

    pieces(RUN_SIZES[n_big:])


def _expert_kernel(st_ref, pd_ref, xs_hbm, wg_ref, wu_ref, wd_ref, ys_hbm, xy, wg_bf, wu_bf, wd_bf, zeros_buf,
                   sems):
    e = pl.program_id(0)
    slot = e % 2
    other = 1 - slot
    sem_tail = sems.at[4]

    def start_all(ex, sl, inbound):
        def body(j, cur):
            start = st_ref[j * N_EXPERTS + ex]
            n = pd_ref[j * N_EXPERTS + ex]

            def piece(off, sz):
                rows_hbm = pl.ds(pl.multiple_of(j * TILE_CAP + start + off, RUN_ALIGN), sz)
                rows_buf = pl.ds(pl.multiple_of(cur + off, RUN_ALIGN), sz)
                if inbound:
                    pltpu.make_async_copy(xs_hbm.at[rows_hbm], xy.at[sl, rows_buf], sems.at[sl]).start()
                else:
                    pltpu.make_async_copy(xy.at[sl, rows_buf], ys_hbm.at[rows_hbm], sems.at[2 + sl]).start()

            _for_each_piece(n, piece)
            return cur + n
        lax.fori_loop(0, N_TILES, body, 0)

    def expert_rows(ex):
        return lax.fori_loop(0, N_TILES, lambda j, c: c + pd_ref[j * N_EXPERTS + ex], 0, unroll=True)

    def wait_all(ex, sl, inbound):
        total = expert_rows(ex)
        for sz in WAIT_SIZES:
            @pl.when((total & sz) != 0)
            def _():
                rows = pl.ds(0, sz)
                if inbound:
                    pltpu.make_async_copy(xs_hbm.at[rows], xy.at[sl, rows], sems.at[sl]).wait()
                else:
                    pltpu.make_async_copy(xy.at[sl, rows], ys_hbm.at[rows], sems.at[2 + sl]).wait()
        return total

    def compute_chunks(first, last):
        def chunk(c, carry):
            rows = pl.ds(pl.multiple_of(c * MOE_CHUNK, MOE_CHUNK), MOE_CHUNK)
            x = xy[slot, rows, :]
            hid = _silu(_dot(x, wg_bf[...])) * _dot(x, wu_bf[...])
            xy[slot, rows, :] = _dot(hid.astype(BF16), wd_bf[...]).astype(BF16)
            return carry
        lax.fori_loop(first, last, chunk, 0)

    @pl.when(e == 0)
    def _():
        xy[...] = jnp.zeros(xy.shape, BF16)
        zeros_buf[...] = jnp.zeros(zeros_buf.shape, BF16)
        start_all(e, slot, True)

    wg_bf[...] = wg_ref[0, 0].astype(BF16)
    wu_bf[...] = wu_ref[0, 0].astype(BF16)
    wd_bf[...] = wd_ref[0, 0].astype(BF16)
    n_chunks = (wait_all(e, slot, True) + (MOE_CHUNK - 1)) // MOE_CHUNK

    mid = (n_chunks + 1) // 2
    compute_chunks(0, mid)

    @pl.when(e > 0)
    def _():
        wait_all(e - 1, other, False)

    @pl.when(e < N_EXPERTS - 1)
    def _():
        start_all(e + 1, other, True)

    compute_chunks(mid, n_chunks)
    start_all(e, slot, False)

    @pl.when(e == N_EXPERTS - 1)
    def _():
        wait_all(e, slot, False)

        def tail_copy(j, off, sz):
            used = st_ref[j * N_EXPERTS + e] + pd_ref[j * N_EXPERTS + e]
            rows_hbm = pl.ds(pl.multiple_of(j * TILE_CAP + used + off, RUN_ALIGN), sz)
            return pltpu.make_async_copy(zeros_buf.at[pl.ds(0, sz)], ys_hbm.at[rows_hbm], sem_tail)

        def tail_len(j):
            return TILE_CAP - (st_ref[j * N_EXPERTS + e] + pd_ref[j * N_EXPERTS + e])

        def t_start(j, c):
            _for_each_piece(tail_len(j), lambda off, sz: tail_copy(j, off, sz).start())
            return c

        def t_wait(j, c):
            _for_each_piece(tail_len(j), lambda off, sz: tail_copy(j, off, sz).wait())
            return c

        lax.fori_loop(0, N_TILES, t_start, 0)
        lax.fori_loop(0, N_TILES, t_wait, 0)


def _experts(layer, starts, pads, xs, prm):
    wspec = lambda shape: pl.BlockSpec((1, 1) + shape, lambda e, st, pd: (layer, e, 0, 0))
    return pl.pallas_call(
        _expert_kernel,
        grid_spec=pltpu.PrefetchScalarGridSpec(
            num_scalar_prefetch=2,
            grid=(N_EXPERTS,),
            in_specs=[pl.BlockSpec(memory_space=pl.ANY), wspec((D, EXPERT_FF)), wspec((D, EXPERT_FF)),
                      wspec((EXPERT_FF, D))],
            out_specs=pl.BlockSpec(memory_space=pl.ANY),
            scratch_shapes=[pltpu.VMEM((2, XY_ROWS, D), BF16), pltpu.VMEM((D, EXPERT_FF), BF16),
                            pltpu.VMEM((D, EXPERT_FF), BF16), pltpu.VMEM((EXPERT_FF, D), BF16),
                            pltpu.VMEM((RUN_SIZES[0], D), BF16),
                            pltpu.SemaphoreType.DMA((5,))]),
        out_shape=jax.ShapeDtypeStruct((N_TILES * TILE_CAP, D), BF16),
        compiler_params=pltpu.CompilerParams(dimension_semantics=("arbitrary",),
                                             vmem_limit_bytes=EXPERT_VMEM_LIMIT),
        name="experts_l%d" % layer,
    )(starts, pads, xs, prm["w_e_gate"], prm["w_e_up"], prm["w_e_down"])


def _final_kernel(x1_ref, ys_ref, pc_ref, mod_ref, op_ref, os_ref):
    x = _moe_combine(x1_ref, ys_ref, pc_ref, mod_ref)
    i = pl.program_id(0)

    @pl.when(i < PROMPT_TILES)
    def _():
        op_ref[...] = x

    @pl.when(i >= PROMPT_TILES)
    def _():
        os_ref[...] = x


def _final(x1, ys, pcol, mods3):
    row = lambda i: (i, 0)
    return pl.pallas_call(
        _final_kernel,
        grid=(N_TILES,),
        in_specs=[pl.BlockSpec((TM, D), row), pl.BlockSpec((TILE_CAP, D), row), pl.BlockSpec((TM, LANES), row),
                  pl.BlockSpec((1, 1, MOD_W), lambda i: ((DEPTH - 1) * COND_ROWS + _mod_row(i), 0, 0))],
        out_specs=[pl.BlockSpec((TM, D), lambda i: (jnp.minimum(i, PROMPT_TILES - 1), 0)),
                   pl.BlockSpec((TM, D), lambda i: (jnp.maximum(i - PROMPT_TILES, 0), 0))],
        out_shape=[jax.ShapeDtypeStruct((T_PROMPT, D), F32), jax.ShapeDtypeStruct((T_SAMPLE, D), F32)],
        compiler_params=_cparams(("arbitrary",)),
        name="final_residual",
    )(x1, ys, pcol, mods3)


def _rope_tables():
    t = np.arange(SAMPLE_LEN)
    n_freq = QK_ROPE // 4
    inv_freq = ROPE_THETA ** (-np.arange(n_freq, dtype=np.float32) / n_freq)
    row = (t // GRID_W).astype(np.float32)
    col = (t % GRID_W).astype(np.float32)
    ang = jnp.concatenate([jnp.asarray(row)[:, None] * inv_freq, jnp.asarray(col)[:, None] * inv_freq],
                          axis=-1)
    cos_p = jnp.repeat(jnp.cos(ang), 2, axis=-1)
    sin_p = jnp.repeat(jnp.sin(ang), 2, axis=-1) * jnp.tile(jnp.asarray([-1.0, 1.0], F32), QK_ROPE // 2)
    pad_l = jnp.ones((SAMPLE_LEN, QK_NOPE), F32)
    pad_r = jnp.ones((SAMPLE_LEN, LANES - QK_HEAD), F32)
    cos = jnp.concatenate([pad_l, cos_p, pad_r], axis=-1)
    sin = jnp.concatenate([0 * pad_l, sin_p, 0 * pad_r], axis=-1)
    ident_c = jnp.ones((TM, LANES), F32)
    return jnp.concatenate([ident_c, cos], axis=0), jnp.concatenate([0 * ident_c, sin], axis=0)


def _pad_lanes(x, lo, total):
    pad = [(0, 0)] * (x.ndim - 1) + [(lo, total - lo - x.shape[-1])]
    return jnp.pad(x, pad)


def kernel(x_prompt, x_sample, cache_mla_ckv, cache_mla_krope, cache_na_k, cache_na_v, c, c_ctx, w_ada, b_ada, norm1_g, norm2_g, w_in, q_a_g, kv_a_g, w_q_up, w_kv_up, mla_q_norm_g, mla_k_norm_g, conv_dw_w, conv_dw_b, conv_ln_g, conv_ln_b, na_q_norm_g, na_k_norm_g, na_rpb, w_out, w_router, b_router, w_e_gate, w_e_up, w_e_down):
    cos, sin = _rope_tables()
    vec = lambda a: a.reshape(DEPTH, 1, a.shape[-1])
    prm = {
        "norm1_g": vec(norm1_g), "norm2_g": vec(norm2_g), "w_in_t": jnp.swapaxes(w_in, 1, 2), "q_a_g": vec(q_a_g),
        "kv_a_g": vec(kv_a_g), "w_q_up": w_q_up, "w_kv_up": w_kv_up,
        "q_norm_g": vec(_pad_lanes(mla_q_norm_g, 0, LANES)), "k_norm_g": vec(_pad_lanes(mla_k_norm_g, 0, LANES)),
        "na_q_g": vec(jnp.tile(na_q_norm_g, (1, 2))), "na_k_g": vec(jnp.tile(na_k_norm_g, (1, 2))),
        "cos": cos, "sin": sin,
        "conv_w": conv_dw_w, "conv_b": vec(conv_dw_b), "conv_ln_g": vec(conv_ln_g), "conv_ln_b": vec(conv_ln_b),
        "w_out": w_out, "w_router_t": w_router.T,
        "b_router_t": jnp.broadcast_to(b_router[:, None], (N_EXPERTS, TM)),
        "w_e_gate": w_e_gate, "w_e_up": w_e_up, "w_e_down": w_e_down,
    }

    cond = jnp.concatenate([c_ctx[None], c, jnp.zeros((COND_ROWS - N_COND, D), F32)], axis=0)
    mods3 = _modulation(cond.T, w_ada, b_ada).reshape(DEPTH * COND_ROWS, 1, MOD_W)

    kc, vc = _ctx_kv(cache_mla_ckv, jnp.swapaxes(cache_mla_krope, 2, 3), prm)
    bias = _na_bias(na_rpb)
    cache_k = cache_na_k.transpose(0, 1, 3, 4, 2).reshape(N_SAMPLE, DEPTH, NA_WIDTH, PAST)
    cache_v = cache_na_v.transpose(0, 1, 3, 4, 2).reshape(N_SAMPLE, DEPTH, NA_WIDTH, PAST)

    x = x1 = ys = pcol = None
    new_ckv, new_kr, new_nak, new_nav = [], [], [], []
    for layer in range(DEPTH):
        if layer == 0:
            outs = _pre(layer, False, (x_prompt.reshape(T_PROMPT, D), x_sample.reshape(T_SAMPLE, D)), mods3, prm)
        else:
            outs = _pre(layer, True, (x1, ys, pcol), mods3, prm)
        x, q, k, v, ckv_n, kr, u, nq, nk, nv, nkf, nvf = outs
        om_p, on_p = _prompt_attn(q, k, v, nq, nk, nv)
        om_s = _sample_mla(layer, q, k, v, kc, vc)
        on_s = _sample_na(layer, nq, nk, nv, cache_k, cache_v, bias)
        x1, xs, pcol, run_start, run_len = _post(layer, x, om_p, om_s, u, on_p, on_s, mods3, prm)
        ys = _experts(layer, run_start[:, :, 0].reshape(-1), run_len[:, :, 0].reshape(-1), xs, prm)
        new_ckv.append(ckv_n.reshape(N_PROMPT, PROMPT_LEN, KV_LORA))
        new_kr.append(kr)
        new_nak.append(nkf)
        new_nav.append(nvf)
    y_prompt, y_sample = _final(x1, ys, pcol, mods3)
    return (y_prompt.reshape(N_PROMPT, PROMPT_LEN, D), y_sample.reshape(N_SAMPLE, SAMPLE_LEN, D),
            jnp.stack(new_ckv, axis=1), jnp.stack(new_kr, axis=1).transpose(0, 1, 3, 2),
            jnp.stack(new_nak, axis=1).transpose(0, 1, 4, 2, 3), jnp.stack(new_nav, axis=1).transpose(0, 1, 4, 2, 3))
```
